```python
import math
import jax, jax.numpy as jnp
from jax import lax
import numpy as np

D_MODEL = 2048
BATCH = 2
SEQ = 16384
DEPTH = 2

NSA_HEADS = 16
NSA_GROUPS = 4
NSA_HEAD_DIM = 64
CMP_BLOCK = 32
CMP_STRIDE = 16
CMP_HIDDEN = 256
SLC_BLOCK = 64
SLC_TOPN = 16
WINDOW = 512
Q_BLOCK = 128
RET_HEADS = 8
RET_HEAD_DIM = 128
RET_CHUNK = 128
ROPE_BASE = 10000.0
D_FF = 5504
ALPHA = (2 * DEPTH) ** 0.25
BETA = (8 * DEPTH) ** -0.25
LN_EPS = 1e-5
NEG = -1e30
FORCE = 1e30
TINY = 1e-30
NSA_Q = NSA_HEADS * NSA_HEAD_DIM
NSA_KV = NSA_GROUPS * NSA_HEAD_DIM
NSA_GATE = 3 * NSA_HEADS
RET_W = RET_HEADS * RET_HEAD_DIM
IN_WIDTHS = (NSA_Q, NSA_KV, NSA_KV, NSA_KV, NSA_KV, NSA_KV, NSA_KV, NSA_GATE, RET_W, RET_W, RET_W, RET_W)
D_IN = NSA_Q + 6 * NSA_KV + NSA_GATE + 4 * RET_W

kernel_name = "nsa_retention_macaron_deepnorm"


def layer_norm(x, g, b):
    xf = x.astype(jnp.float32)
    mu = jnp.mean(xf, -1, keepdims=True)
    var = jnp.mean(jnp.square(xf - mu), -1, keepdims=True)
    return ((xf - mu) * lax.rsqrt(var + LN_EPS) * g + b).astype(x.dtype)


def swiglu(x, w_gate, w_up, w_down):
    return (jax.nn.silu(x @ w_gate) * (x @ w_up)) @ w_down


def masked_softmax(s, mask):
    s = jnp.where(mask, s.astype(jnp.float32), NEG)
    m = jnp.max(s, -1, keepdims=True)
    e = jnp.where(mask, jnp.exp(s - m), 0.0)
    return e / jnp.maximum(jnp.sum(e, -1, keepdims=True), TINY)


def compress_tokens(kv, pe, w1, b1, w2):
    B, S, G, hd = kv.shape
    n_cmp = (S - CMP_BLOCK) // CMP_STRIDE + 1
    idx = np.arange(n_cmp)[:, None] * CMP_STRIDE + np.arange(CMP_BLOCK)[None, :]
    blocks = kv[:, idx] + pe[:, None, :]
    blocks = jnp.transpose(blocks, (0, 3, 1, 2, 4)).reshape(B, G, n_cmp, CMP_BLOCK * hd)
    return jax.nn.gelu(blocks @ w1 + b1) @ w2


def cmp_to_slc_matrix(n_cmp, n_slc):
    c0 = np.arange(n_cmp) * CMP_STRIDE
    c1 = c0 + CMP_BLOCK
    s0 = np.arange(n_slc) * SLC_BLOCK
    s1 = s0 + SLC_BLOCK
    ov = (c0[:, None] < s1[None, :]) & (c1[:, None] > s0[None, :])
    return jnp.asarray(ov, jnp.float32)


def nsa_attention(q, gates, kc, vc, k_slc, v_slc, k_win, v_win):
    B, S, H, hd = q.shape
    G = kc.shape[1]
    R = H // G
    scale = hd ** -0.5
    qg = q.reshape(B, S, G, R, hd)
    gg = gates.reshape(B, S, G, R, 3)
    n_cmp = kc.shape[2]
    n_slc = S // SLC_BLOCK
    n_sel = min(SLC_TOPN, n_slc)
    ov = cmp_to_slc_matrix(n_cmp, n_slc)
    cmp_end = jnp.arange(n_cmp) * CMP_STRIDE + CMP_BLOCK - 1
    blk = jnp.arange(n_slc)
    ks_blocks = jnp.transpose(k_slc.reshape(B, n_slc, SLC_BLOCK, G, hd), (0, 3, 1, 2, 4))
    vs_blocks = jnp.transpose(v_slc.reshape(B, n_slc, SLC_BLOCK, G, hd), (0, 3, 1, 2, 4))
    pad = ((0, 0), (WINDOW, 0), (0, 0), (0, 0))
    kw_pad = jnp.pad(k_win, pad)
    vw_pad = jnp.pad(v_win, pad)
    gather = jax.vmap(jax.vmap(lambda bl, ix: bl[ix]))

    def block_fn(c):
        t0 = c * Q_BLOCK
        qb = lax.dynamic_slice_in_dim(qg, t0, Q_BLOCK, axis=1)
        gb = lax.dynamic_slice_in_dim(gg, t0, Q_BLOCK, axis=1)
        tpos = t0 + jnp.arange(Q_BLOCK)
        s = jnp.einsum('bqgrd,bgnd->bgrqn', qb, kc) * scale
        p_c = masked_softmax(s, cmp_end[None, :] <= tpos[:, None])
        o_c = jnp.einsum('bgrqn,bgnd->bqgrd', p_c.astype(vc.dtype), vc)
        imp = jnp.einsum('bgrqn,nj->bgqj', p_c, ov)
        cur = tpos // SLC_BLOCK
        forced = (blk[None, :] == 0) | (blk[None, :] == cur[:, None]) | (blk[None, :] == cur[:, None] - 1)
        valid = blk[None, :] <= cur[:, None]
        imp = jnp.where(forced, FORCE, jnp.where(valid, imp, NEG))
        _, sel = lax.top_k(imp, n_sel)
        kg = gather(ks_blocks, sel).reshape(B, G, Q_BLOCK, n_sel * SLC_BLOCK, hd)
        vg = gather(vs_blocks, sel).reshape(B, G, Q_BLOCK, n_sel * SLC_BLOCK, hd)
        kpos = (sel[..., None] * SLC_BLOCK + jnp.arange(SLC_BLOCK)).reshape(B, G, Q_BLOCK, n_sel * SLC_BLOCK)
        mask_s = (kpos <= tpos[None, None, :, None])[:, :, None]
        s = jnp.einsum('bqgrd,bgqkd->bgrqk', qb, kg) * scale
        p_s = masked_softmax(s, mask_s)
        o_s = jnp.einsum('bgrqk,bgqkd->bqgrd', p_s.astype(vg.dtype), vg)
        kw = lax.dynamic_slice_in_dim(kw_pad, t0, Q_BLOCK + WINDOW, axis=1)
        vw = lax.dynamic_slice_in_dim(vw_pad, t0, Q_BLOCK + WINDOW, axis=1)
        kwpos = t0 - WINDOW + jnp.arange(Q_BLOCK + WINDOW)
        diff = tpos[:, None] - kwpos[None, :]
        mask_w = (diff >= 0) & (diff < WINDOW) & (kwpos[None, :] >= 0)
        s = jnp.einsum('bqgrd,bkgd->bgrqk', qb, kw) * scale
        p_w = masked_softmax(s, mask_w)
        o_w = jnp.einsum('bgrqk,bkgd->bqgrd', p_w.astype(vw.dtype), vw)
        out = gb[..., 0:1] * o_c + gb[..., 1:2] * o_s + gb[..., 2:3] * o_w
        return out.reshape(B, Q_BLOCK, H * hd)

    outs = lax.map(block_fn, jnp.arange(S // Q_BLOCK))
    return jnp.transpose(outs, (1, 0, 2, 3)).reshape(B, S, H * hd)


def rotary(x, pos):
    d = x.shape[-1]
    inv = ROPE_BASE ** (-jnp.arange(0, d, 2, dtype=jnp.float32) / d)
    ang = pos[:, None].astype(jnp.float32) * inv[None, :]
    cos, sin = jnp.cos(ang)[:, None, :], jnp.sin(ang)[:, None, :]
    x1, x2 = x[..., : d // 2], x[..., d // 2:]
    return jnp.concatenate([x1 * cos - x2 * sin, x1 * sin + x2 * cos], -1)


def retention(q, k, v):
    B, S, H, dk = q.shape
    dv = v.shape[-1]
    C = RET_CHUNK
    n = S // C
    pos = jnp.arange(S)
    q = rotary(q.astype(jnp.float32), pos)
    k = rotary(k.astype(jnp.float32), pos) * dk ** -0.5
    v = v.astype(jnp.float32)
    log_g = jnp.log1p(-jnp.exp2(-5.0 - jnp.arange(H, dtype=jnp.float32)))
    qc, kc, vc = q.reshape(B, n, C, H, dk), k.reshape(B, n, C, H, dk), v.reshape(B, n, C, H, dv)
    i = jnp.arange(C, dtype=jnp.float32)
    diff = i[:, None] - i[None, :]
    dmat = jnp.where(diff >= 0, jnp.exp(jnp.maximum(diff, 0.0)[None] * log_g[:, None, None]), 0.0)
    inner = jnp.einsum('bnihd,bnjhd->bnhij', qc, kc) * dmat
    y_inner = jnp.einsum('bnhij,bnjhe->bnihe', inner, vc)
    kdec = jnp.exp((C - 1 - i)[None, :] * log_g[:, None])
    chunk_kv = jnp.einsum('bnjhd,hj,bnjhe->bnhde', kc, kdec, vc)
    chunk_decay = jnp.exp(C * log_g)[:, None, None]

    def step(state, kv):
        return state * chunk_decay + kv, state

    _, r_prev = lax.scan(step, jnp.zeros((B, H, dk, dv), jnp.float32), jnp.transpose(chunk_kv, (1, 0, 2, 3, 4)))
    r_prev = jnp.transpose(r_prev, (1, 0, 2, 3, 4))
    qdec = jnp.exp((i + 1)[None, :] * log_g[:, None])
    y_cross = jnp.einsum('bnihd,hi,bnhde->bnihe', qc, qdec, r_prev)
    return (y_inner + y_cross).reshape(B, S, H, dv)


def head_group_norm(y, g, b):
    B, S = y.shape[:2]
    mu = jnp.mean(y, -1, keepdims=True)
    var = jnp.mean(jnp.square(y - mu), -1, keepdims=True)
    return ((y - mu) * lax.rsqrt(var + LN_EPS)).reshape(B, S, -1) * g + b


def hybrid_mixer(h, w_in, cmp_k_pe, cmp_k_w1, cmp_k_b1, cmp_k_w2, cmp_v_pe, cmp_v_w1, cmp_v_b1, cmp_v_w2,
                 ret_gn_g, ret_gn_b, w_merge_gate, w_proj_a, w_proj_b, w_o):
    B, S, _ = h.shape
    split_pts = np.cumsum(IN_WIDTHS)[:-1].tolist()
    q_a, k_c, v_c, k_s, v_s, k_w, v_w, g_a, q_b, k_b, v_b, g_b = jnp.split(h @ w_in, split_pts, axis=-1)
    kvh = lambda t: t.reshape(B, S, NSA_GROUPS, NSA_HEAD_DIM)
    kc = compress_tokens(kvh(k_c), cmp_k_pe, cmp_k_w1, cmp_k_b1, cmp_k_w2)
    vc = compress_tokens(kvh(v_c), cmp_v_pe, cmp_v_w1, cmp_v_b1, cmp_v_w2)
    o_a = nsa_attention(q_a.reshape(B, S, NSA_HEADS, NSA_HEAD_DIM),
                        jax.nn.sigmoid(g_a).reshape(B, S, NSA_HEADS, 3),
                        kc, vc, kvh(k_s), kvh(v_s), kvh(k_w), kvh(v_w))
    rh = lambda t: t.reshape(B, S, RET_HEADS, RET_HEAD_DIM)
    y_b = retention(rh(q_b), rh(k_b), rh(v_b))
    o_b = jax.nn.silu(g_b) * head_group_norm(y_b, ret_gn_g, ret_gn_b).astype(h.dtype)
    gate_a, gate_b = jnp.split(jax.nn.sigmoid(h @ w_merge_gate), 2, axis=-1)
    merged = gate_a * (o_a @ w_proj_a) + gate_b * (o_b @ w_proj_b)
    return merged @ w_o


def setup_inputs(seed: int = 0) -> dict:
    key = jax.random.key(seed)
    ks = iter(jax.random.split(key, 40))
    L, D, F, hd = DEPTH, D_MODEL, D_FF, NSA_HEAD_DIM
    nrm = lambda shape, s: jax.random.normal(next(ks), shape, jnp.float32) * s
    gain = lambda shape: 1.0 + nrm(shape, 0.02)
    return {
        "x": nrm((BATCH, SEQ, D), 1.0),
        "ffn1_w_gate": nrm((L, D, F), D ** -0.5),
        "ffn1_w_up": nrm((L, D, F), D ** -0.5),
        "ffn1_w_down": nrm((L, F, D), BETA * F ** -0.5),
        "ln1_g": gain((L, D)),
        "ln1_b": nrm((L, D), 0.02),
        "w_in": nrm((L, D, D_IN), D ** -0.5),
        "cmp_k_pe": nrm((L, CMP_BLOCK, hd), 0.02),
        "cmp_k_w1": nrm((L, CMP_BLOCK * hd, CMP_HIDDEN), (CMP_BLOCK * hd) ** -0.5),
        "cmp_k_b1": nrm((L, CMP_HIDDEN), 0.02),
        "cmp_k_w2": nrm((L, CMP_HIDDEN, hd), CMP_HIDDEN ** -0.5),
        "cmp_v_pe": nrm((L, CMP_BLOCK, hd), 0.02),
        "cmp_v_w1": nrm((L, CMP_BLOCK * hd, CMP_HIDDEN), (CMP_BLOCK * hd) ** -0.5),
        "cmp_v_b1": nrm((L, CMP_HIDDEN), 0.02),
        "cmp_v_w2": nrm((L, CMP_HIDDEN, hd), CMP_HIDDEN ** -0.5),
        "ret_gn_g": gain((L, RET_W)),
        "ret_gn_b": nrm((L, RET_W), 0.02),
        "w_merge_gate": nrm((L, D, 2 * D), D ** -0.5),
        "w_proj_a": nrm((L, NSA_Q, D), NSA_Q ** -0.5),
        "w_proj_b": nrm((L, RET_W, D), RET_W ** -0.5),
        "w_o": nrm((L, D, D), BETA * D ** -0.5),
        "ln2_g": gain((L, D)),
        "ln2_b": nrm((L, D), 0.02),
        "ffn2_w_gate": nrm((L, D, F), D ** -0.5),
        "ffn2_w_up": nrm((L, D, F), D ** -0.5),
        "ffn2_w_down": nrm((L, F, D), BETA * F ** -0.5),
        "ln3_g": gain((L, D)),
        "ln3_b": nrm((L, D), 0.02),
    }


def reference(x, ffn1_w_gate, ffn1_w_up, ffn1_w_down, ln1_g, ln1_b, w_in,
              cmp_k_pe, cmp_k_w1, cmp_k_b1, cmp_k_w2, cmp_v_pe, cmp_v_w1, cmp_v_b1, cmp_v_w2,
              ret_gn_g, ret_gn_b, w_merge_gate, w_proj_a, w_proj_b, w_o, ln2_g, ln2_b,
              ffn2_w_gate, ffn2_w_up, ffn2_w_down, ln3_g, ln3_b):
    for l in range(DEPTH):
        x = layer_norm(ALPHA * x + 0.5 * swiglu(x, ffn1_w_gate[l], ffn1_w_up[l], ffn1_w_down[l]), ln1_g[l], ln1_b[l])
        mix = hybrid_mixer(x, w_in[l], cmp_k_pe[l], cmp_k_w1[l], cmp_k_b1[l], cmp_k_w2[l],
                           cmp_v_pe[l], cmp_v_w1[l], cmp_v_b1[l], cmp_v_w2[l],
                           ret_gn_g[l], ret_gn_b[l], w_merge_gate[l], w_proj_a[l], w_proj_b[l], w_o[l])
        x = layer_norm(ALPHA * x + mix, ln2_g[l], ln2_b[l])
        x = layer_norm(ALPHA * x + 0.5 * swiglu(x, ffn2_w_gate[l], ffn2_w_up[l], ffn2_w_down[l]), ln3_g[l], ln3_b[l])
    return x
```

```python
import functools
import math

import jax
import jax.numpy as jnp
import numpy as np
from jax import lax
from jax.experimental import pallas as pl
from jax.experimental.pallas import tpu as pltpu

NSA_HEADS = 16
NSA_GROUPS = 4
NSA_REP = NSA_HEADS // NSA_GROUPS
HEAD_DIM = 64
CMP_BLOCK = 32
CMP_STRIDE = 16
SLC_BLOCK = 64
SLC_TOPN = 16
WINDOW = 512
RET_HEADS = 8
RET_DIM = 128
RET_CHUNK = 128
ROPE_BASE = 10000.0
LN_EPS = 1e-5
NEG = -1e30
FORCE = 1e30
TINY = 1e-30
NSA_Q = NSA_HEADS * HEAD_DIM
NSA_KV = NSA_GROUPS * HEAD_DIM
NSA_GATE = 3 * NSA_HEADS
RET_W = RET_HEADS * RET_DIM

LANE = 128
VMEM_LIMIT = 52 * 1024 * 1024
ROW_TILE = 512
FF_TILE = 512
COL_TILE = 512
Q_TILE = 256
K_TILE = 512
RET_TILE = 1024

BF16 = jnp.bfloat16
F32 = jnp.float32


def _dot(a, b):
    return jnp.dot(a, b, preferred_element_type=F32)


def _dot_nt(a, b):
    return lax.dot_general(a, b, (((1,), (1,)), ((), ())), preferred_element_type=F32)


def _dot_tn(a, b):
    return lax.dot_general(a, b, (((0,), (0,)), ((), ())), preferred_element_type=F32)


def _sigmoid(x):
    return 1.0 / (1.0 + jnp.exp(-x))


def _layer_norm_rows(y, g, b):
    mu = jnp.mean(y, axis=-1, keepdims=True)
    d = y - mu
    var = jnp.mean(d * d, axis=-1, keepdims=True)
    return d * lax.rsqrt(var + LN_EPS) * g + b


def _params(sem):
    return pltpu.CompilerParams(dimension_semantics=sem, vmem_limit_bytes=VMEM_LIMIT)


def _ffn_ln_kernel(x_ref, wg_ref, wu_ref, wd_ref, g_ref, b_ref, *rest, alpha, nf, with_bf16):
    if with_bf16:
        o_ref, ob_ref, xb_sc, acc_sc = rest
    else:
        o_ref, xb_sc, acc_sc = rest
        ob_ref = None
    f = pl.program_id(1)

    @pl.when(f == 0)
    def _():
        xb_sc[...] = x_ref[...].astype(BF16)
        acc_sc[...] = jnp.zeros_like(acc_sc)

    xb = xb_sc[...]
    a = _dot(xb, wg_ref[...])
    u = _dot(xb, wu_ref[...])
    h = (a * _sigmoid(a)) * u
    acc_sc[...] += _dot(h.astype(BF16), wd_ref[...])

    @pl.when(f == nf - 1)
    def _():
        y = alpha * x_ref[...] + 0.5 * acc_sc[...]
        out = _layer_norm_rows(y, g_ref[...], b_ref[...])
        o_ref[...] = out
        if with_bf16:
            ob_ref[...] = out.astype(BF16)


def _ffn_ln(x, wg, wu, wd, g, b, alpha, with_bf16):
    n, d = x.shape
    fp = wg.shape[1]
    bm = min(ROW_TILE, n)
    bf = min(FF_TILE, fp)
    nf = fp // bf
    out_shape = [jax.ShapeDtypeStruct((n, d), F32)]
    out_specs = [pl.BlockSpec((bm, d), lambda i, f: (i, 0))]
    if with_bf16:
        out_shape.append(jax.ShapeDtypeStruct((n, d), BF16))
        out_specs.append(pl.BlockSpec((bm, d), lambda i, f: (i, 0)))
    res = pl.pallas_call(
        functools.partial(_ffn_ln_kernel, alpha=alpha, nf=nf, with_bf16=with_bf16),
        grid=(n // bm, nf),
        in_specs=[
            pl.BlockSpec((bm, d), lambda i, f: (i, 0)),
            pl.BlockSpec((d, bf), lambda i, f: (0, f)),
            pl.BlockSpec((d, bf), lambda i, f: (0, f)),
            pl.BlockSpec((bf, d), lambda i, f: (f, 0)),
            pl.BlockSpec((1, d), lambda i, f: (0, 0)),
            pl.BlockSpec((1, d), lambda i, f: (0, 0)),
        ],
        out_specs=out_specs,
        out_shape=out_shape,
        scratch_shapes=[pltpu.VMEM((bm, d), BF16), pltpu.VMEM((bm, d), F32)],
        compiler_params=_params(("parallel", "arbitrary")),
        name="ffn_ln",
    )(x, wg, wu, wd, g, b)
    return res if with_bf16 else (res[0], None)


def _mm_kernel(x_ref, w_ref, o_ref, *, act):
    y = _dot(x_ref[...], w_ref[...])
    if act == "sigmoid":
        y = _sigmoid(y)
    o_ref[...] = y.astype(o_ref.dtype)


def _matmul(x, w, act, out_dtype, name):
    n, k = x.shape
    nout = w.shape[1]
    bm = min(ROW_TILE, n)
    bn = min(COL_TILE, nout)
    assert nout % bn == 0 and n % bm == 0
    return pl.pallas_call(
        functools.partial(_mm_kernel, act=act),
        grid=(n // bm, nout // bn),
        in_specs=[
            pl.BlockSpec((bm, k), lambda i, j: (i, 0)),
            pl.BlockSpec((k, bn), lambda i, j: (0, j)),
        ],
        out_specs=pl.BlockSpec((bm, bn), lambda i, j: (i, j)),
        out_shape=jax.ShapeDtypeStruct((n, nout), out_dtype),
        compiler_params=_params(("parallel", "arbitrary")),
        name=name,
    )(x, w)


def _compress_kernel(x_ref, pe_ref, w1_ref, b1_ref, w2_ref, o_ref, *, nh, half):
    x = x_ref[0, 0].astype(F32)
    xa = (x + pe_ref[0:1, :]).astype(BF16)
    xb = (x + pe_ref[1:2, :]).astype(BF16)
    ha = _dot(xa, w1_ref[0:half, :])
    hb = _dot(xb, w1_ref[half:2 * half, :])
    hid = ha + pltpu.roll(hb, nh - 1, 0) + b1_ref[...]
    c = math.sqrt(2.0 / math.pi)
    act = 0.5 * hid * (1.0 + jnp.tanh(c * (hid + 0.044715 * (hid * hid * hid))))
    o_ref[0, 0] = _dot(act.astype(BF16), w2_ref[...]).astype(o_ref.dtype)


def _compress(x, pe2, w1, b1, w2):
    bsz, g, nh, half = x.shape
    hid = w1.shape[1]
    hd = w2.shape[1]
    return pl.pallas_call(
        functools.partial(_compress_kernel, nh=nh, half=half),
        grid=(bsz, g),
        in_specs=[
            pl.BlockSpec((1, 1, nh, half), lambda b, gg: (b, gg, 0, 0)),
            pl.BlockSpec((2, half), lambda b, gg: (0, 0)),
            pl.BlockSpec((2 * half, hid), lambda b, gg: (0, 0)),
            pl.BlockSpec((1, hid), lambda b, gg: (0, 0)),
            pl.BlockSpec((hid, hd), lambda b, gg: (0, 0)),
        ],
        out_specs=pl.BlockSpec((1, 1, nh, hd), lambda b, gg: (b, gg, 0, 0)),
        out_shape=jax.ShapeDtypeStruct((bsz, g, nh, hd), BF16),
        compiler_params=_params(("parallel", "parallel")),
        name="compress",
    )(x, pe2, w1, b1, w2)


def _cmp_topk_kernel(q_ref, kct_ref, vc_ref, ov_ref, oc_ref, sel_ref, *, tq, nh, nb, n_sel):
    t0 = pl.program_id(2) * tq
    q = q_ref[0]
    kct = kct_ref[0, 0]
    vc = vc_ref[0, 0]
    tpos = t0 + lax.broadcasted_iota(jnp.int32, (tq, nh), 0)
    cend = lax.broadcasted_iota(jnp.int32, (tq, nh), 1) * CMP_STRIDE + (CMP_BLOCK - 1)
    mask = cend <= tpos
    psum = jnp.zeros((tq, nh), F32)
    outs = []
    for r in range(NSA_REP):
        qr = q[:, r * HEAD_DIM:(r + 1) * HEAD_DIM]
        s = jnp.where(mask, _dot(qr, kct), NEG)
        m = jnp.max(s, axis=-1, keepdims=True)
        e = jnp.where(mask, jnp.exp(s - m), 0.0)
        inv = 1.0 / jnp.maximum(jnp.sum(e, axis=-1, keepdims=True), TINY)
        p = e * inv
        psum = psum + p
        outs.append(_dot(p.astype(BF16), vc))
    oc_ref[0] = jnp.concatenate(outs, axis=-1)

    imp = _dot(psum.astype(BF16), ov_ref[...])
    blk = lax.broadcasted_iota(jnp.int32, (tq, nb), 1)
    cur = (t0 + lax.broadcasted_iota(jnp.int32, (tq, nb), 0)) // SLC_BLOCK
    forced = (blk == 0) | (blk == cur) | (blk == cur - 1)
    x = jnp.where(forced, FORCE, jnp.where(blk <= cur, imp, NEG))
    sel = jnp.zeros((tq, nb), F32)
    for _ in range(n_sel):
        mx = jnp.max(x, axis=-1, keepdims=True)
        idx = jnp.min(jnp.where(x == mx, blk, nb), axis=-1, keepdims=True)
        hit = blk == idx
        sel = jnp.where(hit, 1.0, sel)
        x = jnp.where(hit, -jnp.inf, x)
    sel_ref[0, 0] = sel.astype(sel_ref.dtype)


def _cmp_topk(q, kct, vc, ov):
    bsz, s, _ = q.shape
    g = kct.shape[1]
    nh = kct.shape[3]
    nb = ov.shape[1]
    tq = min(Q_TILE, s)
    rw = NSA_REP * HEAD_DIM
    n_sel = min(SLC_TOPN, nb)
    return pl.pallas_call(
        functools.partial(_cmp_topk_kernel, tq=tq, nh=nh, nb=nb, n_sel=n_sel),
        grid=(bsz, g, s // tq),
        in_specs=[
            pl.BlockSpec((1, tq, rw), lambda b, gg, i: (b, i, gg)),
            pl.BlockSpec((1, 1, HEAD_DIM, nh), lambda b, gg, i: (b, gg, 0, 0)),
            pl.BlockSpec((1, 1, nh, HEAD_DIM), lambda b, gg, i: (b, gg, 0, 0)),
            pl.BlockSpec((nh, nb), lambda b, gg, i: (0, 0)),
        ],
        out_specs=[
            pl.BlockSpec((1, tq, rw), lambda b, gg, i: (b, i, gg)),
            pl.BlockSpec((1, 1, tq, nb), lambda b, gg, i: (b, gg, i, 0)),
        ],
        out_shape=[
            jax.ShapeDtypeStruct((bsz, s, g * rw), F32),
            jax.ShapeDtypeStruct((bsz, g, s, nb), BF16),
        ],
        compiler_params=_params(("parallel", "parallel", "arbitrary")),
        name="cmp_topk",
    )(q, kct, vc, ov)


def _sel_win_kernel(q_ref, kst_ref, vs_ref, kwt_ref, vw_ref, sel_ref, exp_ref, oc_ref, gate_ref,
                    o_ref, m_sc, l_sc, acc_sc, *, tq, tk, chunk, nvar, wlen):
    i = pl.program_id(2)
    t0 = i * tq
    q = q_ref[0]
    qs = [q[:, r * HEAD_DIM:(r + 1) * HEAD_DIM] for r in range(NSA_REP)]
    trow = t0 + lax.broadcasted_iota(jnp.int32, (tq, tk), 0)

    m_sc[...] = jnp.full_like(m_sc, NEG)
    l_sc[...] = jnp.zeros_like(l_sc)
    acc_sc[...] = jnp.zeros_like(acc_sc)

    def tile(kt, causal):
        k0 = pl.multiple_of(kt * tk, tk)
        kt_t = kst_ref[0, 0, :, pl.ds(k0, tk)]
        v_t = vs_ref[0, 0, pl.ds(k0, tk), :]
        c0 = pl.multiple_of((kt // nvar) * chunk, chunk)
        selc = sel_ref[0, 0, :, pl.ds(c0, chunk)]
        mask = _dot(selc, exp_ref[kt % nvar]) > 0.5
        if causal:
            kpos = k0 + lax.broadcasted_iota(jnp.int32, (tq, tk), 1)
            mask = jnp.logical_and(mask, kpos <= trow)
        for r in range(NSA_REP):
            s = jnp.where(mask, _dot(qs[r], kt_t), NEG)
            m_old = m_sc[r]
            m_new = jnp.maximum(m_old, jnp.max(s, axis=-1, keepdims=True))
            alpha = jnp.exp(m_old - m_new)
            p = jnp.where(mask, jnp.exp(s - m_new), 0.0)
            l_sc[r] = alpha * l_sc[r] + jnp.sum(p, axis=-1, keepdims=True)
            acc_sc[r] = alpha * acc_sc[r] + _dot(p.astype(BF16), v_t)
            m_sc[r] = m_new

    n_kt = (t0 + tq - 1) // tk + 1

    def body(kt, carry):
        tile(kt, False)
        return carry

    lax.fori_loop(0, n_kt - 1, body, 0)
    tile(n_kt - 1, True)

    wk = wlen + tq
    w0 = pl.multiple_of(jnp.maximum(t0 - wlen, 0), tq)
    kw_t = kwt_ref[0, 0, :, pl.ds(w0, wk)]
    vw_t = vw_ref[0, 0, pl.ds(w0, wk), :]
    diff = (t0 + lax.broadcasted_iota(jnp.int32, (tq, wk), 0)) - (
        w0 + lax.broadcasted_iota(jnp.int32, (tq, wk), 1))
    wmask = jnp.logical_and(diff >= 0, diff < wlen)

    gate = gate_ref[0, 0]
    oc = oc_ref[0]
    outs = []
    for r in range(NSA_REP):
        o_s = acc_sc[r] * (1.0 / jnp.maximum(l_sc[r], TINY))
        s = jnp.where(wmask, _dot(qs[r], kw_t), NEG)
        m = jnp.max(s, axis=-1, keepdims=True)
        e = jnp.where(wmask, jnp.exp(s - m), 0.0)
        inv = 1.0 / jnp.maximum(jnp.sum(e, axis=-1, keepdims=True), TINY)
        o_w = _dot(e.astype(BF16), vw_t) * inv
        o_c = oc[:, r * HEAD_DIM:(r + 1) * HEAD_DIM]
        outs.append(gate[:, 3 * r:3 * r + 1] * o_c + gate[:, 3 * r + 1:3 * r + 2] * o_s
                    + gate[:, 3 * r + 2:3 * r + 3] * o_w)
    o_ref[0] = jnp.concatenate(outs, axis=-1).astype(o_ref.dtype)


def _sel_win(q, kst, vs, kwt, vw, sel, expand, oc, gates):
    bsz, s, _ = q.shape
    g = kst.shape[1]
    nb = sel.shape[3]
    nvar, chunk, tk = expand.shape
    tq = min(Q_TILE, s)
    rw = NSA_REP * HEAD_DIM
    return pl.pallas_call(
        functools.partial(_sel_win_kernel, tq=tq, tk=tk, chunk=chunk, nvar=nvar, wlen=WINDOW),
        grid=(bsz, g, s // tq),
        in_specs=[
            pl.BlockSpec((1, tq, rw), lambda b, gg, i: (b, i, gg)),
            pl.BlockSpec((1, 1, HEAD_DIM, s), lambda b, gg, i: (b, gg, 0, 0)),
            pl.BlockSpec((1, 1, s, HEAD_DIM), lambda b, gg, i: (b, gg, 0, 0)),
            pl.BlockSpec((1, 1, HEAD_DIM, s), lambda b, gg, i: (b, gg, 0, 0)),
            pl.BlockSpec((1, 1, s, HEAD_DIM), lambda b, gg, i: (b, gg, 0, 0)),
            pl.BlockSpec((1, 1, tq, nb), lambda b, gg, i: (b, gg, i, 0)),
            pl.BlockSpec((nvar, chunk, tk), lambda b, gg, i: (0, 0, 0)),
            pl.BlockSpec((1, tq, rw), lambda b, gg, i: (b, i, gg)),
            pl.BlockSpec((1, 1, tq, LANE), lambda b, gg, i: (b, gg, i, 0)),
        ],
        out_specs=pl.BlockSpec((1, tq, rw), lambda b, gg, i: (b, i, gg)),
        out_shape=jax.ShapeDtypeStruct((bsz, s, g * rw), BF16),
        scratch_shapes=[
            pltpu.VMEM((NSA_REP, tq, 1), F32),
            pltpu.VMEM((NSA_REP, tq, 1), F32),
            pltpu.VMEM((NSA_REP, tq, HEAD_DIM), F32),
        ],
        compiler_params=_params(("parallel", "parallel", "arbitrary")),
        name="sel_win",
    )(q, kst, vs, kwt, vw, sel, expand, oc, gates)


def _retention_kernel(q_ref, k_ref, v_ref, gb_ref, cos_ref, sin_ref, dmat_ref, qdec_ref, kdec_ref,
                      cdec_ref, gng_ref, gnb_ref, o_ref, state_sc, *, nchunk, c):
    @pl.when(pl.program_id(2) == 0)
    def _():
        state_sc[...] = jnp.zeros_like(state_sc)

    dmat = dmat_ref[0]
    qdec = qdec_ref[0]
    kdec = kdec_ref[0]
    cdec = cdec_ref[0]
    kscale = RET_DIM ** -0.5
    half = RET_DIM // 2
    for n in range(nchunk):
        rows = pl.ds(n * c, c)
        cos = cos_ref[rows, :]
        sin = sin_ref[rows, :]
        qf = q_ref[0, rows, :].astype(F32)
        kf = k_ref[0, rows, :].astype(F32)
        qr = qf * cos + pltpu.roll(qf, half, 1) * sin
        kr = (kf * cos + pltpu.roll(kf, half, 1) * sin) * kscale
        v = v_ref[0, rows, :]
        inner = _dot_nt(qr.astype(BF16), kr.astype(BF16)) * dmat
        state = state_sc[...]
        y = _dot(inner.astype(BF16), v) + _dot((qr * qdec).astype(BF16), state.astype(BF16))
        state_sc[...] = state * cdec + _dot_tn((kr * kdec).astype(BF16), v)
        mu = jnp.mean(y, axis=-1, keepdims=True)
        d = y - mu
        var = jnp.mean(d * d, axis=-1, keepdims=True)
        yn = d * lax.rsqrt(var + LN_EPS) * gng_ref[...] + gnb_ref[...]
        gb = gb_ref[0, rows, :].astype(F32)
        o_ref[0, rows, :] = ((gb * _sigmoid(gb)) * yn).astype(o_ref.dtype)


def _retention(ret, cos2, sin2, dmat, qdec, kdec, cdec, gng, gnb):
    bsz, s, _ = ret.shape
    h = RET_HEADS
    c = RET_CHUNK
    tc = min(RET_TILE, s)
    spec = lambda off: pl.BlockSpec((1, tc, RET_DIM), lambda b, hh, j, off=off: (b, j, off + hh))
    hspec = lambda shp: pl.BlockSpec((1,) + shp, lambda b, hh, j: (hh, 0, 0))
    return pl.pallas_call(
        functools.partial(_retention_kernel, nchunk=tc // c, c=c),
        grid=(bsz, h, s // tc),
        in_specs=[
            spec(0), spec(h), spec(2 * h), spec(3 * h),
            pl.BlockSpec((tc, RET_DIM), lambda b, hh, j: (j, 0)),
            pl.BlockSpec((tc, RET_DIM), lambda b, hh, j: (j, 0)),
            hspec((c, c)), hspec((c, RET_DIM)), hspec((c, RET_DIM)), hspec((1, RET_DIM)),
            pl.BlockSpec((1, RET_DIM), lambda b, hh, j: (0, hh)),
            pl.BlockSpec((1, RET_DIM), lambda b, hh, j: (0, hh)),
        ],
        out_specs=pl.BlockSpec((1, tc, RET_DIM), lambda b, hh, j: (b, j, hh)),
        out_shape=jax.ShapeDtypeStruct((bsz, s, h * RET_DIM), BF16),
        scratch_shapes=[pltpu.VMEM((RET_DIM, RET_DIM), F32)],
        compiler_params=_params(("parallel", "parallel", "arbitrary")),
        name="retention",
    )(ret, ret, ret, ret, cos2, sin2, dmat, qdec, kdec, cdec, gng, gnb)


def _merge_kernel(h_ref, oa_ref, ob_ref, wga_ref, wgb_ref, wpa_ref, wpb_ref, o_ref):
    hb = h_ref[...]
    ga = _sigmoid(_dot(hb, wga_ref[...]))
    gb = _sigmoid(_dot(hb, wgb_ref[...]))
    merged = ga * _dot(oa_ref[...], wpa_ref[...]) + gb * _dot(ob_ref[...], wpb_ref[...])
    o_ref[...] = merged.astype(o_ref.dtype)


def _merge(hb, oa, ob, wga, wgb, wpa, wpb):
    n, d = hb.shape
    bm = min(ROW_TILE, n)
    bn = min(COL_TILE, d)
    ka, kb = oa.shape[1], ob.shape[1]
    return pl.pallas_call(
        _merge_kernel,
        grid=(n // bm, d // bn),
        in_specs=[
            pl.BlockSpec((bm, d), lambda i, j: (i, 0)),
            pl.BlockSpec((bm, ka), lambda i, j: (i, 0)),
            pl.BlockSpec((bm, kb), lambda i, j: (i, 0)),
            pl.BlockSpec((d, bn), lambda i, j: (0, j)),
            pl.BlockSpec((d, bn), lambda i, j: (0, j)),
            pl.BlockSpec((ka, bn), lambda i, j: (0, j)),
            pl.BlockSpec((kb, bn), lambda i, j: (0, j)),
        ],
        out_specs=pl.BlockSpec((bm, bn), lambda i, j: (i, j)),
        out_shape=jax.ShapeDtypeStruct((n, d), BF16),
        compiler_params=_params(("parallel", "arbitrary")),
        name="merge",
    )(hb, oa, ob, wga, wgb, wpa, wpb)


def _proj_ln_kernel(x_ref, m_ref, w_ref, g_ref, b_ref, o_ref, *, alpha):
    y = alpha * x_ref[...] + _dot(m_ref[...], w_ref[...])
    o_ref[...] = _layer_norm_rows(y, g_ref[...], b_ref[...])


def _proj_ln(x, m, w, g, b, alpha):
    n, d = x.shape
    bm = min(ROW_TILE, n)
    return pl.pallas_call(
        functools.partial(_proj_ln_kernel, alpha=alpha),
        grid=(n // bm,),
        in_specs=[
            pl.BlockSpec((bm, d), lambda i: (i, 0)),
            pl.BlockSpec((bm, d), lambda i: (i, 0)),
            pl.BlockSpec((d, d), lambda i: (0, 0)),
            pl.BlockSpec((1, d), lambda i: (0, 0)),
            pl.BlockSpec((1, d), lambda i: (0, 0)),
        ],
        out_specs=pl.BlockSpec((bm, d), lambda i: (i, 0)),
        out_shape=jax.ShapeDtypeStruct((n, d), F32),
        compiler_params=_params(("parallel",)),
        name="proj_ln",
    )(x, m, w, g, b)


def _overlap_matrix(nh, nb):
    c0 = np.arange(nh) * CMP_STRIDE
    c1 = c0 + CMP_BLOCK
    s0 = np.arange(nb) * SLC_BLOCK
    s1 = s0 + SLC_BLOCK
    ov = (c0[:, None] < s1[None, :]) & (c1[:, None] > s0[None, :])
    ov[nh - 1, :] = False
    return jnp.asarray(ov, BF16)


def _expand_matrix(nb, tk):
    chunk = min(LANE, nb)
    per_tile = tk // SLC_BLOCK
    nvar = chunk // per_tile
    e = np.zeros((nvar, chunk, tk), np.float32)
    for v in range(nvar):
        for c in range(tk):
            e[v, v * per_tile + c // SLC_BLOCK, c] = 1.0
    return jnp.asarray(e, BF16)


def _retention_tables(s):
    h, c, d = RET_HEADS, RET_CHUNK, RET_DIM
    inv = ROPE_BASE ** (-jnp.arange(0, d, 2, dtype=F32) / d)
    ang = jnp.arange(s)[:, None].astype(F32) * inv[None, :]
    cos, sin = jnp.cos(ang), jnp.sin(ang)
    cos2 = jnp.concatenate([cos, cos], -1)
    sin2 = jnp.concatenate([-sin, sin], -1)
    log_g = jnp.log1p(-jnp.exp2(-5.0 - jnp.arange(h, dtype=F32)))
    i = jnp.arange(c, dtype=F32)
    diff = i[:, None] - i[None, :]
    dmat = jnp.where(diff >= 0, jnp.exp(jnp.maximum(diff, 0.0)[None] * log_g[:, None, None]), 0.0)
    kdec = jnp.exp((c - 1 - i)[None, :] * log_g[:, None])
    qdec = jnp.exp((i + 1)[None, :] * log_g[:, None])
    cdec = jnp.exp(c * log_g)
    bc = lambda t: jnp.broadcast_to(t[:, :, None], (h, c, d))
    return cos2, sin2, dmat, bc(qdec), bc(kdec), jnp.broadcast_to(cdec[:, None, None], (h, 1, d))


def _pad_cols(w, mult):
    pad = (-w.shape[1]) % mult
    return jnp.pad(w, ((0, 0), (0, pad))) if pad else w


def _mixer(hf, hb, bsz, s, w_in, cmp_k, cmp_v, ret_gn_g, ret_gn_b, w_merge_gate, w_proj_a, w_proj_b,
           w_o, ln_g, ln_b, alpha, tables):
    n, d = hf.shape
    g, hd, rw = NSA_GROUPS, HEAD_DIM, NSA_REP * HEAD_DIM
    ov, expand, ret_tabs = tables
    o_nsa = NSA_Q + 6 * NSA_KV
    w_nsa = w_in[:, :o_nsa].astype(BF16)
    w_gate = _pad_cols(w_in[:, o_nsa:o_nsa + NSA_GATE], LANE).astype(BF16)
    w_ret = w_in[:, o_nsa + NSA_GATE:].astype(BF16)

    nsa = _matmul(hb, w_nsa, None, BF16, "proj_nsa")
    ga = _matmul(hb, w_gate, "sigmoid", F32, "proj_gate")
    ret = _matmul(hb, w_ret, None, BF16, "proj_ret")

    q = (nsa[:, :NSA_Q] * (hd ** -0.5)).astype(BF16).reshape(bsz, s, NSA_Q)
    kv = nsa[:, NSA_Q:].reshape(bsz, s, 6, g, hd)
    to_rows = lambda t: jnp.transpose(t, (0, 2, 1, 3))
    to_cols = lambda t: jnp.transpose(t, (0, 2, 3, 1))
    nh = s // CMP_STRIDE
    half = CMP_STRIDE * hd

    def compress(t, prm):
        pe, w1, b1, w2 = prm
        x = to_rows(t).reshape(bsz, g, nh, half)
        return _compress(x, pe.reshape(2, half), w1.astype(BF16), b1.reshape(1, -1), w2.astype(BF16))

    kc = compress(kv[:, :, 0], cmp_k)
    vc = compress(kv[:, :, 1], cmp_v)
    kct = jnp.transpose(kc, (0, 1, 3, 2))
    oc, sel = _cmp_topk(q, kct, vc, ov)

    gates = ga[:, :NSA_GATE].reshape(bsz, s, g, 3 * NSA_REP)
    gates = jnp.pad(jnp.transpose(gates, (0, 2, 1, 3)), ((0, 0), (0, 0), (0, 0), (0, LANE - 3 * NSA_REP)))
    o_a = _sel_win(q, to_cols(kv[:, :, 2]), to_rows(kv[:, :, 3]), to_cols(kv[:, :, 4]),
                   to_rows(kv[:, :, 5]), sel, expand, oc, gates)

    o_b = _retention(ret.reshape(bsz, s, 4 * RET_W), *ret_tabs,
                     ret_gn_g.reshape(1, -1), ret_gn_b.reshape(1, -1))

    merged = _merge(hb, o_a.reshape(n, NSA_Q), o_b.reshape(n, RET_W),
                    w_merge_gate[:, :d].astype(BF16), w_merge_gate[:, d:].astype(BF16),
                    w_proj_a.astype(BF16), w_proj_b.astype(BF16))
    return _proj_ln(hf, merged, w_o.astype(BF16), ln_g.reshape(1, -1), ln_b.reshape(1, -1), alpha)


def kernel(x, ffn1_w_gate, ffn1_w_up, ffn1_w_down, ln1_g, ln1_b, w_in, cmp_k_pe, cmp_k_w1, cmp_k_b1,
           cmp_k_w2, cmp_v_pe, cmp_v_w1, cmp_v_b1, cmp_v_w2, ret_gn_g, ret_gn_b, w_merge_gate, w_proj_a,
           w_proj_b, w_o, ln2_g, ln2_b, ffn2_w_gate, ffn2_w_up, ffn2_w_down, ln3_g, ln3_b):
    bsz, s, d = x.shape
    depth = ffn1_w_gate.shape[0]
    f = ffn1_w_gate.shape[2]
    alpha = (2 * depth) ** 0.25
    fpad = (-f) % min(FF_TILE, f)
    nb = s // SLC_BLOCK
    tk = min(K_TILE, s)
    tables = (_overlap_matrix(s // CMP_STRIDE, nb), _expand_matrix(nb, tk), _retention_tables(s))

    def ffn_weights(wg, wu, wd):
        return (jnp.pad(wg, ((0, 0), (0, fpad))).astype(BF16), jnp.pad(wu, ((0, 0), (0, fpad))).astype(BF16),
                jnp.pad(wd, ((0, fpad), (0, 0))).astype(BF16))

    row = lambda t: t.reshape(1, -1)
    xf = x.reshape(bsz * s, d)
    for l in range(depth):
        hf, hb = _ffn_ln(xf, *ffn_weights(ffn1_w_gate[l], ffn1_w_up[l], ffn1_w_down[l]),
                         row(ln1_g[l]), row(ln1_b[l]), alpha, True)
        xf = _mixer(hf, hb, bsz, s, w_in[l],
                    (cmp_k_pe[l], cmp_k_w1[l], cmp_k_b1[l], cmp_k_w2[l]),
                    (cmp_v_pe[l], cmp_v_w1[l], cmp_v_b1[l], cmp_v_w2[l]),
                    ret_gn_g[l], ret_gn_b[l], w_merge_gate[l], w_proj_a[l], w_proj_b[l], w_o[l],
                    ln2_g[l], ln2_b[l], alpha, tables)
        xf, _ = _ffn_ln(xf, *ffn_weights(ffn2_w_gate[l], ffn2_w_up[l], ffn2_w_down[l]),
                        row(ln3_g[l]), row(ln3_b[l]), alpha, False)
    return xf.reshape(bsz, s, d)
```

```python
import functools
import math

import jax
import jax.numpy as jnp
import numpy as np
from jax import lax
from jax.experimental import pallas as pl
from jax.experimental.pallas import tpu as pltpu

NSA_HEADS = 16
NSA_GROUPS = 4
NSA_REP = NSA_HEADS // NSA_GROUPS
HEAD_DIM = 64
CMP_BLOCK = 32
CMP_STRIDE = 16
SLC_BLOCK = 64
SLC_TOPN = 16
WINDOW = 512
RET_HEADS = 8
RET_DIM = 128
RET_CHUNK = 128
ROPE_BASE = 10000.0
LN_EPS = 1e-5
NEG = -1e30
FORCE = 1e30
TINY = 1e-30
NSA_Q = NSA_HEADS * HEAD_DIM
NSA_KV = NSA_GROUPS * HEAD_DIM
NSA_GATE = 3 * NSA_HEADS
RET_W = RET_HEADS * RET_DIM

LANE = 128
VMEM_LIMIT = 52 * 1024 * 1024
ROW_TILE = 512
FF_TILE = 512
COL_TILE = 512
Q_TILE = 256
K_TILE = 512
RET_TILE = 1024

BF16 = jnp.bfloat16
F32 = jnp.float32


def _dot(a, b):
    return jnp.dot(a, b, preferred_element_type=F32)


def _dot_nt(a, b):
    return lax.dot_general(a, b, (((1,), (1,)), ((), ())), preferred_element_type=F32)


def _dot_tn(a, b):
    return lax.dot_general(a, b, (((0,), (0,)), ((), ())), preferred_element_type=F32)


def _sigmoid(x):
    return 1.0 / (1.0 + jnp.exp(-x))


def _layer_norm_rows(y, g, b):
    mu = jnp.mean(y, axis=-1, keepdims=True)
    d = y - mu
    var = jnp.mean(d * d, axis=-1, keepdims=True)
    return d * lax.rsqrt(var + LN_EPS) * g + b


def _params(sem):
    return pltpu.CompilerParams(dimension_semantics=sem, vmem_limit_bytes=VMEM_LIMIT)


def _ffn_ln_kernel(x_ref, wg_ref, wu_ref, wd_ref, g_ref, b_ref, *rest, alpha, nf, with_bf16):
    if with_bf16:
        o_ref, ob_ref, xb_sc, acc_sc = rest
    else:
        o_ref, xb_sc, acc_sc = rest
        ob_ref = None
    f = pl.program_id(1)

    @pl.when(f == 0)
    def _():
        xb_sc[...] = x_ref[...].astype(BF16)
        acc_sc[...] = jnp.zeros_like(acc_sc)

    xb = xb_sc[...]
    a = _dot(xb, wg_ref[...])
    u = _dot(xb, wu_ref[...])
    h = (a * _sigmoid(a)) * u
    acc_sc[...] += _dot(h.astype(BF16), wd_ref[...])

    @pl.when(f == nf - 1)
    def _():
        y = alpha * x_ref[...] + 0.5 * acc_sc[...]
        out = _layer_norm_rows(y, g_ref[...], b_ref[...])
        o_ref[...] = out
        if with_bf16:
            ob_ref[...] = out.astype(BF16)


def _ffn_ln(x, wg, wu, wd, g, b, alpha, with_bf16):
    n, d = x.shape
    fp = wg.shape[1]
    bm = min(ROW_TILE, n)
    bf = min(FF_TILE, fp)
    nf = fp // bf
    out_shape = [jax.ShapeDtypeStruct((n, d), F32)]
    out_specs = [pl.BlockSpec((bm, d), lambda i, f: (i, 0))]
    if with_bf16:
        out_shape.append(jax.ShapeDtypeStruct((n, d), BF16))
        out_specs.append(pl.BlockSpec((bm, d), lambda i, f: (i, 0)))
    res = pl.pallas_call(
        functools.partial(_ffn_ln_kernel, alpha=alpha, nf=nf, with_bf16=with_bf16),
        grid=(n // bm, nf),
        in_specs=[
            pl.BlockSpec((bm, d), lambda i, f: (i, 0)),
            pl.BlockSpec((d, bf), lambda i, f: (0, f)),
            pl.BlockSpec((d, bf), lambda i, f: (0, f)),
            pl.BlockSpec((bf, d), lambda i, f: (f, 0)),
            pl.BlockSpec((1, d), lambda i, f: (0, 0)),
            pl.BlockSpec((1, d), lambda i, f: (0, 0)),
        ],
        out_specs=out_specs,
        out_shape=out_shape,
        scratch_shapes=[pltpu.VMEM((bm, d), BF16), pltpu.VMEM((bm, d), F32)],
        compiler_params=_params(("parallel", "arbitrary")),
        name="ffn_ln",
    )(x, wg, wu, wd, g, b)
    return res if with_bf16 else (res[0], None)


def _mm_kernel(x_ref, w_ref, o_ref, *, act):
    y = _dot(x_ref[...], w_ref[...])
    if act == "sigmoid":
        y = _sigmoid(y)
    o_ref[...] = y.astype(o_ref.dtype)


def _matmul(x, w, act, out_dtype, name):
    n, k = x.shape
    nout = w.shape[1]
    bm = min(ROW_TILE, n)
    bn = min(COL_TILE, nout)
    assert nout % bn == 0 and n % bm == 0
    return pl.pallas_call(
        functools.partial(_mm_kernel, act=act),
        grid=(n // bm, nout // bn),
        in_specs=[
            pl.BlockSpec((bm, k), lambda i, j: (i, 0)),
            pl.BlockSpec((k, bn), lambda i, j: (0, j)),
        ],
        out_specs=pl.BlockSpec((bm, bn), lambda i, j: (i, j)),
        out_shape=jax.ShapeDtypeStruct((n, nout), out_dtype),
        compiler_params=_params(("parallel", "arbitrary")),
        name=name,
    )(x, w)


def _compress_kernel(x_ref, pe_ref, w1_ref, b1_ref, w2_ref, o_ref, *, nh, half):
    x = x_ref[0, 0].astype(F32)
    xa = (x + pe_ref[0:1, :]).astype(BF16)
    xb = (x + pe_ref[1:2, :]).astype(BF16)
    ha = _dot(xa, w1_ref[0:half, :])
    hb = _dot(xb, w1_ref[half:2 * half, :])
    hid = ha + pltpu.roll(hb, nh - 1, 0) + b1_ref[...]
    c = math.sqrt(2.0 / math.pi)
    act = 0.5 * hid * (1.0 + jnp.tanh(c * (hid + 0.044715 * (hid * hid * hid))))
    o_ref[0, 0] = _dot(act.astype(BF16), w2_ref[...]).astype(o_ref.dtype)


def _compress(x, pe2, w1, b1, w2):
    bsz, g, nh, half = x.shape
    hid = w1.shape[1]
    hd = w2.shape[1]
    return pl.pallas_call(
        functools.partial(_compress_kernel, nh=nh, half=half),
        grid=(bsz, g),
        in_specs=[
            pl.BlockSpec((1, 1, nh, half), lambda b, gg: (b, gg, 0, 0)),
            pl.BlockSpec((2, half), lambda b, gg: (0, 0)),
            pl.BlockSpec((2 * half, hid), lambda b, gg: (0, 0)),
            pl.BlockSpec((1, hid), lambda b, gg: (0, 0)),
            pl.BlockSpec((hid, hd), lambda b, gg: (0, 0)),
        ],
        out_specs=pl.BlockSpec((1, 1, nh, hd), lambda b, gg: (b, gg, 0, 0)),
        out_shape=jax.ShapeDtypeStruct((bsz, g, nh, hd), BF16),
        compiler_params=_params(("parallel", "parallel")),
        name="compress",
    )(x, pe2, w1, b1, w2)


def _cmp_topk_kernel(q_ref, kct_ref, vc_ref, ovt_ref, oc_ref, sel_ref, *, tq, nh, nb, n_sel):
    t0 = pl.program_id(2) * tq
    q = q_ref[0]
    kct = kct_ref[0, 0]
    vc = vc_ref[0, 0]
    rows = NSA_REP * tq
    tpos = t0 + lax.broadcasted_iota(jnp.int32, (tq, nh), 0)
    cend = lax.broadcasted_iota(jnp.int32, (tq, nh), 1) * CMP_STRIDE + (CMP_BLOCK - 1)
    bias = jnp.where(cend <= tpos, 0.0, NEG)
    q_all = jnp.concatenate([q[:, r * HEAD_DIM:(r + 1) * HEAD_DIM] for r in range(NSA_REP)], axis=0)
    s = (_dot(q_all, kct).reshape(NSA_REP, tq, nh) + bias[None]).reshape(rows, nh)
    m = jnp.max(s, axis=-1, keepdims=True)
    e = jnp.exp(s - m)
    seen = (t0 + lax.broadcasted_iota(jnp.int32, (tq, 1), 0)) >= (CMP_BLOCK - 1)
    seen = jnp.concatenate([seen] * NSA_REP, axis=0)
    inv = jnp.where(seen, 1.0 / jnp.maximum(jnp.sum(e, axis=-1, keepdims=True), TINY), 0.0)
    p = e * inv
    o_all = _dot(p.astype(BF16), vc)
    oc_ref[0] = jnp.concatenate([o_all[r * tq:(r + 1) * tq] for r in range(NSA_REP)], axis=-1)

    psum = jnp.sum(p.reshape(NSA_REP, tq, nh), axis=0)
    imp = _dot_nt(ovt_ref[...], psum.astype(BF16))
    blk = lax.broadcasted_iota(jnp.int32, (nb, tq), 0).astype(F32)
    cur = ((t0 + lax.broadcasted_iota(jnp.int32, (nb, tq), 1)) // SLC_BLOCK).astype(F32)
    forced = (blk == 0.0) | (blk == cur) | (blk == cur - 1.0)
    x = jnp.where(forced, -jnp.inf, jnp.where(blk <= cur, imp, NEG))
    selb = jnp.where(forced, 0.0, NEG)
    for _ in range(n_sel - 3):
        mx = jnp.max(x, axis=0, keepdims=True)
        idx = jnp.min(jnp.where(x == mx, blk, float(nb)), axis=0, keepdims=True)
        hit = blk == idx
        selb = jnp.where(hit, 0.0, selb)
        x = jnp.where(hit, -jnp.inf, x)
    sel_ref[0, 0] = selb.T.astype(sel_ref.dtype)


def _cmp_topk(q, kct, vc, ovt):
    bsz, s, _ = q.shape
    g = kct.shape[1]
    nh = kct.shape[3]
    nb = ovt.shape[0]
    assert nb >= SLC_TOPN
    tq = min(Q_TILE, s)
    rw = NSA_REP * HEAD_DIM
    n_sel = SLC_TOPN
    return pl.pallas_call(
        functools.partial(_cmp_topk_kernel, tq=tq, nh=nh, nb=nb, n_sel=n_sel),
        grid=(bsz, g, s // tq),
        in_specs=[
            pl.BlockSpec((1, tq, rw), lambda b, gg, i: (b, i, gg)),
            pl.BlockSpec((1, 1, HEAD_DIM, nh), lambda b, gg, i: (b, gg, 0, 0)),
            pl.BlockSpec((1, 1, nh, HEAD_DIM), lambda b, gg, i: (b, gg, 0, 0)),
            pl.BlockSpec((nb, nh), lambda b, gg, i: (0, 0)),
        ],
        out_specs=[
            pl.BlockSpec((1, tq, rw), lambda b, gg, i: (b, i, gg)),
            pl.BlockSpec((1, 1, tq, nb), lambda b, gg, i: (b, gg, i, 0)),
        ],
        out_shape=[
            jax.ShapeDtypeStruct((bsz, s, g * rw), F32),
            jax.ShapeDtypeStruct((bsz, g, s, nb), BF16),
        ],
        compiler_params=_params(("parallel", "parallel", "arbitrary")),
        name="cmp_topk",
    )(q, kct, vc, ovt)


def _sel_win_kernel(q_ref, ke_ref, vs_ref, kwt_ref, vw_ref, selb_ref, oc_ref, gate_ref,
                    o_ref, lhs_sc, m_sc, l_sc, acc_sc, *, tq, tk, chunk, nvar, nchunk, wlen):
    i = pl.program_id(2)
    t0 = i * tq
    rows = NSA_REP * tq
    q = q_ref[0]
    q_all = jnp.concatenate([q[:, r * HEAD_DIM:(r + 1) * HEAD_DIM] for r in range(NSA_REP)], axis=0)
    for c in range(nchunk):
        selb = selb_ref[0, 0, :, c * chunk:(c + 1) * chunk]
        lhs_sc[c] = jnp.concatenate([jnp.concatenate([selb] * NSA_REP, axis=0), q_all], axis=1)

    m_sc[...] = jnp.full_like(m_sc, NEG)
    l_sc[...] = jnp.zeros_like(l_sc)
    acc_sc[...] = jnp.zeros_like(acc_sc)

    def tile(kt, causal):
        k0 = pl.multiple_of(kt * tk, tk)
        ke_t = ke_ref[0, 0, :, pl.ds(k0, tk)]
        v_t = vs_ref[0, 0, pl.ds(k0, tk), :]
        s = _dot(lhs_sc[kt // nvar], ke_t)
        if causal:
            kpos = k0 + lax.broadcasted_iota(jnp.int32, (tq, tk), 1)
            trow = t0 + lax.broadcasted_iota(jnp.int32, (tq, tk), 0)
            cb = jnp.where(kpos <= trow, 0.0, NEG)
            s = (s.reshape(NSA_REP, tq, tk) + cb[None]).reshape(rows, tk)
        m_old = m_sc[...]
        m_new = jnp.maximum(m_old, jnp.max(s, axis=-1, keepdims=True))
        alpha = jnp.exp(m_old - m_new)
        p = jnp.exp(s - jnp.tile(m_new, (1, tk // LANE)))
        l_sc[...] = alpha * l_sc[...] + jnp.sum(p, axis=-1, keepdims=True)
        acc_sc[...] = alpha[:, :HEAD_DIM] * acc_sc[...] + _dot(p.astype(BF16), v_t)
        m_sc[...] = m_new

    n_kt = (t0 + tq - 1) // tk + 1

    def body(kt, carry):
        tile(kt, False)
        return carry

    lax.fori_loop(0, n_kt - 1, body, 0)
    tile(n_kt - 1, True)

    wk = wlen + tq
    w0 = pl.multiple_of(jnp.maximum(t0 - wlen, 0), tq)
    kw_t = kwt_ref[0, 0, :, pl.ds(w0, wk)]
    vw_t = vw_ref[0, 0, pl.ds(w0, wk), :]
    diff = (t0 + lax.broadcasted_iota(jnp.int32, (tq, wk), 0)) - (
        w0 + lax.broadcasted_iota(jnp.int32, (tq, wk), 1))
    wmask = jnp.logical_and(diff >= 0, diff < wlen)
    sw = jnp.where(wmask[None], _dot(q_all, kw_t).reshape(NSA_REP, tq, wk), NEG).reshape(rows, wk)
    mw = jnp.max(sw, axis=-1, keepdims=True)
    ew = jnp.exp(sw - mw)
    inv_w = 1.0 / jnp.maximum(jnp.sum(ew, axis=-1, keepdims=True), TINY)
    o_w = _dot(ew.astype(BF16), vw_t) * inv_w
    o_s = acc_sc[...] * (1.0 / jnp.maximum(l_sc[...], TINY))[:, :HEAD_DIM]

    gate = gate_ref[0, 0]
    oc = oc_ref[0]
    outs = []
    for r in range(NSA_REP):
        rs = slice(r * tq, (r + 1) * tq)
        o_c = oc[:, r * HEAD_DIM:(r + 1) * HEAD_DIM]
        outs.append(gate[:, 3 * r:3 * r + 1] * o_c + gate[:, 3 * r + 1:3 * r + 2] * o_s[rs]
                    + gate[:, 3 * r + 2:3 * r + 3] * o_w[rs])
    o_ref[0] = jnp.concatenate(outs, axis=-1).astype(o_ref.dtype)


def _sel_win(q, ke, vs, kwt, vw, selb, oc, gates, tk):
    bsz, s, _ = q.shape
    g = ke.shape[1]
    nb = selb.shape[3]
    chunk = ke.shape[2] - HEAD_DIM
    nchunk = nb // chunk
    nvar = chunk * SLC_BLOCK // tk
    tq = min(Q_TILE, s)
    rw = NSA_REP * HEAD_DIM
    rows = NSA_REP * tq
    return pl.pallas_call(
        functools.partial(_sel_win_kernel, tq=tq, tk=tk, chunk=chunk, nvar=nvar, nchunk=nchunk,
                          wlen=WINDOW),
        grid=(bsz, g, s // tq),
        in_specs=[
            pl.BlockSpec((1, tq, rw), lambda b, gg, i: (b, i, gg)),
            pl.BlockSpec((1, 1, chunk + HEAD_DIM, s), lambda b, gg, i: (b, gg, 0, 0)),
            pl.BlockSpec((1, 1, s, HEAD_DIM), lambda b, gg, i: (b, gg, 0, 0)),
            pl.BlockSpec((1, 1, HEAD_DIM, s), lambda b, gg, i: (b, gg, 0, 0)),
            pl.BlockSpec((1, 1, s, HEAD_DIM), lambda b, gg, i: (b, gg, 0, 0)),
            pl.BlockSpec((1, 1, tq, nb), lambda b, gg, i: (b, gg, i, 0)),
            pl.BlockSpec((1, tq, rw), lambda b, gg, i: (b, i, gg)),
            pl.BlockSpec((1, 1, tq, LANE), lambda b, gg, i: (b, gg, i, 0)),
        ],
        out_specs=pl.BlockSpec((1, tq, rw), lambda b, gg, i: (b, i, gg)),
        out_shape=jax.ShapeDtypeStruct((bsz, s, g * rw), BF16),
        scratch_shapes=[
            pltpu.VMEM((nchunk, rows, chunk + HEAD_DIM), BF16),
            pltpu.VMEM((rows, LANE), F32),
            pltpu.VMEM((rows, LANE), F32),
            pltpu.VMEM((rows, HEAD_DIM), F32),
        ],
        compiler_params=_params(("parallel", "parallel", "arbitrary")),
        name="sel_win",
    )(q, ke, vs, kwt, vw, selb, oc, gates)


def _retention_kernel(q_ref, k_ref, v_ref, gb_ref, cos_ref, sin_ref, dmat_ref, qdec_ref, kdec_ref,
                      cdec_ref, gng_ref, gnb_ref, o_ref, state_sc, *, nchunk, c):
    @pl.when(pl.program_id(2) == 0)
    def _():
        state_sc[...] = jnp.zeros_like(state_sc)

    dmat = dmat_ref[0]
    qdec = qdec_ref[0]
    kdec = kdec_ref[0]
    cdec = cdec_ref[0]
    kscale = RET_DIM ** -0.5
    half = RET_DIM // 2
    for n in range(nchunk):
        rows = pl.ds(n * c, c)
        cos = cos_ref[rows, :]
        sin = sin_ref[rows, :]
        qf = q_ref[0, rows, :].astype(F32)
        kf = k_ref[0, rows, :].astype(F32)
        qr = qf * cos + pltpu.roll(qf, half, 1) * sin
        kr = (kf * cos + pltpu.roll(kf, half, 1) * sin) * kscale
        v = v_ref[0, rows, :]
        inner = _dot_nt(qr.astype(BF16), kr.astype(BF16)) * dmat
        state = state_sc[...]
        y = _dot(inner.astype(BF16), v) + _dot((qr * qdec).astype(BF16), state.astype(BF16))
        state_sc[...] = state * cdec + _dot_tn((kr * kdec).astype(BF16), v)
        mu = jnp.mean(y, axis=-1, keepdims=True)
        d = y - mu
        var = jnp.mean(d * d, axis=-1, keepdims=True)
        yn = d * lax.rsqrt(var + LN_EPS) * gng_ref[...] + gnb_ref[...]
        gb = gb_ref[0, rows, :].astype(F32)
        o_ref[0, rows, :] = ((gb * _sigmoid(gb)) * yn).astype(o_ref.dtype)


def _retention(ret, cos2, sin2, dmat, qdec, kdec, cdec, gng, gnb):
    bsz, s, _ = ret.shape
    h = RET_HEADS
    c = RET_CHUNK
    tc = min(RET_TILE, s)
    spec = lambda off: pl.BlockSpec((1, tc, RET_DIM), lambda b, hh, j, off=off: (b, j, off + hh))
    hspec = lambda shp: pl.BlockSpec((1,) + shp, lambda b, hh, j: (hh, 0, 0))
    return pl.pallas_call(
        functools.partial(_retention_kernel, nchunk=tc // c, c=c),
        grid=(bsz, h, s // tc),
        in_specs=[
            spec(0), spec(h), spec(2 * h), spec(3 * h),
            pl.BlockSpec((tc, RET_DIM), lambda b, hh, j: (j, 0)),
            pl.BlockSpec((tc, RET_DIM), lambda b, hh, j: (j, 0)),
            hspec((c, c)), hspec((c, RET_DIM)), hspec((c, RET_DIM)), hspec((1, RET_DIM)),
            pl.BlockSpec((1, RET_DIM), lambda b, hh, j: (0, hh)),
            pl.BlockSpec((1, RET_DIM), lambda b, hh, j: (0, hh)),
        ],
        out_specs=pl.BlockSpec((1, tc, RET_DIM), lambda b, hh, j: (b, j, hh)),
        out_shape=jax.ShapeDtypeStruct((bsz, s, h * RET_DIM), BF16),
        scratch_shapes=[pltpu.VMEM((RET_DIM, RET_DIM), F32)],
        compiler_params=_params(("parallel", "parallel", "arbitrary")),
        name="retention",
    )(ret, ret, ret, ret, cos2, sin2, dmat, qdec, kdec, cdec, gng, gnb)


def _merge_kernel(h_ref, oa_ref, ob_ref, wga_ref, wgb_ref, wpa_ref, wpb_ref, o_ref):
    hb = h_ref[...]
    ga = _sigmoid(_dot(hb, wga_ref[...]))
    gb = _sigmoid(_dot(hb, wgb_ref[...]))
    merged = ga * _dot(oa_ref[...], wpa_ref[...]) + gb * _dot(ob_ref[...], wpb_ref[...])
    o_ref[...] = merged.astype(o_ref.dtype)


def _merge(hb, oa, ob, wga, wgb, wpa, wpb):
    n, d = hb.shape
    bm = min(ROW_TILE, n)
    bn = min(COL_TILE, d)
    ka, kb = oa.shape[1], ob.shape[1]
    return pl.pallas_call(
        _merge_kernel,
        grid=(n // bm, d // bn),
        in_specs=[
            pl.BlockSpec((bm, d), lambda i, j: (i, 0)),
            pl.BlockSpec((bm, ka), lambda i, j: (i, 0)),
            pl.BlockSpec((bm, kb), lambda i, j: (i, 0)),
            pl.BlockSpec((d, bn), lambda i, j: (0, j)),
            pl.BlockSpec((d, bn), lambda i, j: (0, j)),
            pl.BlockSpec((ka, bn), lambda i, j: (0, j)),
            pl.BlockSpec((kb, bn), lambda i, j: (0, j)),
        ],
        out_specs=pl.BlockSpec((bm, bn), lambda i, j: (i, j)),
        out_shape=jax.ShapeDtypeStruct((n, d), BF16),
        compiler_params=_params(("parallel", "arbitrary")),
        name="merge",
    )(hb, oa, ob, wga, wgb, wpa, wpb)


def _proj_ln_kernel(x_ref, m_ref, w_ref, g_ref, b_ref, o_ref, *, alpha):
    y = alpha * x_ref[...] + _dot(m_ref[...], w_ref[...])
    o_ref[...] = _layer_norm_rows(y, g_ref[...], b_ref[...])


def _proj_ln(x, m, w, g, b, alpha):
    n, d = x.shape
    bm = min(ROW_TILE, n)
    return pl.pallas_call(
        functools.partial(_proj_ln_kernel, alpha=alpha),
        grid=(n // bm,),
        in_specs=[
            pl.BlockSpec((bm, d), lambda i: (i, 0)),
            pl.BlockSpec((bm, d), lambda i: (i, 0)),
            pl.BlockSpec((d, d), lambda i: (0, 0)),
            pl.BlockSpec((1, d), lambda i: (0, 0)),
            pl.BlockSpec((1, d), lambda i: (0, 0)),
        ],
        out_specs=pl.BlockSpec((bm, d), lambda i: (i, 0)),
        out_shape=jax.ShapeDtypeStruct((n, d), F32),
        compiler_params=_params(("parallel",)),
        name="proj_ln",
    )(x, m, w, g, b)


def _overlap_matrix(nh, nb):
    c0 = np.arange(nh) * CMP_STRIDE
    c1 = c0 + CMP_BLOCK
    s0 = np.arange(nb) * SLC_BLOCK
    s1 = s0 + SLC_BLOCK
    ov = (c0[:, None] < s1[None, :]) & (c1[:, None] > s0[None, :])
    ov[nh - 1, :] = False
    return jnp.asarray(ov.T, BF16)


def _expand_matrix(nb, s):
    chunk = min(LANE, nb)
    blk_in_chunk = (np.arange(s) // SLC_BLOCK) % chunk
    return jnp.asarray(blk_in_chunk[None, :] == np.arange(chunk)[:, None], BF16)


def _retention_tables(s):
    h, c, d = RET_HEADS, RET_CHUNK, RET_DIM
    inv = ROPE_BASE ** (-jnp.arange(0, d, 2, dtype=F32) / d)
    ang = jnp.arange(s)[:, None].astype(F32) * inv[None, :]
    cos, sin = jnp.cos(ang), jnp.sin(ang)
    cos2 = jnp.concatenate([cos, cos], -1)
    sin2 = jnp.concatenate([-sin, sin], -1)
    log_g = jnp.log1p(-jnp.exp2(-5.0 - jnp.arange(h, dtype=F32)))
    i = jnp.arange(c, dtype=F32)
    diff = i[:, None] - i[None, :]
    dmat = jnp.where(diff >= 0, jnp.exp(jnp.maximum(diff, 0.0)[None] * log_g[:, None, None]), 0.0)
    kdec = jnp.exp((c - 1 - i)[None, :] * log_g[:, None])
    qdec = jnp.exp((i + 1)[None, :] * log_g[:, None])
    cdec = jnp.exp(c * log_g)
    bc = lambda t: jnp.broadcast_to(t[:, :, None], (h, c, d))
    return cos2, sin2, dmat, bc(qdec), bc(kdec), jnp.broadcast_to(cdec[:, None, None], (h, 1, d))


def _pad_cols(w, mult):
    pad = (-w.shape[1]) % mult
    return jnp.pad(w, ((0, 0), (0, pad))) if pad else w


def _mixer(hf, hb, bsz, s, w_in, cmp_k, cmp_v, ret_gn_g, ret_gn_b, w_merge_gate, w_proj_a, w_proj_b,
           w_o, ln_g, ln_b, alpha, tables):
    n, d = hf.shape
    g, hd, rw = NSA_GROUPS, HEAD_DIM, NSA_REP * HEAD_DIM
    ov, expand, ret_tabs = tables
    o_nsa = NSA_Q + 6 * NSA_KV
    w_nsa = w_in[:, :o_nsa].astype(BF16)
    w_gate = _pad_cols(w_in[:, o_nsa:o_nsa + NSA_GATE], LANE).astype(BF16)
    w_ret = w_in[:, o_nsa + NSA_GATE:].astype(BF16)

    nsa = _matmul(hb, w_nsa, None, BF16, "proj_nsa")
    ga = _matmul(hb, w_gate, "sigmoid", F32, "proj_gate")
    ret = _matmul(hb, w_ret, None, BF16, "proj_ret")

    q = (nsa[:, :NSA_Q] * (hd ** -0.5)).astype(BF16).reshape(bsz, s, NSA_Q)
    kv = nsa[:, NSA_Q:].reshape(bsz, s, 6, g, hd)
    to_rows = lambda t: jnp.transpose(t, (0, 2, 1, 3))
    to_cols = lambda t: jnp.transpose(t, (0, 2, 3, 1))
    nh = s // CMP_STRIDE
    half = CMP_STRIDE * hd

    def compress(t, prm):
        pe, w1, b1, w2 = prm
        x = to_rows(t).reshape(bsz, g, nh, half)
        return _compress(x, pe.reshape(2, half), w1.astype(BF16), b1.reshape(1, -1), w2.astype(BF16))

    kc = compress(kv[:, :, 0], cmp_k)
    vc = compress(kv[:, :, 1], cmp_v)
    kct = jnp.transpose(kc, (0, 1, 3, 2))
    oc, sel = _cmp_topk(q, kct, vc, ov)

    gates = ga[:, :NSA_GATE].reshape(bsz, s, g, 3 * NSA_REP)
    gates = jnp.pad(jnp.transpose(gates, (0, 2, 1, 3)), ((0, 0), (0, 0), (0, 0), (0, LANE - 3 * NSA_REP)))
    ke = jnp.concatenate([jnp.broadcast_to(expand, (bsz, g) + expand.shape), to_cols(kv[:, :, 2])], axis=2)
    o_a = _sel_win(q, ke, to_rows(kv[:, :, 3]), to_cols(kv[:, :, 4]), to_rows(kv[:, :, 5]),
                   sel, oc, gates, min(K_TILE, s))

    o_b = _retention(ret.reshape(bsz, s, 4 * RET_W), *ret_tabs,
                     ret_gn_g.reshape(1, -1), ret_gn_b.reshape(1, -1))

    merged = _merge(hb, o_a.reshape(n, NSA_Q), o_b.reshape(n, RET_W),
                    w_merge_gate[:, :d].astype(BF16), w_merge_gate[:, d:].astype(BF16),
                    w_proj_a.astype(BF16), w_proj_b.astype(BF16))
    return _proj_ln(hf, merged, w_o.astype(BF16), ln_g.reshape(1, -1), ln_b.reshape(1, -1), alpha)


def kernel(x, ffn1_w_gate, ffn1_w_up, ffn1_w_down, ln1_g, ln1_b, w_in, cmp_k_pe, cmp_k_w1, cmp_k_b1,
           cmp_k_w2, cmp_v_pe, cmp_v_w1, cmp_v_b1, cmp_v_w2, ret_gn_g, ret_gn_b, w_merge_gate, w_proj_a,
           w_proj_b, w_o, ln2_g, ln2_b, ffn2_w_gate, ffn2_w_up, ffn2_w_down, ln3_g, ln3_b):
    bsz, s, d = x.shape
    depth = ffn1_w_gate.shape[0]
    f = ffn1_w_gate.shape[2]
    alpha = (2 * depth) ** 0.25
    fpad = (-f) % min(FF_TILE, f)
    nb = s // SLC_BLOCK
    tables = (_overlap_matrix(s // CMP_STRIDE, nb), _expand_matrix(nb, s), _retention_tables(s))

    def ffn_weights(wg, wu, wd):
        return (jnp.pad(wg, ((0, 0), (0, fpad))).astype(BF16), jnp.pad(wu, ((0, 0), (0, fpad))).astype(BF16),
                jnp.pad(wd, ((0, fpad), (0, 0))).astype(BF16))

    row = lambda t: t.reshape(1, -1)
    xf = x.reshape(bsz * s, d)
    for l in range(depth):
        hf, hb = _ffn_ln(xf, *ffn_weights(ffn1_w_gate[l], ffn1_w_up[l], ffn1_w_down[l]),
                         row(ln1_g[l]), row(ln1_b[l]), alpha, True)
        xf = _mixer(hf, hb, bsz, s, w_in[l],
                    (cmp_k_pe[l], cmp_k_w1[l], cmp_k_b1[l], cmp_k_w2[l]),
                    (cmp_v_pe[l], cmp_v_w1[l], cmp_v_b1[l], cmp_v_w2[l]),
                    ret_gn_g[l], ret_gn_b[l], w_merge_gate[l], w_proj_a[l], w_proj_b[l], w_o[l],
                    ln2_g[l], ln2_b[l], alpha, tables)
        xf, _ = _ffn_ln(xf, *ffn_weights(ffn2_w_gate[l], ffn2_w_up[l], ffn2_w_down[l]),
                        row(ln3_g[l]), row(ln3_b[l]), alpha, False)
    return xf.reshape(bsz, s, d)
```

```python
import functools
import math

import jax
import jax.numpy as jnp
import numpy as np
from jax import lax
from jax.experimental import pallas as pl
from jax.experimental.pallas import tpu as pltpu

NSA_HEADS = 16
NSA_GROUPS = 4
NSA_REP = NSA_HEADS // NSA_GROUPS
HEAD_DIM = 64
CMP_BLOCK = 32
CMP_STRIDE = 16
SLC_BLOCK = 64
SLC_TOPN = 16
N_FORCED = 3
WINDOW = 512
RET_HEADS = 8
RET_DIM = 128
RET_CHUNK = 128
ROPE_BASE = 10000.0
LN_EPS = 1e-5
NEG = -1e30
TINY = 1e-30
NSA_Q = NSA_HEADS * HEAD_DIM
NSA_KV = NSA_GROUPS * HEAD_DIM
NSA_GATE = 3 * NSA_HEADS
GATE_ROWS = 16
ONES_ROWS = 16
QK_SCALE = HEAD_DIM ** -0.5 * math.log2(math.e)
RET_W = RET_HEADS * RET_DIM

LANE = 128
SUBLANE = 8
VMEM_LIMIT = 52 * 1024 * 1024
ROW_TILE = 512
FF_TILE = 512
COL_TILE = 512
Q_TILE = 256
K_TILE = 512
RET_TILE = 1024

BF16 = jnp.bfloat16
F32 = jnp.float32


def _dot(a, b):
    return jnp.dot(a, b, preferred_element_type=F32)


def _dot_nt(a, b):
    return lax.dot_general(a, b, (((1,), (1,)), ((), ())), preferred_element_type=F32)


def _dot_tn(a, b):
    return lax.dot_general(a, b, (((0,), (0,)), ((), ())), preferred_element_type=F32)


def _sigmoid(x):
    return 1.0 / (1.0 + jnp.exp(-x))


def _layer_norm_rows(y, g, b):
    mu = jnp.mean(y, axis=-1, keepdims=True)
    d = y - mu
    var = jnp.mean(d * d, axis=-1, keepdims=True)
    return d * lax.rsqrt(var + LN_EPS) * g + b


def _params(sem):
    return pltpu.CompilerParams(dimension_semantics=sem, vmem_limit_bytes=VMEM_LIMIT)


def _ffn_ln_kernel(x_ref, wg_ref, wu_ref, wd_ref, g_ref, b_ref, *rest, alpha, nf, with_bf16):
    if with_bf16:
        o_ref, ob_ref, xb_sc, acc_sc = rest
    else:
        o_ref, xb_sc, acc_sc = rest
        ob_ref = None
    f = pl.program_id(1)

    @pl.when(f == 0)
    def _():
        xb_sc[...] = x_ref[...].astype(BF16)
        acc_sc[...] = jnp.zeros_like(acc_sc)

    xb = xb_sc[...]
    a = _dot(xb, wg_ref[...])
    u = _dot(xb, wu_ref[...])
    h = (a * _sigmoid(a)) * u
    acc_sc[...] += _dot(h.astype(BF16), wd_ref[...])

    @pl.when(f == nf - 1)
    def _():
        y = alpha * x_ref[...] + 0.5 * acc_sc[...]
        out = _layer_norm_rows(y, g_ref[...], b_ref[...])
        o_ref[...] = out
        if with_bf16:
            ob_ref[...] = out.astype(BF16)


def _ffn_ln(x, wg, wu, wd, g, b, alpha, with_bf16):
    n, d = x.shape
    fp = wg.shape[1]
    bm = min(ROW_TILE, n)
    bf = min(FF_TILE, fp)
    nf = fp // bf
    out_shape = [jax.ShapeDtypeStruct((n, d), F32)]
    out_specs = [pl.BlockSpec((bm, d), lambda i, f: (i, 0))]
    if with_bf16:
        out_shape.append(jax.ShapeDtypeStruct((n, d), BF16))
        out_specs.append(pl.BlockSpec((bm, d), lambda i, f: (i, 0)))
    res = pl.pallas_call(
        functools.partial(_ffn_ln_kernel, alpha=alpha, nf=nf, with_bf16=with_bf16),
        grid=(n // bm, nf),
        in_specs=[
            pl.BlockSpec((bm, d), lambda i, f: (i, 0)),
            pl.BlockSpec((d, bf), lambda i, f: (0, f)),
            pl.BlockSpec((d, bf), lambda i, f: (0, f)),
            pl.BlockSpec((bf, d), lambda i, f: (f, 0)),
            pl.BlockSpec((1, d), lambda i, f: (0, 0)),
            pl.BlockSpec((1, d), lambda i, f: (0, 0)),
        ],
        out_specs=out_specs,
        out_shape=out_shape,
        scratch_shapes=[pltpu.VMEM((bm, d), BF16), pltpu.VMEM((bm, d), F32)],
        compiler_params=_params(("parallel", "arbitrary")),
        name="ffn_ln",
    )(x, wg, wu, wd, g, b)
    return res if with_bf16 else (res[0], None)


def _mm_kernel(x_ref, w_ref, o_ref, *, act, scale):
    y = _dot(x_ref[...], w_ref[...])
    if scale is not None:
        y = y * scale
    if act == "sigmoid":
        y = _sigmoid(y)
    o_ref[...] = y.astype(o_ref.dtype)


def _matmul(x, w, act, out_dtype, name, scale=None):
    n, k = x.shape
    nout = w.shape[1]
    bm = min(ROW_TILE, n)
    bn = min(COL_TILE, nout)
    assert nout % bn == 0 and n % bm == 0
    return pl.pallas_call(
        functools.partial(_mm_kernel, act=act, scale=scale),
        grid=(n // bm, nout // bn),
        in_specs=[
            pl.BlockSpec((bm, k), lambda i, j: (i, 0)),
            pl.BlockSpec((k, bn), lambda i, j: (0, j)),
        ],
        out_specs=pl.BlockSpec((bm, bn), lambda i, j: (i, j)),
        out_shape=jax.ShapeDtypeStruct((n, nout), out_dtype),
        compiler_params=_params(("parallel", "arbitrary")),
        name=name,
    )(x, w)


def _compress_kernel(x_ref, pe_ref, w1_ref, b1_ref, w2_ref, o_ref, *, nh, half):
    x = x_ref[0, 0].astype(F32)
    xa = (x + pe_ref[0:1, :]).astype(BF16)
    xb = (x + pe_ref[1:2, :]).astype(BF16)
    ha = _dot(xa, w1_ref[0:half, :])
    hb = _dot(xb, w1_ref[half:2 * half, :])
    hid = ha + pltpu.roll(hb, nh - 1, 0) + b1_ref[...]
    c = math.sqrt(2.0 / math.pi)
    act = 0.5 * hid * (1.0 + jnp.tanh(c * (hid + 0.044715 * (hid * hid * hid))))
    o_ref[0, 0] = _dot(act.astype(BF16), w2_ref[...]).astype(o_ref.dtype)


def _compress(x, pe2, w1, b1, w2):
    bsz, g, nh, half = x.shape
    hid = w1.shape[1]
    hd = w2.shape[1]
    return pl.pallas_call(
        functools.partial(_compress_kernel, nh=nh, half=half),
        grid=(bsz, g),
        in_specs=[
            pl.BlockSpec((1, 1, nh, half), lambda b, gg: (b, gg, 0, 0)),
            pl.BlockSpec((2, half), lambda b, gg: (0, 0)),
            pl.BlockSpec((2 * half, hid), lambda b, gg: (0, 0)),
            pl.BlockSpec((1, hid), lambda b, gg: (0, 0)),
            pl.BlockSpec((hid, hd), lambda b, gg: (0, 0)),
        ],
        out_specs=pl.BlockSpec((1, 1, nh, hd), lambda b, gg: (b, gg, 0, 0)),
        out_shape=jax.ShapeDtypeStruct((bsz, g, nh, hd), BF16),
        compiler_params=_params(("parallel", "parallel")),
        name="compress",
    )(x, pe2, w1, b1, w2)


def _stack_heads(qt_ref):
    return jnp.concatenate([qt_ref[0, r] for r in range(NSA_REP)], axis=1)


def _per_head(x):
    return jnp.tile(x, (1, NSA_REP))


def _cmp_topk_kernel(qt_ref, kc_ref, vct_ref, ovt_ref, oct_ref, selt_ref, *, tq, nh, nb, n_sel):
    t0 = pl.program_id(2) * tq
    cend = lax.broadcasted_iota(jnp.int32, (nh, tq), 0) * CMP_STRIDE + (CMP_BLOCK - 1)
    tpos = t0 + lax.broadcasted_iota(jnp.int32, (nh, tq), 1)
    bias = jnp.where(cend <= tpos, 0.0, NEG)
    seen = jnp.where(t0 + lax.broadcasted_iota(jnp.int32, (1, tq), 1) >= CMP_BLOCK - 1, 1.0, 0.0)
    qt = _stack_heads(qt_ref)
    s = _dot(kc_ref[0, 0], qt) + _per_head(bias)
    e = jnp.exp2(s - jnp.max(s, axis=0, keepdims=True))
    ea = _dot(vct_ref[0, 0], e.astype(BF16))
    inv = _per_head(seen) / jnp.maximum(ea[HEAD_DIM:HEAD_DIM + 1], TINY)
    oct_ref[0, 0] = ea[0:HEAD_DIM] * inv
    p = e * inv
    psum = p[:, 0:tq]
    for r in range(1, NSA_REP):
        psum = psum + p[:, r * tq:(r + 1) * tq]
    imp = _dot(ovt_ref[...], psum.astype(BF16))
    blk = lax.broadcasted_iota(jnp.int32, (nb, tq), 0).astype(F32)
    cur = ((t0 + lax.broadcasted_iota(jnp.int32, (nb, tq), 1)) // SLC_BLOCK).astype(F32)
    forced = (blk == 0.0) | (blk == cur) | (blk == cur - 1.0)
    x = jnp.where(forced, -jnp.inf, jnp.where(blk <= cur, imp, NEG))
    selb = jnp.where(forced, 0.0, NEG)
    for _ in range(n_sel - N_FORCED):
        mx = jnp.max(x, axis=0, keepdims=True)
        idx = jnp.min(jnp.where(x == mx, blk, float(nb)), axis=0, keepdims=True)
        hit = blk == idx
        selb = jnp.where(hit, 0.0, selb)
        x = jnp.where(hit, -jnp.inf, x)
    selt_ref[0, 0] = selb.astype(selt_ref.dtype)


def _cmp_topk(qt, kc, vct, ovt):
    bsz, _, hd, s = qt.shape
    g, nh = kc.shape[1], kc.shape[2]
    nb = ovt.shape[0]
    assert nb >= SLC_TOPN
    tq = min(Q_TILE, s)
    cols = NSA_REP * tq
    return pl.pallas_call(
        functools.partial(_cmp_topk_kernel, tq=tq, nh=nh, nb=nb, n_sel=SLC_TOPN),
        grid=(bsz, g, s // tq),
        in_specs=[
            pl.BlockSpec((1, NSA_REP, hd, tq), lambda b, gg, i: (b, gg, 0, i)),
            pl.BlockSpec((1, 1, nh, hd), lambda b, gg, i: (b, gg, 0, 0)),
            pl.BlockSpec((1, 1, hd + ONES_ROWS, nh), lambda b, gg, i: (b, gg, 0, 0)),
            pl.BlockSpec((nb, nh), lambda b, gg, i: (0, 0)),
        ],
        out_specs=[
            pl.BlockSpec((1, 1, hd, cols), lambda b, gg, i: (b, gg, 0, i)),
            pl.BlockSpec((1, 1, nb, tq), lambda b, gg, i: (b, gg, 0, i)),
        ],
        out_shape=[
            jax.ShapeDtypeStruct((bsz, g, hd, NSA_REP * s), F32),
            jax.ShapeDtypeStruct((bsz, g, nb, s), BF16),
        ],
        compiler_params=_params(("parallel", "parallel", "arbitrary")),
        name="cmp_topk",
    )(qt, kc, vct, ovt)


def _sel_win_kernel(qt_ref, ket_ref, vst_ref, kw_ref, vwt_ref, selt_ref, oct_ref, gatet_ref,
                    o_ref, rhs_sc, m_sc, acc_sc, *, tq, tk, chunk, nvar, nchunk, wlen):
    t0 = pl.program_id(2) * tq
    qt = _stack_heads(qt_ref)
    for c in range(nchunk):
        selt = selt_ref[0, 0, c * chunk:(c + 1) * chunk, :]
        rhs_sc[c] = jnp.concatenate([_per_head(selt), qt], axis=0)

    m_sc[...] = jnp.full_like(m_sc, NEG)
    acc_sc[...] = jnp.zeros_like(acc_sc)

    def scores(kt, causal):
        k0 = pl.multiple_of(kt * tk, tk)
        s = _dot(ket_ref[0, 0, pl.ds(k0, tk), :], rhs_sc[kt // nvar])
        if causal:
            kpos = k0 + lax.broadcasted_iota(jnp.int32, (tk, tq), 0)
            tcol = t0 + lax.broadcasted_iota(jnp.int32, (tk, tq), 1)
            s = s + _per_head(jnp.where(kpos <= tcol, 0.0, NEG))
        return s

    def accumulate(kt, s):
        k0 = pl.multiple_of(kt * tk, tk)
        m_old = m_sc[...]
        m_new = jnp.maximum(m_old, jnp.max(s, axis=0, keepdims=True))
        alpha = jnp.exp2(m_old[0:1] - m_new[0:1])
        p = jnp.exp2(s - m_new[0:1])
        acc_sc[...] = alpha * acc_sc[...] + _dot(vst_ref[0, 0, :, pl.ds(k0, tk)], p.astype(BF16))
        m_sc[...] = m_new

    def pair(kt, causal):
        sa = scores(kt, False)
        sb = scores(kt + 1, causal)
        accumulate(kt, sa)
        accumulate(kt + 1, sb)

    nfull = (t0 + tq - 1) // tk

    def body(j, carry):
        pair(2 * j, False)
        return carry

    lax.fori_loop(0, nfull // 2, body, 0)

    @pl.when(nfull % 2 == 1)
    def _():
        pair(nfull - 1, True)

    @pl.when(nfull % 2 == 0)
    def _():
        accumulate(nfull, scores(nfull, True))

    o_s = acc_sc[0:HEAD_DIM] * (1.0 / jnp.maximum(acc_sc[HEAD_DIM:HEAD_DIM + 1], TINY))

    wk = wlen + tq
    w0 = pl.multiple_of(jnp.maximum(t0 - wlen, 0), tq)
    diff = (t0 + lax.broadcasted_iota(jnp.int32, (wk, tq), 1)) - (
        w0 + lax.broadcasted_iota(jnp.int32, (wk, tq), 0))
    wbias = jnp.where(jnp.logical_and(diff >= 0, diff < wlen), 0.0, NEG)
    sw = _dot(kw_ref[0, 0, pl.ds(w0, wk), :], qt) + _per_head(wbias)
    ew = jnp.exp2(sw - jnp.max(sw, axis=0, keepdims=True))
    ow = _dot(vwt_ref[0, 0, :, pl.ds(w0, wk)], ew.astype(BF16))
    o_w = ow[0:HEAD_DIM] * (1.0 / jnp.maximum(ow[HEAD_DIM:HEAD_DIM + 1], TINY))

    gate = gatet_ref[0, 0]
    o_c = oct_ref[0, 0]
    outs = []
    for r in range(NSA_REP):
        cs = slice(r * tq, (r + 1) * tq)
        outs.append(gate[3 * r:3 * r + 1] * o_c[:, cs] + gate[3 * r + 1:3 * r + 2] * o_s[:, cs]
                    + gate[3 * r + 2:3 * r + 3] * o_w[:, cs])
    pairs = [jnp.concatenate(outs[r:r + 2], axis=0).T for r in range(0, NSA_REP, 2)]
    o_ref[0] = jnp.concatenate(pairs, axis=1).astype(o_ref.dtype)


def _sel_win(qt, ket, vst, kw, vwt, selt, oct, gatet, tk):
    bsz, _, hd, s = qt.shape
    g = ket.shape[1]
    nb = selt.shape[2]
    chunk = ket.shape[3] - hd
    nchunk = nb // chunk
    nvar = chunk * SLC_BLOCK // tk
    tq = min(Q_TILE, s)
    cols = NSA_REP * tq
    return pl.pallas_call(
        functools.partial(_sel_win_kernel, tq=tq, tk=tk, chunk=chunk, nvar=nvar, nchunk=nchunk,
                          wlen=WINDOW),
        grid=(bsz, g, s // tq),
        in_specs=[
            pl.BlockSpec((1, NSA_REP, hd, tq), lambda b, gg, i: (b, gg, 0, i)),
            pl.BlockSpec((1, 1, s, chunk + hd), lambda b, gg, i: (b, gg, 0, 0)),
            pl.BlockSpec((1, 1, hd + ONES_ROWS, s), lambda b, gg, i: (b, gg, 0, 0)),
            pl.BlockSpec((1, 1, s, hd), lambda b, gg, i: (b, gg, 0, 0)),
            pl.BlockSpec((1, 1, hd + ONES_ROWS, s), lambda b, gg, i: (b, gg, 0, 0)),
            pl.BlockSpec((1, 1, nb, tq), lambda b, gg, i: (b, gg, 0, i)),
            pl.BlockSpec((1, 1, hd, cols), lambda b, gg, i: (b, gg, 0, i)),
            pl.BlockSpec((1, 1, GATE_ROWS, tq), lambda b, gg, i: (b, gg, 0, i)),
        ],
        out_specs=pl.BlockSpec((1, tq, NSA_REP * hd), lambda b, gg, i: (b, i, gg)),
        out_shape=jax.ShapeDtypeStruct((bsz, s, g * NSA_REP * hd), BF16),
        scratch_shapes=[
            pltpu.VMEM((nchunk, chunk + hd, cols), BF16),
            pltpu.VMEM((SUBLANE, cols), F32),
            pltpu.VMEM((hd + ONES_ROWS, cols), F32),
        ],
        compiler_params=_params(("parallel", "parallel", "arbitrary")),
        name="sel_win",
    )(qt, ket, vst, kw, vwt, selt, oct, gatet)


def _retention_kernel(q_ref, k_ref, v_ref, gb_ref, cos_ref, sin_ref, dmat_ref, qdec_ref, kdec_ref,
                      cdec_ref, gng_ref, gnb_ref, o_ref, state_sc, *, nchunk, c):
    @pl.when(pl.program_id(2) == 0)
    def _():
        state_sc[...] = jnp.zeros_like(state_sc)

    dmat = dmat_ref[0]
    qdec = qdec_ref[0]
    kdec = kdec_ref[0]
    cdec = cdec_ref[0]
    kscale = RET_DIM ** -0.5
    half = RET_DIM // 2
    for n in range(nchunk):
        rows = pl.ds(n * c, c)
        cos = cos_ref[rows, :]
        sin = sin_ref[rows, :]
        qf = q_ref[0, rows, :].astype(F32)
        kf = k_ref[0, rows, :].astype(F32)
        qr = qf * cos + pltpu.roll(qf, half, 1) * sin
        kr = (kf * cos + pltpu.roll(kf, half, 1) * sin) * kscale
        v = v_ref[0, rows, :]
        inner = _dot_nt(qr.astype(BF16), kr.astype(BF16)) * dmat
        state = state_sc[...]
        y = _dot(inner.astype(BF16), v) + _dot((qr * qdec).astype(BF16), state.astype(BF16))
        state_sc[...] = state * cdec + _dot_tn((kr * kdec).astype(BF16), v)
        mu = jnp.mean(y, axis=-1, keepdims=True)
        d = y - mu
        var = jnp.mean(d * d, axis=-1, keepdims=True)
        yn = d * lax.rsqrt(var + LN_EPS) * gng_ref[...] + gnb_ref[...]
        gb = gb_ref[0, rows, :].astype(F32)
        o_ref[0, rows, :] = ((gb * _sigmoid(gb)) * yn).astype(o_ref.dtype)


def _retention(ret, cos2, sin2, dmat, qdec, kdec, cdec, gng, gnb):
    bsz, s, _ = ret.shape
    h = RET_HEADS
    c = RET_CHUNK
    tc = min(RET_TILE, s)
    spec = lambda off: pl.BlockSpec((1, tc, RET_DIM), lambda b, hh, j, off=off: (b, j, off + hh))
    hspec = lambda shp: pl.BlockSpec((1,) + shp, lambda b, hh, j: (hh, 0, 0))
    return pl.pallas_call(
        functools.partial(_retention_kernel, nchunk=tc // c, c=c),
        grid=(bsz, h, s // tc),
        in_specs=[
            spec(0), spec(h), spec(2 * h), spec(3 * h),
            pl.BlockSpec((tc, RET_DIM), lambda b, hh, j: (j, 0)),
            pl.BlockSpec((tc, RET_DIM), lambda b, hh, j: (j, 0)),
            hspec((c, c)), hspec((c, RET_DIM)), hspec((c, RET_DIM)), hspec((1, RET_DIM)),
            pl.BlockSpec((1, RET_DIM), lambda b, hh, j: (0, hh)),
            pl.BlockSpec((1, RET_DIM), lambda b, hh, j: (0, hh)),
        ],
        out_specs=pl.BlockSpec((1, tc, RET_DIM), lambda b, hh, j: (b, j, hh)),
        out_shape=jax.ShapeDtypeStruct((bsz, s, h * RET_DIM), BF16),
        scratch_shapes=[pltpu.VMEM((RET_DIM, RET_DIM), F32)],
        compiler_params=_params(("parallel", "parallel", "arbitrary")),
        name="retention",
    )(ret, ret, ret, ret, cos2, sin2, dmat, qdec, kdec, cdec, gng, gnb)


def _merge_kernel(h_ref, oa_ref, ob_ref, wga_ref, wgb_ref, wpa_ref, wpb_ref, o_ref):
    hb = h_ref[...]
    ga = _sigmoid(_dot(hb, wga_ref[...]))
    gb = _sigmoid(_dot(hb, wgb_ref[...]))
    merged = ga * _dot(oa_ref[...], wpa_ref[...]) + gb * _dot(ob_ref[...], wpb_ref[...])
    o_ref[...] = merged.astype(o_ref.dtype)


def _merge(hb, oa, ob, wga, wgb, wpa, wpb):
    n, d = hb.shape
    bm = min(ROW_TILE, n)
    bn = min(COL_TILE, d)
    ka, kb = oa.shape[1], ob.shape[1]
    return pl.pallas_call(
        _merge_kernel,
        grid=(n // bm, d // bn),
        in_specs=[
            pl.BlockSpec((bm, d), lambda i, j: (i, 0)),
            pl.BlockSpec((bm, ka), lambda i, j: (i, 0)),
            pl.BlockSpec((bm, kb), lambda i, j: (i, 0)),
            pl.BlockSpec((d, bn), lambda i, j: (0, j)),
            pl.BlockSpec((d, bn), lambda i, j: (0, j)),
            pl.BlockSpec((ka, bn), lambda i, j: (0, j)),
            pl.BlockSpec((kb, bn), lambda i, j: (0, j)),
        ],
        out_specs=pl.BlockSpec((bm, bn), lambda i, j: (i, j)),
        out_shape=jax.ShapeDtypeStruct((n, d), BF16),
        compiler_params=_params(("parallel", "arbitrary")),
        name="merge",
    )(hb, oa, ob, wga, wgb, wpa, wpb)


def _proj_ln_kernel(x_ref, m_ref, w_ref, g_ref, b_ref, o_ref, *, alpha):
    y = alpha * x_ref[...] + _dot(m_ref[...], w_ref[...])
    o_ref[...] = _layer_norm_rows(y, g_ref[...], b_ref[...])


def _proj_ln(x, m, w, g, b, alpha):
    n, d = x.shape
    bm = min(ROW_TILE, n)
    return pl.pallas_call(
        functools.partial(_proj_ln_kernel, alpha=alpha),
        grid=(n // bm,),
        in_specs=[
            pl.BlockSpec((bm, d), lambda i: (i, 0)),
            pl.BlockSpec((bm, d), lambda i: (i, 0)),
            pl.BlockSpec((d, d), lambda i: (0, 0)),
            pl.BlockSpec((1, d), lambda i: (0, 0)),
            pl.BlockSpec((1, d), lambda i: (0, 0)),
        ],
        out_specs=pl.BlockSpec((bm, d), lambda i: (i, 0)),
        out_shape=jax.ShapeDtypeStruct((n, d), F32),
        compiler_params=_params(("parallel",)),
        name="proj_ln",
    )(x, m, w, g, b)


def _overlap_matrix(nh, nb):
    c0 = np.arange(nh) * CMP_STRIDE
    c1 = c0 + CMP_BLOCK
    s0 = np.arange(nb) * SLC_BLOCK
    s1 = s0 + SLC_BLOCK
    ov = (c0[:, None] < s1[None, :]) & (c1[:, None] > s0[None, :])
    ov[nh - 1, :] = False
    return jnp.asarray(ov.T, BF16)


def _expand_matrix(nb, s):
    chunk = min(LANE, nb)
    blk_in_chunk = (np.arange(s) // SLC_BLOCK) % chunk
    return jnp.asarray(blk_in_chunk[:, None] == np.arange(chunk)[None, :], BF16)


def _retention_tables(s):
    h, c, d = RET_HEADS, RET_CHUNK, RET_DIM
    inv = ROPE_BASE ** (-jnp.arange(0, d, 2, dtype=F32) / d)
    ang = jnp.arange(s)[:, None].astype(F32) * inv[None, :]
    cos, sin = jnp.cos(ang), jnp.sin(ang)
    cos2 = jnp.concatenate([cos, cos], -1)
    sin2 = jnp.concatenate([-sin, sin], -1)
    log_g = jnp.log1p(-jnp.exp2(-5.0 - jnp.arange(h, dtype=F32)))
    i = jnp.arange(c, dtype=F32)
    diff = i[:, None] - i[None, :]
    dmat = jnp.where(diff >= 0, jnp.exp(jnp.maximum(diff, 0.0)[None] * log_g[:, None, None]), 0.0)
    kdec = jnp.exp((c - 1 - i)[None, :] * log_g[:, None])
    qdec = jnp.exp((i + 1)[None, :] * log_g[:, None])
    cdec = jnp.exp(c * log_g)
    bc = lambda t: jnp.broadcast_to(t[:, :, None], (h, c, d))
    return cos2, sin2, dmat, bc(qdec), bc(kdec), jnp.broadcast_to(cdec[:, None, None], (h, 1, d))


def _pad_cols(w, mult):
    pad = (-w.shape[1]) % mult
    return jnp.pad(w, ((0, 0), (0, pad))) if pad else w


def _mixer(hf, hb, bsz, s, w_in, cmp_k, cmp_v, ret_gn_g, ret_gn_b, w_merge_gate, w_proj_a, w_proj_b,
           w_o, ln_g, ln_b, alpha, tables):
    n, d = hf.shape
    g, hd = NSA_GROUPS, HEAD_DIM
    ovt, expand, ret_tabs = tables
    o_nsa = NSA_Q + 6 * NSA_KV
    w_q = w_in[:, :NSA_Q].astype(BF16)
    w_kv = w_in[:, NSA_Q:o_nsa].astype(BF16)
    w_gate = _pad_cols(w_in[:, o_nsa:o_nsa + NSA_GATE], LANE).astype(BF16)
    w_ret = w_in[:, o_nsa + NSA_GATE:].astype(BF16)

    q = _matmul(hb, w_q, None, BF16, "proj_q", scale=QK_SCALE)
    kv = _matmul(hb, w_kv, None, BF16, "proj_kv")
    ga = _matmul(hb, w_gate, "sigmoid", F32, "proj_gate")
    ret = _matmul(hb, w_ret, None, BF16, "proj_ret")

    qt = jnp.transpose(q.reshape(bsz, s, NSA_HEADS, hd), (0, 2, 3, 1))
    kv = kv.reshape(bsz, s, 6, g, hd)
    to_rows = lambda t: jnp.transpose(t, (0, 2, 1, 3))
    ones_rows = jnp.zeros((bsz, g, ONES_ROWS, 1), BF16).at[:, :, 0].set(1.0)

    def to_cols_ones(t):
        vt = jnp.transpose(t, (0, 2, 3, 1))
        return jnp.concatenate([vt, jnp.broadcast_to(ones_rows, vt.shape[:2] + (ONES_ROWS, vt.shape[3]))],
                               axis=2)

    nh = s // CMP_STRIDE
    half = CMP_STRIDE * hd

    def compress(t, prm):
        pe, w1, b1, w2 = prm
        x = to_rows(t).reshape(bsz, g, nh, half)
        return _compress(x, pe.reshape(2, half), w1.astype(BF16), b1.reshape(1, -1), w2.astype(BF16))

    kc = compress(kv[:, :, 0], cmp_k)
    vct = to_cols_ones(jnp.transpose(compress(kv[:, :, 1], cmp_v), (0, 2, 1, 3)))
    oct, selt = _cmp_topk(qt, kc, vct, ovt)

    gatet = jnp.transpose(ga[:, :NSA_GATE].reshape(bsz, s, g, 3 * NSA_REP), (0, 2, 3, 1))
    gatet = jnp.pad(gatet, ((0, 0), (0, 0), (0, GATE_ROWS - 3 * NSA_REP), (0, 0)))
    ket = jnp.concatenate([jnp.broadcast_to(expand, (bsz, g) + expand.shape), to_rows(kv[:, :, 2])], axis=3)
    o_a = _sel_win(qt, ket, to_cols_ones(kv[:, :, 3]), to_rows(kv[:, :, 4]), to_cols_ones(kv[:, :, 5]),
                   selt, oct, gatet, min(K_TILE, s))

    o_b = _retention(ret.reshape(bsz, s, 4 * RET_W), *ret_tabs,
                     ret_gn_g.reshape(1, -1), ret_gn_b.reshape(1, -1))

    merged = _merge(hb, o_a.reshape(n, NSA_Q), o_b.reshape(n, RET_W),
                    w_merge_gate[:, :d].astype(BF16), w_merge_gate[:, d:].astype(BF16),
                    w_proj_a.astype(BF16), w_proj_b.astype(BF16))
    return _proj_ln(hf, merged, w_o.astype(BF16), ln_g.reshape(1, -1), ln_b.reshape(1, -1), alpha)


def kernel(x, ffn1_w_gate, ffn1_w_up, ffn1_w_down, ln1_g, ln1_b, w_in, cmp_k_pe, cmp_k_w1, cmp_k_b1,
           cmp_k_w2, cmp_v_pe, cmp_v_w1, cmp_v_b1, cmp_v_w2, ret_gn_g, ret_gn_b, w_merge_gate, w_proj_a,
           w_proj_b, w_o, ln2_g, ln2_b, ffn2_w_gate, ffn2_w_up, ffn2_w_down, ln3_g, ln3_b):
    bsz, s, d = x.shape
    depth = ffn1_w_gate.shape[0]
    f = ffn1_w_gate.shape[2]
    alpha = (2 * depth) ** 0.25
    fpad = (-f) % min(FF_TILE, f)
    nb = s // SLC_BLOCK
    tables = (_overlap_matrix(s // CMP_STRIDE, nb), _expand_matrix(nb, s), _retention_tables(s))

    def ffn_weights(wg, wu, wd):
        return (jnp.pad(wg, ((0, 0), (0, fpad))).astype(BF16), jnp.pad(wu, ((0, 0), (0, fpad))).astype(BF16),
                jnp.pad(wd, ((0, fpad), (0, 0))).astype(BF16))

    row = lambda t: t.reshape(1, -1)
    xf = x.reshape(bsz * s, d)
    for l in range(depth):
        hf, hb = _ffn_ln(xf, *ffn_weights(ffn1_w_gate[l], ffn1_w_up[l], ffn1_w_down[l]),
                         row(ln1_g[l]), row(ln1_b[l]), alpha, True)
        xf = _mixer(hf, hb, bsz, s, w_in[l],
                    (cmp_k_pe[l], cmp_k_w1[l], cmp_k_b1[l], cmp_k_w2[l]),
                    (cmp_v_pe[l], cmp_v_w1[l], cmp_v_b1[l], cmp_v_w2[l]),
                    ret_gn_g[l], ret_gn_b[l], w_merge_gate[l], w_proj_a[l], w_proj_b[l], w_o[l],
                    ln2_g[l], ln2_b[l], alpha, tables)
        xf, _ = _ffn_ln(xf, *ffn_weights(ffn2_w_gate[l], ffn2_w_up[l], ffn2_w_down[l]),
                        row(ln3_g[l]), row(ln3_b[l]), alpha, False)
    return xf.reshape(bsz, s, d)
```

```python
import functools
import math

import jax
import jax.numpy as jnp
import numpy as np
from jax import lax
from jax.experimental import pallas as pl
from jax.experimental.pallas import tpu as pltpu

NSA_HEADS = 16
NSA_GROUPS = 4
NSA_REP = NSA_HEADS // NSA_GROUPS
HEAD_DIM = 64
CMP_BLOCK = 32
CMP_STRIDE = 16
SLC_BLOCK = 64
SLC_TOPN = 16
N_FORCED = 3
WINDOW = 512
RET_HEADS = 8
RET_DIM = 128
RET_CHUNK = 128
ROPE_BASE = 10000.0
LN_EPS = 1e-5
NEG = -1e30
TINY = 1e-30
NSA_Q = NSA_HEADS * HEAD_DIM
NSA_KV = NSA_GROUPS * HEAD_DIM
NSA_GATE = 3 * NSA_HEADS
GATE_ROWS = 16
ONES_ROWS = 16
QK_SCALE = HEAD_DIM ** -0.5 * math.log2(math.e)
RET_W = RET_HEADS * RET_DIM

LANE = 128
SUBLANE = 8
VMEM_LIMIT = 52 * 1024 * 1024
ROW_TILE = 512
FF_TILE = 512
COL_TILE = 512
Q_TILE = 256
K_TILE = 512
RET_TILE = 1024

BF16 = jnp.bfloat16
F32 = jnp.float32


def _dot(a, b):
    return jnp.dot(a, b, preferred_element_type=F32)


def _dot_nt(a, b):
    return lax.dot_general(a, b, (((1,), (1,)), ((), ())), preferred_element_type=F32)


def _dot_tn(a, b):
    return lax.dot_general(a, b, (((0,), (0,)), ((), ())), preferred_element_type=F32)


def _sigmoid(x):
    return 1.0 / (1.0 + jnp.exp(-x))


def _layer_norm_rows(y, g, b):
    mu = jnp.mean(y, axis=-1, keepdims=True)
    d = y - mu
    var = jnp.mean(d * d, axis=-1, keepdims=True)
    return d * lax.rsqrt(var + LN_EPS) * g + b


def _params(sem):
    return pltpu.CompilerParams(dimension_semantics=sem, vmem_limit_bytes=VMEM_LIMIT)


def _ffn_ln_kernel(x_ref, wg_ref, wu_ref, wd_ref, g_ref, b_ref, *rest, alpha, nf, with_bf16):
    if with_bf16:
        o_ref, ob_ref, xb_sc, acc_sc = rest
    else:
        o_ref, xb_sc, acc_sc = rest
        ob_ref = None
    f = pl.program_id(1)

    @pl.when(f == 0)
    def _():
        xb_sc[...] = x_ref[...].astype(BF16)
        acc_sc[...] = jnp.zeros_like(acc_sc)

    xb = xb_sc[...]
    a = _dot(xb, wg_ref[...])
    u = _dot(xb, wu_ref[...])
    h = (a * _sigmoid(a)) * u
    acc_sc[...] += _dot(h.astype(BF16), wd_ref[...])

    @pl.when(f == nf - 1)
    def _():
        y = alpha * x_ref[...] + 0.5 * acc_sc[...]
        out = _layer_norm_rows(y, g_ref[...], b_ref[...])
        o_ref[...] = out
        if with_bf16:
            ob_ref[...] = out.astype(BF16)


def _ffn_ln(x, wg, wu, wd, g, b, alpha, with_bf16):
    n, d = x.shape
    fp = wg.shape[1]
    bm = min(ROW_TILE, n)
    bf = min(FF_TILE, fp)
    nf = fp // bf
    out_shape = [jax.ShapeDtypeStruct((n, d), F32)]
    out_specs = [pl.BlockSpec((bm, d), lambda i, f: (i, 0))]
    if with_bf16:
        out_shape.append(jax.ShapeDtypeStruct((n, d), BF16))
        out_specs.append(pl.BlockSpec((bm, d), lambda i, f: (i, 0)))
    res = pl.pallas_call(
        functools.partial(_ffn_ln_kernel, alpha=alpha, nf=nf, with_bf16=with_bf16),
        grid=(n // bm, nf),
        in_specs=[
            pl.BlockSpec((bm, d), lambda i, f: (i, 0)),
            pl.BlockSpec((d, bf), lambda i, f: (0, f)),
            pl.BlockSpec((d, bf), lambda i, f: (0, f)),
            pl.BlockSpec((bf, d), lambda i, f: (f, 0)),
            pl.BlockSpec((1, d), lambda i, f: (0, 0)),
            pl.BlockSpec((1, d), lambda i, f: (0, 0)),
        ],
        out_specs=out_specs,
        out_shape=out_shape,
        scratch_shapes=[pltpu.VMEM((bm, d), BF16), pltpu.VMEM((bm, d), F32)],
        compiler_params=_params(("parallel", "arbitrary")),
        name="ffn_ln",
    )(x, wg, wu, wd, g, b)
    return res if with_bf16 else (res[0], None)


def _mm_kernel(x_ref, w_ref, o_ref, *, act, scale):
    y = _dot(x_ref[...], w_ref[...])
    if scale is not None:
        y = y * scale
    if act == "sigmoid":
        y = _sigmoid(y)
    o_ref[...] = y.astype(o_ref.dtype)


def _matmul(x, w, act, out_dtype, name, scale=None):
    n, k = x.shape
    nout = w.shape[1]
    bm = min(ROW_TILE, n)
    bn = min(COL_TILE, nout)
    assert nout % bn == 0 and n % bm == 0
    return pl.pallas_call(
        functools.partial(_mm_kernel, act=act, scale=scale),
        grid=(n // bm, nout // bn),
        in_specs=[
            pl.BlockSpec((bm, k), lambda i, j: (i, 0)),
            pl.BlockSpec((k, bn), lambda i, j: (0, j)),
        ],
        out_specs=pl.BlockSpec((bm, bn), lambda i, j: (i, j)),
        out_shape=jax.ShapeDtypeStruct((n, nout), out_dtype),
        compiler_params=_params(("parallel", "arbitrary")),
        name=name,
    )(x, w)


def _compress_kernel(x_ref, pe_ref, w1_ref, b1_ref, w2_ref, o_ref, *, nh, half):
    x = x_ref[0, 0].astype(F32)
    xa = (x + pe_ref[0:1, :]).astype(BF16)
    xb = (x + pe_ref[1:2, :]).astype(BF16)
    ha = _dot(xa, w1_ref[0:half, :])
    hb = _dot(xb, w1_ref[half:2 * half, :])
    hid = ha + pltpu.roll(hb, nh - 1, 0) + b1_ref[...]
    c = math.sqrt(2.0 / math.pi)
    act = 0.5 * hid * (1.0 + jnp.tanh(c * (hid + 0.044715 * (hid * hid * hid))))
    o_ref[0, 0] = _dot(act.astype(BF16), w2_ref[...]).astype(o_ref.dtype)


def _compress(x, pe2, w1, b1, w2):
    bsz, g, nh, half = x.shape
    hid = w1.shape[1]
    hd = w2.shape[1]
    return pl.pallas_call(
        functools.partial(_compress_kernel, nh=nh, half=half),
        grid=(bsz, g),
        in_specs=[
            pl.BlockSpec((1, 1, nh, half), lambda b, gg: (b, gg, 0, 0)),
            pl.BlockSpec((2, half), lambda b, gg: (0, 0)),
            pl.BlockSpec((2 * half, hid), lambda b, gg: (0, 0)),
            pl.BlockSpec((1, hid), lambda b, gg: (0, 0)),
            pl.BlockSpec((hid, hd), lambda b, gg: (0, 0)),
        ],
        out_specs=pl.BlockSpec((1, 1, nh, hd), lambda b, gg: (b, gg, 0, 0)),
        out_shape=jax.ShapeDtypeStruct((bsz, g, nh, hd), BF16),
        compiler_params=_params(("parallel", "parallel")),
        name="compress",
    )(x, pe2, w1, b1, w2)


def _stack_heads(qt_ref):
    return jnp.concatenate([qt_ref[0, r] for r in range(NSA_REP)], axis=1)


def _per_head(x):
    return jnp.tile(x, (1, NSA_REP))


def _cmp_topk_kernel(qt_ref, kc_ref, vct_ref, ovt_ref, oct_ref, selt_ref, *, tq, nh, nb, n_sel):
    t0 = pl.program_id(2) * tq
    cend = lax.broadcasted_iota(jnp.int32, (nh, tq), 0) * CMP_STRIDE + (CMP_BLOCK - 1)
    tpos = t0 + lax.broadcasted_iota(jnp.int32, (nh, tq), 1)
    bias = jnp.where(cend <= tpos, 0.0, NEG)
    seen = jnp.where(t0 + lax.broadcasted_iota(jnp.int32, (1, tq), 1) >= CMP_BLOCK - 1, 1.0, 0.0)
    qt = _stack_heads(qt_ref)
    s = _dot(kc_ref[0, 0], qt) + _per_head(bias)
    e = jnp.exp2(s - jnp.max(s, axis=0, keepdims=True))
    ea = _dot(vct_ref[0, 0], e.astype(BF16))
    inv = _per_head(seen) / jnp.maximum(ea[HEAD_DIM:HEAD_DIM + 1], TINY)
    oct_ref[0, 0] = ea[0:HEAD_DIM] * inv
    p = e * inv
    psum = p[:, 0:tq]
    for r in range(1, NSA_REP):
        psum = psum + p[:, r * tq:(r + 1) * tq]
    imp = _dot(ovt_ref[...], psum.astype(BF16))
    blk = lax.broadcasted_iota(jnp.int32, (nb, tq), 0).astype(F32)
    cur = ((t0 + lax.broadcasted_iota(jnp.int32, (nb, tq), 1)) // SLC_BLOCK).astype(F32)
    forced = (blk == 0.0) | (blk == cur) | (blk == cur - 1.0)
    x = jnp.where(forced, -jnp.inf, jnp.where(blk <= cur, imp, NEG))
    selb = jnp.where(forced, 0.0, NEG)
    for _ in range(n_sel - N_FORCED):
        mx = jnp.max(x, axis=0, keepdims=True)
        idx = jnp.min(jnp.where(x == mx, blk, float(nb)), axis=0, keepdims=True)
        hit = blk == idx
        selb = jnp.where(hit, 0.0, selb)
        x = jnp.where(hit, -jnp.inf, x)
    selt_ref[0, 0] = selb.astype(selt_ref.dtype)


def _cmp_topk(qt, kc, vct, ovt):
    bsz, _, hd, s = qt.shape
    g, nh = kc.shape[1], kc.shape[2]
    nb = ovt.shape[0]
    assert nb >= SLC_TOPN
    tq = min(Q_TILE, s)
    cols = NSA_REP * tq
    return pl.pallas_call(
        functools.partial(_cmp_topk_kernel, tq=tq, nh=nh, nb=nb, n_sel=SLC_TOPN),
        grid=(bsz, g, s // tq),
        in_specs=[
            pl.BlockSpec((1, NSA_REP, hd, tq), lambda b, gg, i: (b, gg, 0, i)),
            pl.BlockSpec((1, 1, nh, hd), lambda b, gg, i: (b, gg, 0, 0)),
            pl.BlockSpec((1, 1, hd + ONES_ROWS, nh), lambda b, gg, i: (b, gg, 0, 0)),
            pl.BlockSpec((nb, nh), lambda b, gg, i: (0, 0)),
        ],
        out_specs=[
            pl.BlockSpec((1, 1, hd, cols), lambda b, gg, i: (b, gg, 0, i)),
            pl.BlockSpec((1, 1, nb, tq), lambda b, gg, i: (b, gg, 0, i)),
        ],
        out_shape=[
            jax.ShapeDtypeStruct((bsz, g, hd, NSA_REP * s), F32),
            jax.ShapeDtypeStruct((bsz, g, nb, s), BF16),
        ],
        compiler_params=_params(("parallel", "parallel", "arbitrary")),
        name="cmp_topk",
    )(qt, kc, vct, ovt)


def _sel_win_kernel(qt_ref, ket_ref, vst_ref, kw_ref, vwt_ref, selt_ref, oct_ref, gatet_ref,
                    o_ref, rhs_sc, m_sc, acc_sc, sa_sc, sb_sc, ma_sc, mb_sc,
                    *, tq, tk, chunk, nvar, nchunk, wlen):
    t0 = pl.program_id(2) * tq
    qt = _stack_heads(qt_ref)
    for c in range(nchunk):
        selt = selt_ref[0, 0, c * chunk:(c + 1) * chunk, :]
        rhs_sc[c] = jnp.concatenate([_per_head(selt), qt], axis=0)

    m_sc[...] = jnp.full_like(m_sc, NEG)
    acc_sc[...] = jnp.zeros_like(acc_sc)

    def scores(kt, causal):
        k0 = pl.multiple_of(kt * tk, tk)
        s = _dot(ket_ref[0, 0, pl.ds(k0, tk), :], rhs_sc[kt // nvar])
        if causal:
            kpos = k0 + lax.broadcasted_iota(jnp.int32, (tk, tq), 0)
            tcol = t0 + lax.broadcasted_iota(jnp.int32, (tk, tq), 1)
            s = s + _per_head(jnp.where(kpos <= tcol, 0.0, NEG))
        return s

    def put_scores(slot, kt, causal):
        s_ref, mx_ref = slot
        s = scores(kt, causal)
        s_ref[...] = s
        mx_ref[...] = jnp.max(s, axis=0, keepdims=True)

    def accumulate(slot, kt):
        s_ref, mx_ref = slot
        k0 = pl.multiple_of(kt * tk, tk)
        m_old = m_sc[...]
        m_new = jnp.maximum(m_old, mx_ref[...])
        alpha = jnp.exp2(m_old - m_new)
        p = jnp.exp2(s_ref[...] - m_new)
        acc_sc[...] = alpha * acc_sc[...] + _dot(vst_ref[0, 0, :, pl.ds(k0, tk)], p.astype(BF16))
        m_sc[...] = m_new

    def stage(dst, kt_new, causal, src, kt_old):
        put_scores(dst, kt_new, causal)
        accumulate(src, kt_old)

    slot_a, slot_b = (sa_sc, ma_sc), (sb_sc, mb_sc)
    nfull = (t0 + tq - 1) // tk

    @pl.when(nfull == 0)
    def _():
        put_scores(slot_a, 0, True)
        accumulate(slot_a, 0)

    @pl.when(nfull > 0)
    def _():
        put_scores(slot_a, 0, False)

        def body(j, carry):
            stage(slot_b, 2 * j + 1, False, slot_a, 2 * j)
            stage(slot_a, 2 * j + 2, False, slot_b, 2 * j + 1)
            return carry

        lax.fori_loop(0, (nfull - 1) // 2, body, 0)

        @pl.when(nfull % 2 == 1)
        def _():
            stage(slot_b, nfull, True, slot_a, nfull - 1)
            accumulate(slot_b, nfull)

        @pl.when(nfull % 2 == 0)
        def _():
            stage(slot_b, nfull - 1, False, slot_a, nfull - 2)
            stage(slot_a, nfull, True, slot_b, nfull - 1)
            accumulate(slot_a, nfull)

    o_s = acc_sc[0:HEAD_DIM] * (1.0 / jnp.maximum(acc_sc[HEAD_DIM:HEAD_DIM + 1], TINY))

    wk = wlen + tq
    w0 = pl.multiple_of(jnp.maximum(t0 - wlen, 0), tq)
    diff = (t0 + lax.broadcasted_iota(jnp.int32, (wk, tq), 1)) - (
        w0 + lax.broadcasted_iota(jnp.int32, (wk, tq), 0))
    wbias = jnp.where(jnp.logical_and(diff >= 0, diff < wlen), 0.0, NEG)
    sw = _dot(kw_ref[0, 0, pl.ds(w0, wk), :], qt) + _per_head(wbias)
    ew = jnp.exp2(sw - jnp.max(sw, axis=0, keepdims=True))
    ow = _dot(vwt_ref[0, 0, :, pl.ds(w0, wk)], ew.astype(BF16))
    o_w = ow[0:HEAD_DIM] * (1.0 / jnp.maximum(ow[HEAD_DIM:HEAD_DIM + 1], TINY))

    gate = gatet_ref[0, 0]
    o_c = oct_ref[0, 0]
    outs = []
    for r in range(NSA_REP):
        cs = slice(r * tq, (r + 1) * tq)
        outs.append(gate[3 * r:3 * r + 1] * o_c[:, cs] + gate[3 * r + 1:3 * r + 2] * o_s[:, cs]
                    + gate[3 * r + 2:3 * r + 3] * o_w[:, cs])
    pairs = [jnp.concatenate(outs[r:r + 2], axis=0).T for r in range(0, NSA_REP, 2)]
    o_ref[0] = jnp.concatenate(pairs, axis=1).astype(o_ref.dtype)


def _sel_win(qt, ket, vst, kw, vwt, selt, oct, gatet, tk):
    bsz, _, hd, s = qt.shape
    g = ket.shape[1]
    nb = selt.shape[2]
    chunk = ket.shape[3] - hd
    nchunk = nb // chunk
    nvar = chunk * SLC_BLOCK // tk
    tq = min(Q_TILE, s)
    cols = NSA_REP * tq
    resident = lambda shape: pl.BlockSpec(shape, lambda b, gg, i: (b, gg, 0, 0),
                                          pipeline_mode=pl.Buffered(1))
    return pl.pallas_call(
        functools.partial(_sel_win_kernel, tq=tq, tk=tk, chunk=chunk, nvar=nvar, nchunk=nchunk,
                          wlen=WINDOW),
        grid=(bsz, g, s // tq),
        in_specs=[
            pl.BlockSpec((1, NSA_REP, hd, tq), lambda b, gg, i: (b, gg, 0, i)),
            resident((1, 1, s, chunk + hd)),
            resident((1, 1, hd + ONES_ROWS, s)),
            resident((1, 1, s, hd)),
            resident((1, 1, hd + ONES_ROWS, s)),
            pl.BlockSpec((1, 1, nb, tq), lambda b, gg, i: (b, gg, 0, i)),
            pl.BlockSpec((1, 1, hd, cols), lambda b, gg, i: (b, gg, 0, i)),
            pl.BlockSpec((1, 1, GATE_ROWS, tq), lambda b, gg, i: (b, gg, 0, i)),
        ],
        out_specs=pl.BlockSpec((1, tq, NSA_REP * hd), lambda b, gg, i: (b, i, gg)),
        out_shape=jax.ShapeDtypeStruct((bsz, s, g * NSA_REP * hd), BF16),
        scratch_shapes=[
            pltpu.VMEM((nchunk, chunk + hd, cols), BF16),
            pltpu.VMEM((1, cols), F32),
            pltpu.VMEM((hd + ONES_ROWS, cols), F32),
            pltpu.VMEM((tk, cols), F32),
            pltpu.VMEM((tk, cols), F32),
            pltpu.VMEM((1, cols), F32),
            pltpu.VMEM((1, cols), F32),
        ],
        compiler_params=_params(("parallel", "parallel", "arbitrary")),
        name="sel_win",
    )(qt, ket, vst, kw, vwt, selt, oct, gatet)


def _retention_kernel(q_ref, k_ref, v_ref, gb_ref, cos_ref, sin_ref, dmat_ref, qdec_ref, kdec_ref,
                      cdec_ref, gng_ref, gnb_ref, o_ref, state_sc, *, nchunk, c):
    @pl.when(pl.program_id(2) == 0)
    def _():
        state_sc[...] = jnp.zeros_like(state_sc)

    dmat = dmat_ref[0]
    qdec = qdec_ref[0]
    kdec = kdec_ref[0]
    cdec = cdec_ref[0]
    kscale = RET_DIM ** -0.5
    half = RET_DIM // 2
    for n in range(nchunk):
        rows = pl.ds(n * c, c)
        cos = cos_ref[rows, :]
        sin = sin_ref[rows, :]
        qf = q_ref[0, rows, :].astype(F32)
        kf = k_ref[0, rows, :].astype(F32)
        qr = qf * cos + pltpu.roll(qf, half, 1) * sin
        kr = (kf * cos + pltpu.roll(kf, half, 1) * sin) * kscale
        v = v_ref[0, rows, :]
        inner = _dot_nt(qr.astype(BF16), kr.astype(BF16)) * dmat
        state = state_sc[...]
        y = _dot(inner.astype(BF16), v) + _dot((qr * qdec).astype(BF16), state.astype(BF16))
        state_sc[...] = state * cdec + _dot_tn((kr * kdec).astype(BF16), v)
        mu = jnp.mean(y, axis=-1, keepdims=True)
        d = y - mu
        var = jnp.mean(d * d, axis=-1, keepdims=True)
        yn = d * lax.rsqrt(var + LN_EPS) * gng_ref[...] + gnb_ref[...]
        gb = gb_ref[0, rows, :].astype(F32)
        o_ref[0, rows, :] = ((gb * _sigmoid(gb)) * yn).astype(o_ref.dtype)


def _retention(ret, cos2, sin2, dmat, qdec, kdec, cdec, gng, gnb):
    bsz, s, _ = ret.shape
    h = RET_HEADS
    c = RET_CHUNK
    tc = min(RET_TILE, s)
    spec = lambda off: pl.BlockSpec((1, tc, RET_DIM), lambda b, hh, j, off=off: (b, j, off + hh))
    hspec = lambda shp: pl.BlockSpec((1,) + shp, lambda b, hh, j: (hh, 0, 0))
    return pl.pallas_call(
        functools.partial(_retention_kernel, nchunk=tc // c, c=c),
        grid=(bsz, h, s // tc),
        in_specs=[
            spec(0), spec(h), spec(2 * h), spec(3 * h),
            pl.BlockSpec((tc, RET_DIM), lambda b, hh, j: (j, 0)),
            pl.BlockSpec((tc, RET_DIM), lambda b, hh, j: (j, 0)),
            hspec((c, c)), hspec((c, RET_DIM)), hspec((c, RET_DIM)), hspec((1, RET_DIM)),
            pl.BlockSpec((1, RET_DIM), lambda b, hh, j: (0, hh)),
            pl.BlockSpec((1, RET_DIM), lambda b, hh, j: (0, hh)),
        ],
        out_specs=pl.BlockSpec((1, tc, RET_DIM), lambda b, hh, j: (b, j, hh)),
        out_shape=jax.ShapeDtypeStruct((bsz, s, h * RET_DIM), BF16),
        scratch_shapes=[pltpu.VMEM((RET_DIM, RET_DIM), F32)],
        compiler_params=_params(("parallel", "parallel", "arbitrary")),
        name="retention",
    )(ret, ret, ret, ret, cos2, sin2, dmat, qdec, kdec, cdec, gng, gnb)


def _merge_kernel(h_ref, oa_ref, ob_ref, wga_ref, wgb_ref, wpa_ref, wpb_ref, o_ref):
    hb = h_ref[...]
    ga = _sigmoid(_dot(hb, wga_ref[...]))
    gb = _sigmoid(_dot(hb, wgb_ref[...]))
    merged = ga * _dot(oa_ref[...], wpa_ref[...]) + gb * _dot(ob_ref[...], wpb_ref[...])
    o_ref[...] = merged.astype(o_ref.dtype)


def _merge(hb, oa, ob, wga, wgb, wpa, wpb):
    n, d = hb.shape
    bm = min(ROW_TILE, n)
    bn = min(COL_TILE, d)
    ka, kb = oa.shape[1], ob.shape[1]
    return pl.pallas_call(
        _merge_kernel,
        grid=(n // bm, d // bn),
        in_specs=[
            pl.BlockSpec((bm, d), lambda i, j: (i, 0)),
            pl.BlockSpec((bm, ka), lambda i, j: (i, 0)),
            pl.BlockSpec((bm, kb), lambda i, j: (i, 0)),
            pl.BlockSpec((d, bn), lambda i, j: (0, j)),
            pl.BlockSpec((d, bn), lambda i, j: (0, j)),
            pl.BlockSpec((ka, bn), lambda i, j: (0, j)),
            pl.BlockSpec((kb, bn), lambda i, j: (0, j)),
        ],
        out_specs=pl.BlockSpec((bm, bn), lambda i, j: (i, j)),
        out_shape=jax.ShapeDtypeStruct((n, d), BF16),
        compiler_params=_params(("parallel", "arbitrary")),
        name="merge",
    )(hb, oa, ob, wga, wgb, wpa, wpb)


def _proj_ln_kernel(x_ref, m_ref, w_ref, g_ref, b_ref, o_ref, *, alpha):
    y = alpha * x_ref[...] + _dot(m_ref[...], w_ref[...])
    o_ref[...] = _layer_norm_rows(y, g_ref[...], b_ref[...])


def _proj_ln(x, m, w, g, b, alpha):
    n, d = x.shape
    bm = min(ROW_TILE, n)
    return pl.pallas_call(
        functools.partial(_proj_ln_kernel, alpha=alpha),
        grid=(n // bm,),
        in_specs=[
            pl.BlockSpec((bm, d), lambda i: (i, 0)),
            pl.BlockSpec((bm, d), lambda i: (i, 0)),
            pl.BlockSpec((d, d), lambda i: (0, 0)),
            pl.BlockSpec((1, d), lambda i: (0, 0)),
            pl.BlockSpec((1, d), lambda i: (0, 0)),
        ],
        out_specs=pl.BlockSpec((bm, d), lambda i: (i, 0)),
        out_shape=jax.ShapeDtypeStruct((n, d), F32),
        compiler_params=_params(("parallel",)),
        name="proj_ln",
    )(x, m, w, g, b)


def _overlap_matrix(nh, nb):
    c0 = np.arange(nh) * CMP_STRIDE
    c1 = c0 + CMP_BLOCK
    s0 = np.arange(nb) * SLC_BLOCK
    s1 = s0 + SLC_BLOCK
    ov = (c0[:, None] < s1[None, :]) & (c1[:, None] > s0[None, :])
    ov[nh - 1, :] = False
    return jnp.asarray(ov.T, BF16)


def _expand_matrix(nb, s):
    chunk = min(LANE, nb)
    blk_in_chunk = (np.arange(s) // SLC_BLOCK) % chunk
    return jnp.asarray(blk_in_chunk[:, None] == np.arange(chunk)[None, :], BF16)


def _retention_tables(s):
    h, c, d = RET_HEADS, RET_CHUNK, RET_DIM
    inv = ROPE_BASE ** (-jnp.arange(0, d, 2, dtype=F32) / d)
    ang = jnp.arange(s)[:, None].astype(F32) * inv[None, :]
    cos, sin = jnp.cos(ang), jnp.sin(ang)
    cos2 = jnp.concatenate([cos, cos], -1)
    sin2 = jnp.concatenate([-sin, sin], -1)
    log_g = jnp.log1p(-jnp.exp2(-5.0 - jnp.arange(h, dtype=F32)))
    i = jnp.arange(c, dtype=F32)
    diff = i[:, None] - i[None, :]
    dmat = jnp.where(diff >= 0, jnp.exp(jnp.maximum(diff, 0.0)[None] * log_g[:, None, None]), 0.0)
    kdec = jnp.exp((c - 1 - i)[None, :] * log_g[:, None])
    qdec = jnp.exp((i + 1)[None, :] * log_g[:, None])
    cdec = jnp.exp(c * log_g)
    bc = lambda t: jnp.broadcast_to(t[:, :, None], (h, c, d))
    return cos2, sin2, dmat, bc(qdec), bc(kdec), jnp.broadcast_to(cdec[:, None, None], (h, 1, d))


def _pad_cols(w, mult):
    pad = (-w.shape[1]) % mult
    return jnp.pad(w, ((0, 0), (0, pad))) if pad else w


def _mixer(hf, hb, bsz, s, w_in, cmp_k, cmp_v, ret_gn_g, ret_gn_b, w_merge_gate, w_proj_a, w_proj_b,
           w_o, ln_g, ln_b, alpha, tables):
    n, d = hf.shape
    g, hd = NSA_GROUPS, HEAD_DIM
    ovt, expand, ret_tabs = tables
    o_nsa = NSA_Q + 6 * NSA_KV
    w_q = w_in[:, :NSA_Q].astype(BF16)
    w_kv = w_in[:, NSA_Q:o_nsa].astype(BF16)
    w_gate = _pad_cols(w_in[:, o_nsa:o_nsa + NSA_GATE], LANE).astype(BF16)
    w_ret = w_in[:, o_nsa + NSA_GATE:].astype(BF16)

    q = _matmul(hb, w_q, None, BF16, "proj_q", scale=QK_SCALE)
    kv = _matmul(hb, w_kv, None, BF16, "proj_kv")
    ga = _matmul(hb, w_gate, "sigmoid", F32, "proj_gate")
    ret = _matmul(hb, w_ret, None, BF16, "proj_ret")

    qt = jnp.transpose(q.reshape(bsz, s, NSA_HEADS, hd), (0, 2, 3, 1))
    kv = kv.reshape(bsz, s, 6, g, hd)
    to_rows = lambda t: jnp.transpose(t, (0, 2, 1, 3))
    ones_rows = jnp.zeros((bsz, g, ONES_ROWS, 1), BF16).at[:, :, 0].set(1.0)

    def to_cols_ones(t):
        vt = jnp.transpose(t, (0, 2, 3, 1))
        return jnp.concatenate([vt, jnp.broadcast_to(ones_rows, vt.shape[:2] + (ONES_ROWS, vt.shape[3]))],
                               axis=2)

    nh = s // CMP_STRIDE
    half = CMP_STRIDE * hd

    def compress(t, prm):
        pe, w1, b1, w2 = prm
        x = to_rows(t).reshape(bsz, g, nh, half)
        return _compress(x, pe.reshape(2, half), w1.astype(BF16), b1.reshape(1, -1), w2.astype(BF16))

    kc = compress(kv[:, :, 0], cmp_k)
    vct = to_cols_ones(jnp.transpose(compress(kv[:, :, 1], cmp_v), (0, 2, 1, 3)))
    oct, selt = _cmp_topk(qt, kc, vct, ovt)

    gatet = jnp.transpose(ga[:, :NSA_GATE].reshape(bsz, s, g, 3 * NSA_REP), (0, 2, 3, 1))
    gatet = jnp.pad(gatet, ((0, 0), (0, 0), (0, GATE_ROWS - 3 * NSA_REP), (0, 0)))
    ket = jnp.concatenate([jnp.broadcast_to(expand, (bsz, g) + expand.shape), to_rows(kv[:, :, 2])], axis=3)
    o_a = _sel_win(qt, ket, to_cols_ones(kv[:, :, 3]), to_rows(kv[:, :, 4]), to_cols_ones(kv[:, :, 5]),
                   selt, oct, gatet, min(K_TILE, s))

    o_b = _retention(ret.reshape(bsz, s, 4 * RET_W), *ret_tabs,
                     ret_gn_g.reshape(1, -1), ret_gn_b.reshape(1, -1))

    merged = _merge(hb, o_a.reshape(n, NSA_Q), o_b.reshape(n, RET_W),
                    w_merge_gate[:, :d].astype(BF16), w_merge_gate[:, d:].astype(BF16),
                    w_proj_a.astype(BF16), w_proj_b.astype(BF16))
    return _proj_ln(hf, merged, w_o.astype(BF16), ln_g.reshape(1, -1), ln_b.reshape(1, -1), alpha)


def kernel(x, ffn1_w_gate, ffn1_w_up, ffn1_w_down, ln1_g, ln1_b, w_in, cmp_k_pe, cmp_k_w1, cmp_k_b1,
           cmp_k_w2, cmp_v_pe, cmp_v_w1, cmp_v_b1, cmp_v_w2, ret_gn_g, ret_gn_b, w_merge_gate, w_proj_a,
           w_proj_b, w_o, ln2_g, ln2_b, ffn2_w_gate, ffn2_w_up, ffn2_w_down, ln3_g, ln3_b):
    bsz, s, d = x.shape
    depth = ffn1_w_gate.shape[0]
    f = ffn1_w_gate.shape[2]
    alpha = (2 * depth) ** 0.25
    fpad = (-f) % min(FF_TILE, f)
    nb = s // SLC_BLOCK
    tables = (_overlap_matrix(s // CMP_STRIDE, nb), _expand_matrix(nb, s), _retention_tables(s))

    def ffn_weights(wg, wu, wd):
        return (jnp.pad(wg, ((0, 0), (0, fpad))).astype(BF16), jnp.pad(wu, ((0, 0), (0, fpad))).astype(BF16),
                jnp.pad(wd, ((0, fpad), (0, 0))).astype(BF16))

    row = lambda t: t.reshape(1, -1)
    xf = x.reshape(bsz * s, d)
    for l in range(depth):
        hf, hb = _ffn_ln(xf, *ffn_weights(ffn1_w_gate[l], ffn1_w_up[l], ffn1_w_down[l]),
                         row(ln1_g[l]), row(ln1_b[l]), alpha, True)
        xf = _mixer(hf, hb, bsz, s, w_in[l],
                    (cmp_k_pe[l], cmp_k_w1[l], cmp_k_b1[l], cmp_k_w2[l]),
                    (cmp_v_pe[l], cmp_v_w1[l], cmp_v_b1[l], cmp_v_w2[l]),
                    ret_gn_g[l], ret_gn_b[l], w_merge_gate[l], w_proj_a[l], w_proj_b[l], w_o[l],
                    ln2_g[l], ln2_b[l], alpha, tables)
        xf, _ = _ffn_ln(xf, *ffn_weights(ffn2_w_gate[l], ffn2_w_up[l], ffn2_w_down[l]),
                        row(ln3_g[l]), row(ln3_b[l]), alpha, False)
    return xf.reshape(bsz, s, d)
```

```python
import functools
import math

import jax
import jax.numpy as jnp
import numpy as np
from jax import lax
from jax.experimental import pallas as pl
from jax.experimental.pallas import tpu as pltpu

NSA_HEADS = 16
NSA_GROUPS = 4
NSA_REP = NSA_HEADS // NSA_GROUPS
HEAD_DIM = 64
CMP_BLOCK = 32
CMP_STRIDE = 16
SLC_BLOCK = 64
SLC_TOPN = 16
N_FORCED = 3
CAUSAL_PARTS = 4
WINDOW = 512
RET_HEADS = 8
RET_DIM = 128
RET_CHUNK = 128
ROPE_BASE = 10000.0
LN_EPS = 1e-5
NEG = -1e30
TINY = 1e-30
NSA_Q = NSA_HEADS * HEAD_DIM
NSA_KV = NSA_GROUPS * HEAD_DIM
NSA_GATE = 3 * NSA_HEADS
GATE_ROWS = 16
ONES_ROWS = 16
QK_SCALE = HEAD_DIM ** -0.5 * math.log2(math.e)
RET_W = RET_HEADS * RET_DIM

LANE = 128
SUBLANE = 8
VMEM_LIMIT = 52 * 1024 * 1024
ROW_TILE = 512
FF_TILE = 512
COL_TILE = 512
PROJ_ROW_TILE = 1024
PROJ_COL_TILE = 1024
Q_TILE = 256
K_TILE = 512
RET_TILE = 1024

BF16 = jnp.bfloat16
F32 = jnp.float32


def _dot(a, b):
    return jnp.dot(a, b, preferred_element_type=F32)


def _dot_nt(a, b):
    return lax.dot_general(a, b, (((1,), (1,)), ((), ())), preferred_element_type=F32)


def _dot_tn(a, b):
    return lax.dot_general(a, b, (((0,), (0,)), ((), ())), preferred_element_type=F32)


def _sigmoid(x):
    return 1.0 / (1.0 + jnp.exp(-x))


def _layer_norm_rows(y, g, b):
    mu = jnp.mean(y, axis=-1, keepdims=True)
    d = y - mu
    var = jnp.mean(d * d, axis=-1, keepdims=True)
    return d * lax.rsqrt(var + LN_EPS) * g + b


def _params(sem):
    return pltpu.CompilerParams(dimension_semantics=sem, vmem_limit_bytes=VMEM_LIMIT)


def _ffn_ln_kernel(x_ref, wg_ref, wu_ref, wd_ref, g_ref, b_ref, *rest, alpha, nf, with_bf16):
    if with_bf16:
        o_ref, ob_ref, xb_sc, acc_sc = rest
    else:
        o_ref, xb_sc, acc_sc = rest
        ob_ref = None
    f = pl.program_id(1)

    @pl.when(f == 0)
    def _():
        xb_sc[...] = x_ref[...].astype(BF16)
        acc_sc[...] = jnp.zeros_like(acc_sc)

    xb = xb_sc[...]
    a = _dot(xb, wg_ref[...])
    u = _dot(xb, wu_ref[...])
    h = (a * _sigmoid(a)) * u
    acc_sc[...] += _dot(h.astype(BF16), wd_ref[...])

    @pl.when(f == nf - 1)
    def _():
        y = alpha * x_ref[...] + 0.5 * acc_sc[...]
        out = _layer_norm_rows(y, g_ref[...], b_ref[...])
        o_ref[...] = out
        if with_bf16:
            ob_ref[...] = out.astype(BF16)


def _ffn_ln(x, wg, wu, wd, g, b, alpha, with_bf16):
    n, d = x.shape
    fp = wg.shape[1]
    bm = min(ROW_TILE, n)
    bf = min(FF_TILE, fp)
    nf = fp // bf
    out_shape = [jax.ShapeDtypeStruct((n, d), F32)]
    out_specs = [pl.BlockSpec((bm, d), lambda i, f: (i, 0))]
    if with_bf16:
        out_shape.append(jax.ShapeDtypeStruct((n, d), BF16))
        out_specs.append(pl.BlockSpec((bm, d), lambda i, f: (i, 0)))
    res = pl.pallas_call(
        functools.partial(_ffn_ln_kernel, alpha=alpha, nf=nf, with_bf16=with_bf16),
        grid=(n // bm, nf),
        in_specs=[
            pl.BlockSpec((bm, d), lambda i, f: (i, 0)),
            pl.BlockSpec((d, bf), lambda i, f: (0, f)),
            pl.BlockSpec((d, bf), lambda i, f: (0, f)),
            pl.BlockSpec((bf, d), lambda i, f: (f, 0)),
            pl.BlockSpec((1, d), lambda i, f: (0, 0)),
            pl.BlockSpec((1, d), lambda i, f: (0, 0)),
        ],
        out_specs=out_specs,
        out_shape=out_shape,
        scratch_shapes=[pltpu.VMEM((bm, d), BF16), pltpu.VMEM((bm, d), F32)],
        compiler_params=_params(("parallel", "arbitrary")),
        name="ffn_ln",
    )(x, wg, wu, wd, g, b)
    return res if with_bf16 else (res[0], None)


def _mm_kernel(x_ref, w_ref, o_ref, *, act, scale):
    y = _dot(x_ref[...], w_ref[...])
    if scale is not None:
        y = y * scale
    if act == "sigmoid":
        y = _sigmoid(y)
    o_ref[...] = y.astype(o_ref.dtype)


def _matmul(x, w, act, out_dtype, name, scale=None):
    n, k = x.shape
    nout = w.shape[1]
    bm = min(PROJ_ROW_TILE, n)
    bn = next(c for c in (PROJ_COL_TILE, COL_TILE, 2 * LANE, LANE, nout) if nout % c == 0)
    assert n % bm == 0
    return pl.pallas_call(
        functools.partial(_mm_kernel, act=act, scale=scale),
        grid=(n // bm, nout // bn),
        in_specs=[
            pl.BlockSpec((bm, k), lambda i, j: (i, 0)),
            pl.BlockSpec((k, bn), lambda i, j: (0, j)),
        ],
        out_specs=pl.BlockSpec((bm, bn), lambda i, j: (i, j)),
        out_shape=jax.ShapeDtypeStruct((n, nout), out_dtype),
        compiler_params=_params(("parallel", "arbitrary")),
        name=name,
    )(x, w)


def _compress_kernel(x_ref, pe_ref, w1_ref, b1_ref, w2_ref, o_ref, *, nh, half):
    x = x_ref[0, 0].astype(F32)
    xa = (x + pe_ref[0:1, :]).astype(BF16)
    xb = (x + pe_ref[1:2, :]).astype(BF16)
    ha = _dot(xa, w1_ref[0:half, :])
    hb = _dot(xb, w1_ref[half:2 * half, :])
    hid = ha + pltpu.roll(hb, nh - 1, 0) + b1_ref[...]
    c = math.sqrt(2.0 / math.pi)
    act = 0.5 * hid * (1.0 + jnp.tanh(c * (hid + 0.044715 * (hid * hid * hid))))
    o_ref[0, 0] = _dot(act.astype(BF16), w2_ref[...]).astype(o_ref.dtype)


def _compress(x, pe2, w1, b1, w2):
    bsz, g, nh, half = x.shape
    hid = w1.shape[1]
    hd = w2.shape[1]
    return pl.pallas_call(
        functools.partial(_compress_kernel, nh=nh, half=half),
        grid=(bsz, g),
        in_specs=[
            pl.BlockSpec((1, 1, nh, half), lambda b, gg: (b, gg, 0, 0)),
            pl.BlockSpec((2, half), lambda b, gg: (0, 0)),
            pl.BlockSpec((2 * half, hid), lambda b, gg: (0, 0)),
            pl.BlockSpec((1, hid), lambda b, gg: (0, 0)),
            pl.BlockSpec((hid, hd), lambda b, gg: (0, 0)),
        ],
        out_specs=pl.BlockSpec((1, 1, nh, hd), lambda b, gg: (b, gg, 0, 0)),
        out_shape=jax.ShapeDtypeStruct((bsz, g, nh, hd), BF16),
        compiler_params=_params(("parallel", "parallel")),
        name="compress",
    )(x, pe2, w1, b1, w2)


def _stack_heads(qt_ref):
    return jnp.concatenate([qt_ref[0, r] for r in range(NSA_REP)], axis=1)


def _per_head(x):
    return jnp.tile(x, (1, NSA_REP))


def _cmp_topk_kernel(qt_ref, kc_ref, vct_ref, ovt_ref, oct_ref, selt_ref, *, tq, nh, nb, n_sel, nq):
    i = pl.program_id(2)
    for part in range(CAUSAL_PARTS):
        lo, hi = part * nq // CAUSAL_PARTS, (part + 1) * nq // CAUSAL_PARTS

        @pl.when(jnp.logical_and(i >= lo, i < hi))
        def _(part=part):
            _cmp_topk_part(qt_ref, kc_ref, vct_ref, ovt_ref, oct_ref, selt_ref, i * tq, tq=tq,
                           nh=(part + 1) * nh // CAUSAL_PARTS, nb=(part + 1) * nb // CAUSAL_PARTS,
                           nb_all=nb, n_sel=n_sel)


def _cmp_topk_part(qt_ref, kc_ref, vct_ref, ovt_ref, oct_ref, selt_ref, t0, *, tq, nh, nb, nb_all, n_sel):
    cend = lax.broadcasted_iota(jnp.int32, (nh, tq), 0) * CMP_STRIDE + (CMP_BLOCK - 1)
    tpos = t0 + lax.broadcasted_iota(jnp.int32, (nh, tq), 1)
    bias = jnp.where(cend <= tpos, 0.0, NEG)
    seen = jnp.where(t0 + lax.broadcasted_iota(jnp.int32, (1, tq), 1) >= CMP_BLOCK - 1, 1.0, 0.0)
    qt = _stack_heads(qt_ref)
    s = _dot(kc_ref[0, 0, 0:nh, :], qt) + _per_head(bias)
    e = jnp.exp2(s - jnp.max(s, axis=0, keepdims=True))
    ea = _dot(vct_ref[0, 0, :, 0:nh], e.astype(BF16))
    inv = _per_head(seen) / jnp.maximum(ea[HEAD_DIM:HEAD_DIM + 1], TINY)
    oct_ref[0, 0] = ea[0:HEAD_DIM] * inv
    p = e * inv
    psum = p[:, 0:tq]
    for r in range(1, NSA_REP):
        psum = psum + p[:, r * tq:(r + 1) * tq]
    imp = _dot(ovt_ref[0:nb, 0:nh], psum.astype(BF16))
    blk = lax.broadcasted_iota(jnp.int32, (nb, tq), 0).astype(F32)
    cur = ((t0 + lax.broadcasted_iota(jnp.int32, (nb, tq), 1)) // SLC_BLOCK).astype(F32)
    forced = (blk == 0.0) | (blk == cur) | (blk == cur - 1.0)
    x = jnp.where(forced, -jnp.inf, jnp.where(blk <= cur, imp, NEG))
    selb = jnp.where(forced, 0.0, NEG)
    for _ in range(n_sel - N_FORCED):
        mx = jnp.max(x, axis=0, keepdims=True)
        idx = jnp.min(jnp.where(x == mx, blk, float(nb)), axis=0, keepdims=True)
        hit = blk == idx
        selb = jnp.where(hit, 0.0, selb)
        x = jnp.where(hit, -jnp.inf, x)
    selt_ref[0, 0, 0:nb, :] = selb.astype(selt_ref.dtype)
    if nb < nb_all:
        selt_ref[0, 0, nb:nb_all, :] = jnp.full((nb_all - nb, tq), NEG, selt_ref.dtype)


def _cmp_topk(qt, kc, vct, ovt):
    bsz, _, hd, s = qt.shape
    g, nh = kc.shape[1], kc.shape[2]
    nb = ovt.shape[0]
    assert nb >= SLC_TOPN and nh % CAUSAL_PARTS == 0 and nb % CAUSAL_PARTS == 0
    tq = min(Q_TILE, s)
    cols = NSA_REP * tq
    return pl.pallas_call(
        functools.partial(_cmp_topk_kernel, tq=tq, nh=nh, nb=nb, n_sel=SLC_TOPN, nq=s // tq),
        grid=(bsz, g, s // tq),
        in_specs=[
            pl.BlockSpec((1, NSA_REP, hd, tq), lambda b, gg, i: (b, gg, 0, i)),
            pl.BlockSpec((1, 1, nh, hd), lambda b, gg, i: (b, gg, 0, 0)),
            pl.BlockSpec((1, 1, hd + ONES_ROWS, nh), lambda b, gg, i: (b, gg, 0, 0)),
            pl.BlockSpec((nb, nh), lambda b, gg, i: (0, 0)),
        ],
        out_specs=[
            pl.BlockSpec((1, 1, hd, cols), lambda b, gg, i: (b, gg, 0, i)),
            pl.BlockSpec((1, 1, nb, tq), lambda b, gg, i: (b, gg, 0, i)),
        ],
        out_shape=[
            jax.ShapeDtypeStruct((bsz, g, hd, NSA_REP * s), F32),
            jax.ShapeDtypeStruct((bsz, g, nb, s), BF16),
        ],
        compiler_params=_params(("parallel", "parallel", "arbitrary")),
        name="cmp_topk",
    )(qt, kc, vct, ovt)


def _sel_win_kernel(qt_ref, ket_ref, vst_ref, kw_ref, vwt_ref, selt_ref, oct_ref, gatet_ref,
                    o_ref, rhs_sc, m_sc, acc_sc, sa_sc, sb_sc, ma_sc, mb_sc,
                    *, tq, tk, chunk, nvar, nchunk, wlen):
    t0 = pl.program_id(2) * tq
    qt = _stack_heads(qt_ref)
    for c in range(nchunk):
        selt = selt_ref[0, 0, c * chunk:(c + 1) * chunk, :]
        rhs_sc[c] = jnp.concatenate([_per_head(selt), qt], axis=0)

    m_sc[...] = jnp.full_like(m_sc, NEG)
    acc_sc[...] = jnp.zeros_like(acc_sc)

    def scores(kt, causal):
        k0 = pl.multiple_of(kt * tk, tk)
        s = _dot(ket_ref[0, 0, pl.ds(k0, tk), :], rhs_sc[kt // nvar])
        if causal:
            kpos = k0 + lax.broadcasted_iota(jnp.int32, (tk, tq), 0)
            tcol = t0 + lax.broadcasted_iota(jnp.int32, (tk, tq), 1)
            s = s + _per_head(jnp.where(kpos <= tcol, 0.0, NEG))
        return s

    def put_scores(slot, kt, causal):
        s_ref, mx_ref = slot
        s = scores(kt, causal)
        s_ref[...] = s
        mx_ref[...] = jnp.max(s, axis=0, keepdims=True)

    def accumulate(slot, kt):
        s_ref, mx_ref = slot
        k0 = pl.multiple_of(kt * tk, tk)
        m_old = m_sc[...]
        m_new = jnp.maximum(m_old, mx_ref[...])
        alpha = jnp.exp2(m_old - m_new)
        p = jnp.exp2(s_ref[...] - m_new)
        acc_sc[...] = alpha * acc_sc[...] + _dot(vst_ref[0, 0, :, pl.ds(k0, tk)], p.astype(BF16))
        m_sc[...] = m_new

    def stage(dst, kt_new, causal, src, kt_old):
        put_scores(dst, kt_new, causal)
        accumulate(src, kt_old)

    slot_a, slot_b = (sa_sc, ma_sc), (sb_sc, mb_sc)
    nfull = (t0 + tq - 1) // tk

    @pl.when(nfull == 0)
    def _():
        put_scores(slot_a, 0, True)
        accumulate(slot_a, 0)

    @pl.when(nfull > 0)
    def _():
        put_scores(slot_a, 0, False)

        def body(j, carry):
            stage(slot_b, 2 * j + 1, False, slot_a, 2 * j)
            stage(slot_a, 2 * j + 2, False, slot_b, 2 * j + 1)
            return carry

        lax.fori_loop(0, (nfull - 1) // 2, body, 0)

        @pl.when(nfull % 2 == 1)
        def _():
            stage(slot_b, nfull, True, slot_a, nfull - 1)
            accumulate(slot_b, nfull)

        @pl.when(nfull % 2 == 0)
        def _():
            stage(slot_b, nfull - 1, False, slot_a, nfull - 2)
            stage(slot_a, nfull, True, slot_b, nfull - 1)
            accumulate(slot_a, nfull)

    o_s = acc_sc[0:HEAD_DIM] * (1.0 / jnp.maximum(acc_sc[HEAD_DIM:HEAD_DIM + 1], TINY))

    wk = wlen + tq
    w0 = pl.multiple_of(jnp.maximum(t0 - wlen, 0), tq)
    diff = (t0 + lax.broadcasted_iota(jnp.int32, (wk, tq), 1)) - (
        w0 + lax.broadcasted_iota(jnp.int32, (wk, tq), 0))
    wbias = jnp.where(jnp.logical_and(diff >= 0, diff < wlen), 0.0, NEG)
    sw = _dot(kw_ref[0, 0, pl.ds(w0, wk), :], qt) + _per_head(wbias)
    ew = jnp.exp2(sw - jnp.max(sw, axis=0, keepdims=True))
    ow = _dot(vwt_ref[0, 0, :, pl.ds(w0, wk)], ew.astype(BF16))
    o_w = ow[0:HEAD_DIM] * (1.0 / jnp.maximum(ow[HEAD_DIM:HEAD_DIM + 1], TINY))

    gate = gatet_ref[0, 0]
    o_c = oct_ref[0, 0]
    outs = []
    for r in range(NSA_REP):
        cs = slice(r * tq, (r + 1) * tq)
        outs.append(gate[3 * r:3 * r + 1] * o_c[:, cs] + gate[3 * r + 1:3 * r + 2] * o_s[:, cs]
                    + gate[3 * r + 2:3 * r + 3] * o_w[:, cs])
    pairs = [jnp.concatenate(outs[r:r + 2], axis=0).T for r in range(0, NSA_REP, 2)]
    o_ref[0] = jnp.concatenate(pairs, axis=1).astype(o_ref.dtype)


def _sel_win(qt, ket, vst, kw, vwt, selt, oct, gatet, tk):
    bsz, _, hd, s = qt.shape
    g = ket.shape[1]
    nb = selt.shape[2]
    chunk = ket.shape[3] - hd
    nchunk = nb // chunk
    nvar = chunk * SLC_BLOCK // tk
    tq = min(Q_TILE, s)
    cols = NSA_REP * tq
    resident = lambda shape: pl.BlockSpec(shape, lambda b, gg, i: (b, gg, 0, 0),
                                          pipeline_mode=pl.Buffered(1))
    return pl.pallas_call(
        functools.partial(_sel_win_kernel, tq=tq, tk=tk, chunk=chunk, nvar=nvar, nchunk=nchunk,
                          wlen=WINDOW),
        grid=(bsz, g, s // tq),
        in_specs=[
            pl.BlockSpec((1, NSA_REP, hd, tq), lambda b, gg, i: (b, gg, 0, i)),
            resident((1, 1, s, chunk + hd)),
            resident((1, 1, hd + ONES_ROWS, s)),
            resident((1, 1, s, hd)),
            resident((1, 1, hd + ONES_ROWS, s)),
            pl.BlockSpec((1, 1, nb, tq), lambda b, gg, i: (b, gg, 0, i)),
            pl.BlockSpec((1, 1, hd, cols), lambda b, gg, i: (b, gg, 0, i)),
            pl.BlockSpec((1, 1, GATE_ROWS, tq), lambda b, gg, i: (b, gg, 0, i)),
        ],
        out_specs=pl.BlockSpec((1, tq, NSA_REP * hd), lambda b, gg, i: (b, i, gg)),
        out_shape=jax.ShapeDtypeStruct((bsz, s, g * NSA_REP * hd), BF16),
        scratch_shapes=[
            pltpu.VMEM((nchunk, chunk + hd, cols), BF16),
            pltpu.VMEM((1, cols), F32),
            pltpu.VMEM((hd + ONES_ROWS, cols), F32),
            pltpu.VMEM((tk, cols), F32),
            pltpu.VMEM((tk, cols), F32),
            pltpu.VMEM((1, cols), F32),
            pltpu.VMEM((1, cols), F32),
        ],
        compiler_params=_params(("parallel", "parallel", "arbitrary")),
        name="sel_win",
    )(qt, ket, vst, kw, vwt, selt, oct, gatet)


def _retention_kernel(q_ref, k_ref, v_ref, gb_ref, cos_ref, sin_ref, dmat_ref, qdec_ref, kdec_ref,
                      cdec_ref, gng_ref, gnb_ref, o_ref, state_sc, *, nchunk, c):
    @pl.when(pl.program_id(2) == 0)
    def _():
        state_sc[...] = jnp.zeros_like(state_sc)

    dmat = dmat_ref[0]
    qdec = qdec_ref[0]
    kdec = kdec_ref[0]
    cdec = cdec_ref[0]
    kscale = RET_DIM ** -0.5
    half = RET_DIM // 2
    for n in range(nchunk):
        rows = pl.ds(n * c, c)
        cos = cos_ref[rows, :]
        sin = sin_ref[rows, :]
        qf = q_ref[0, rows, :].astype(F32)
        kf = k_ref[0, rows, :].astype(F32)
        qr = qf * cos + pltpu.roll(qf, half, 1) * sin
        kr = (kf * cos + pltpu.roll(kf, half, 1) * sin) * kscale
        v = v_ref[0, rows, :]
        inner = _dot_nt(qr.astype(BF16), kr.astype(BF16)) * dmat
        state = state_sc[...]
        y = _dot(inner.astype(BF16), v) + _dot((qr * qdec).astype(BF16), state.astype(BF16))
        state_sc[...] = state * cdec + _dot_tn((kr * kdec).astype(BF16), v)
        mu = jnp.mean(y, axis=-1, keepdims=True)
        d = y - mu
        var = jnp.mean(d * d, axis=-1, keepdims=True)
        yn = d * lax.rsqrt(var + LN_EPS) * gng_ref[...] + gnb_ref[...]
        gb = gb_ref[0, rows, :].astype(F32)
        o_ref[0, rows, :] = ((gb * _sigmoid(gb)) * yn).astype(o_ref.dtype)


def _retention(ret, cos2, sin2, dmat, qdec, kdec, cdec, gng, gnb):
    bsz, s, _ = ret.shape
    h = RET_HEADS
    c = RET_CHUNK
    tc = min(RET_TILE, s)
    spec = lambda off: pl.BlockSpec((1, tc, RET_DIM), lambda b, hh, j, off=off: (b, j, off + hh))
    hspec = lambda shp: pl.BlockSpec((1,) + shp, lambda b, hh, j: (hh, 0, 0))
    return pl.pallas_call(
        functools.partial(_retention_kernel, nchunk=tc // c, c=c),
        grid=(bsz, h, s // tc),
        in_specs=[
            spec(0), spec(h), spec(2 * h), spec(3 * h),
            pl.BlockSpec((tc, RET_DIM), lambda b, hh, j: (j, 0)),
            pl.BlockSpec((tc, RET_DIM), lambda b, hh, j: (j, 0)),
            hspec((c, c)), hspec((c, RET_DIM)), hspec((c, RET_DIM)), hspec((1, RET_DIM)),
            pl.BlockSpec((1, RET_DIM), lambda b, hh, j: (0, hh)),
            pl.BlockSpec((1, RET_DIM), lambda b, hh, j: (0, hh)),
        ],
        out_specs=pl.BlockSpec((1, tc, RET_DIM), lambda b, hh, j: (b, j, hh)),
        out_shape=jax.ShapeDtypeStruct((bsz, s, h * RET_DIM), BF16),
        scratch_shapes=[pltpu.VMEM((RET_DIM, RET_DIM), F32)],
        compiler_params=_params(("parallel", "parallel", "arbitrary")),
        name="retention",
    )(ret, ret, ret, ret, cos2, sin2, dmat, qdec, kdec, cdec, gng, gnb)


def _merge_kernel(h_ref, oa_ref, ob_ref, wga_ref, wgb_ref, wpa_ref, wpb_ref, o_ref):
    hb = h_ref[...]
    ga = _sigmoid(_dot(hb, wga_ref[...]))
    gb = _sigmoid(_dot(hb, wgb_ref[...]))
    merged = ga * _dot(oa_ref[...], wpa_ref[...]) + gb * _dot(ob_ref[...], wpb_ref[...])
    o_ref[...] = merged.astype(o_ref.dtype)


def _merge(hb, oa, ob, wga, wgb, wpa, wpb):
    n, d = hb.shape
    bm = min(PROJ_ROW_TILE, n)
    bn = min(COL_TILE, d)
    ka, kb = oa.shape[1], ob.shape[1]
    return pl.pallas_call(
        _merge_kernel,
        grid=(n // bm, d // bn),
        in_specs=[
            pl.BlockSpec((bm, d), lambda i, j: (i, 0)),
            pl.BlockSpec((bm, ka), lambda i, j: (i, 0)),
            pl.BlockSpec((bm, kb), lambda i, j: (i, 0)),
            pl.BlockSpec((d, bn), lambda i, j: (0, j)),
            pl.BlockSpec((d, bn), lambda i, j: (0, j)),
            pl.BlockSpec((ka, bn), lambda i, j: (0, j)),
            pl.BlockSpec((kb, bn), lambda i, j: (0, j)),
        ],
        out_specs=pl.BlockSpec((bm, bn), lambda i, j: (i, j)),
        out_shape=jax.ShapeDtypeStruct((n, d), BF16),
        compiler_params=_params(("parallel", "arbitrary")),
        name="merge",
    )(hb, oa, ob, wga, wgb, wpa, wpb)


def _proj_ln_kernel(x_ref, m_ref, w_ref, g_ref, b_ref, o_ref, *, alpha):
    y = alpha * x_ref[...] + _dot(m_ref[...], w_ref[...])
    o_ref[...] = _layer_norm_rows(y, g_ref[...], b_ref[...])


def _proj_ln(x, m, w, g, b, alpha):
    n, d = x.shape
    bm = min(ROW_TILE, n)
    return pl.pallas_call(
        functools.partial(_proj_ln_kernel, alpha=alpha),
        grid=(n // bm,),
        in_specs=[
            pl.BlockSpec((bm, d), lambda i: (i, 0)),
            pl.BlockSpec((bm, d), lambda i: (i, 0)),
            pl.BlockSpec((d, d), lambda i: (0, 0)),
            pl.BlockSpec((1, d), lambda i: (0, 0)),
            pl.BlockSpec((1, d), lambda i: (0, 0)),
        ],
        out_specs=pl.BlockSpec((bm, d), lambda i: (i, 0)),
        out_shape=jax.ShapeDtypeStruct((n, d), F32),
        compiler_params=_params(("parallel",)),
        name="proj_ln",
    )(x, m, w, g, b)


def _overlap_matrix(nh, nb):
    c0 = np.arange(nh) * CMP_STRIDE
    c1 = c0 + CMP_BLOCK
    s0 = np.arange(nb) * SLC_BLOCK
    s1 = s0 + SLC_BLOCK
    ov = (c0[:, None] < s1[None, :]) & (c1[:, None] > s0[None, :])
    ov[nh - 1, :] = False
    return jnp.asarray(ov.T, BF16)


def _expand_matrix(nb, s):
    chunk = min(LANE, nb)
    blk_in_chunk = (np.arange(s) // SLC_BLOCK) % chunk
    return jnp.asarray(blk_in_chunk[:, None] == np.arange(chunk)[None, :], BF16)


def _retention_tables(s):
    h, c, d = RET_HEADS, RET_CHUNK, RET_DIM
    inv = ROPE_BASE ** (-jnp.arange(0, d, 2, dtype=F32) / d)
    ang = jnp.arange(s)[:, None].astype(F32) * inv[None, :]
    cos, sin = jnp.cos(ang), jnp.sin(ang)
    cos2 = jnp.concatenate([cos, cos], -1)
    sin2 = jnp.concatenate([-sin, sin], -1)
    log_g = jnp.log1p(-jnp.exp2(-5.0 - jnp.arange(h, dtype=F32)))
    i = jnp.arange(c, dtype=F32)
    diff = i[:, None] - i[None, :]
    dmat = jnp.where(diff >= 0, jnp.exp(jnp.maximum(diff, 0.0)[None] * log_g[:, None, None]), 0.0)
    kdec = jnp.exp((c - 1 - i)[None, :] * log_g[:, None])
    qdec = jnp.exp((i + 1)[None, :] * log_g[:, None])
    cdec = jnp.exp(c * log_g)
    bc = lambda t: jnp.broadcast_to(t[:, :, None], (h, c, d))
    return cos2, sin2, dmat, bc(qdec), bc(kdec), jnp.broadcast_to(cdec[:, None, None], (h, 1, d))


def _pad_cols(w, mult):
    pad = (-w.shape[1]) % mult
    return jnp.pad(w, ((0, 0), (0, pad))) if pad else w


def _mixer(hf, hb, bsz, s, w_in, cmp_k, cmp_v, ret_gn_g, ret_gn_b, w_merge_gate, w_proj_a, w_proj_b,
           w_o, ln_g, ln_b, alpha, tables):
    n, d = hf.shape
    g, hd = NSA_GROUPS, HEAD_DIM
    ovt, expand, ret_tabs = tables
    o_nsa = NSA_Q + 6 * NSA_KV
    w_q = w_in[:, :NSA_Q].astype(BF16)
    w_kv = w_in[:, NSA_Q:o_nsa].astype(BF16)
    w_gate = _pad_cols(w_in[:, o_nsa:o_nsa + NSA_GATE], LANE).astype(BF16)
    w_ret = w_in[:, o_nsa + NSA_GATE:].astype(BF16)

    q = _matmul(hb, w_q, None, BF16, "proj_q", scale=QK_SCALE)
    kv = _matmul(hb, w_kv, None, BF16, "proj_kv")
    ga = _matmul(hb, w_gate, "sigmoid", F32, "proj_gate")
    ret = _matmul(hb, w_ret, None, BF16, "proj_ret")

    qt = jnp.transpose(q.reshape(bsz, s, NSA_HEADS, hd), (0, 2, 3, 1))
    kv = kv.reshape(bsz, s, 6, g, hd)
    to_rows = lambda t: jnp.transpose(t, (0, 2, 1, 3))
    ones_rows = jnp.zeros((bsz, g, ONES_ROWS, 1), BF16).at[:, :, 0].set(1.0)

    def to_cols_ones(t):
        vt = jnp.transpose(t, (0, 2, 3, 1))
        return jnp.concatenate([vt, jnp.broadcast_to(ones_rows, vt.shape[:2] + (ONES_ROWS, vt.shape[3]))],
                               axis=2)

    nh = s // CMP_STRIDE
    half = CMP_STRIDE * hd

    def compress(t, prm):
        pe, w1, b1, w2 = prm
        x = to_rows(t).reshape(bsz, g, nh, half)
        return _compress(x, pe.reshape(2, half), w1.astype(BF16), b1.reshape(1, -1), w2.astype(BF16))

    kc = compress(kv[:, :, 0], cmp_k)
    vct = to_cols_ones(jnp.transpose(compress(kv[:, :, 1], cmp_v), (0, 2, 1, 3)))
    oct, selt = _cmp_topk(qt, kc, vct, ovt)

    gatet = jnp.transpose(ga[:, :NSA_GATE].reshape(bsz, s, g, 3 * NSA_REP), (0, 2, 3, 1))
    gatet = jnp.pad(gatet, ((0, 0), (0, 0), (0, GATE_ROWS - 3 * NSA_REP), (0, 0)))
    ket = jnp.concatenate([jnp.broadcast_to(expand, (bsz, g) + expand.shape), to_rows(kv[:, :, 2])], axis=3)
    o_a = _sel_win(qt, ket, to_cols_ones(kv[:, :, 3]), to_rows(kv[:, :, 4]), to_cols_ones(kv[:, :, 5]),
                   selt, oct, gatet, min(K_TILE, s))

    o_b = _retention(ret.reshape(bsz, s, 4 * RET_W), *ret_tabs,
                     ret_gn_g.reshape(1, -1), ret_gn_b.reshape(1, -1))

    merged = _merge(hb, o_a.reshape(n, NSA_Q), o_b.reshape(n, RET_W),
                    w_merge_gate[:, :d].astype(BF16), w_merge_gate[:, d:].astype(BF16),
                    w_proj_a.astype(BF16), w_proj_b.astype(BF16))
    return _proj_ln(hf, merged, w_o.astype(BF16), ln_g.reshape(1, -1), ln_b.reshape(1, -1), alpha)


def kernel(x, ffn1_w_gate, ffn1_w_up, ffn1_w_down, ln1_g, ln1_b, w_in, cmp_k_pe, cmp_k_w1, cmp_k_b1,
           cmp_k_w2, cmp_v_pe, cmp_v_w1, cmp_v_b1, cmp_v_w2, ret_gn_g, ret_gn_b, w_merge_gate, w_proj_a,
           w_proj_b, w_o, ln2_g, ln2_b, ffn2_w_gate, ffn2_w_up, ffn2_w_down, ln3_g, ln3_b):
    bsz, s, d = x.shape
    depth = ffn1_w_gate.shape[0]
    f = ffn1_w_gate.shape[2]
    alpha = (2 * depth) ** 0.25
    fpad = (-f) % min(FF_TILE, f)
    nb = s // SLC_BLOCK
    tables = (_overlap_matrix(s // CMP_STRIDE, nb), _expand_matrix(nb, s), _retention_tables(s))

    def ffn_weights(wg, wu, wd):
        return (jnp.pad(wg, ((0, 0), (0, fpad))).astype(BF16), jnp.pad(wu, ((0, 0), (0, fpad))).astype(BF16),
                jnp.pad(wd, ((0, fpad), (0, 0))).astype(BF16))

    row = lambda t: t.reshape(1, -1)
    xf = x.reshape(bsz * s, d)
    for l in range(depth):
        hf, hb = _ffn_ln(xf, *ffn_weights(ffn1_w_gate[l], ffn1_w_up[l], ffn1_w_down[l]),
                         row(ln1_g[l]), row(ln1_b[l]), alpha, True)
        xf = _mixer(hf, hb, bsz, s, w_in[l],
                    (cmp_k_pe[l], cmp_k_w1[l], cmp_k_b1[l], cmp_k_w2[l]),
                    (cmp_v_pe[l], cmp_v_w1[l], cmp_v_b1[l], cmp_v_w2[l]),
                    ret_gn_g[l], ret_gn_b[l], w_merge_gate[l], w_proj_a[l], w_proj_b[l], w_o[l],
                    ln2_g[l], ln2_b[l], alpha, tables)
        xf, _ = _ffn_ln(xf, *ffn_weights(ffn2_w_gate[l], ffn2_w_up[l], ffn2_w_down[l]),
                        row(ln3_g[l]), row(ln3_b[l]), alpha, False)
    return xf.reshape(bsz, s, d)
```

```python
import functools
import math

import jax
import jax.numpy as jnp
import numpy as np
from jax import lax
from jax.experimental import pallas as pl
from jax.experimental.pallas import tpu as pltpu

NSA_HEADS = 16
NSA_GROUPS = 4
NSA_REP = NSA_HEADS // NSA_GROUPS
HEAD_DIM = 64
CMP_BLOCK = 32
CMP_STRIDE = 16
SLC_BLOCK = 64
SLC_TOPN = 16
N_FORCED = 3
CAUSAL_PARTS = 4
WINDOW = 512
RET_HEADS = 8
RET_DIM = 128
RET_CHUNK = 128
ROPE_BASE = 10000.0
LN_EPS = 1e-5
NEG = -1e30
TINY = 1e-30
NSA_Q = NSA_HEADS * HEAD_DIM
NSA_KV = NSA_GROUPS * HEAD_DIM
NSA_GATE = 3 * NSA_HEADS
GATE_ROWS = 16
ONES_ROWS = 16
QK_SCALE = HEAD_DIM ** -0.5 * math.log2(math.e)
RET_W = RET_HEADS * RET_DIM

LANE = 128
SUBLANE = 8
VMEM_LIMIT = 52 * 1024 * 1024
ROW_TILE = 512
FF_TILE = 512
COL_TILE = 512
PROJ_ROW_TILE = 1024
PROJ_COL_TILE = 1024
Q_TILE = 256
K_TILE = 512
RET_TILE = 1024

BF16 = jnp.bfloat16
F32 = jnp.float32


def _dot(a, b):
    return jnp.dot(a, b, preferred_element_type=F32)


def _dot_nt(a, b):
    return lax.dot_general(a, b, (((1,), (1,)), ((), ())), preferred_element_type=F32)


def _dot_tn(a, b):
    return lax.dot_general(a, b, (((0,), (0,)), ((), ())), preferred_element_type=F32)


def _sigmoid(x):
    return 1.0 / (1.0 + jnp.exp(-x))


def _layer_norm_rows(y, g, b):
    mu = jnp.mean(y, axis=-1, keepdims=True)
    d = y - mu
    var = jnp.mean(d * d, axis=-1, keepdims=True)
    return d * lax.rsqrt(var + LN_EPS) * g + b


def _params(sem):
    return pltpu.CompilerParams(dimension_semantics=sem, vmem_limit_bytes=VMEM_LIMIT)


def _ffn_ln_kernel(x_ref, wg_ref, wu_ref, wd_ref, g_ref, b_ref, *rest, alpha, nf, with_bf16):
    if with_bf16:
        o_ref, ob_ref, xb_sc, acc_sc = rest
    else:
        o_ref, xb_sc, acc_sc = rest
        ob_ref = None
    f = pl.program_id(1)

    @pl.when(f == 0)
    def _():
        xb_sc[...] = x_ref[...].astype(BF16)
        acc_sc[...] = jnp.zeros_like(acc_sc)

    xb = xb_sc[...]
    a = _dot(xb, wg_ref[...])
    u = _dot(xb, wu_ref[...])
    h = (a * _sigmoid(a)) * u
    acc_sc[...] += _dot(h.astype(BF16), wd_ref[...])

    @pl.when(f == nf - 1)
    def _():
        y = alpha * x_ref[...] + 0.5 * acc_sc[...]
        out = _layer_norm_rows(y, g_ref[...], b_ref[...])
        o_ref[...] = out
        if with_bf16:
            ob_ref[...] = out.astype(BF16)


def _ffn_ln(x, wg, wu, wd, g, b, alpha, with_bf16):
    n, d = x.shape
    fp = wg.shape[1]
    bm = min(ROW_TILE, n)
    bf = min(FF_TILE, fp)
    nf = fp // bf
    out_shape = [jax.ShapeDtypeStruct((n, d), F32)]
    out_specs = [pl.BlockSpec((bm, d), lambda i, f: (i, 0))]
    if with_bf16:
        out_shape.append(jax.ShapeDtypeStruct((n, d), BF16))
        out_specs.append(pl.BlockSpec((bm, d), lambda i, f: (i, 0)))
    res = pl.pallas_call(
        functools.partial(_ffn_ln_kernel, alpha=alpha, nf=nf, with_bf16=with_bf16),
        grid=(n // bm, nf),
        in_specs=[
            pl.BlockSpec((bm, d), lambda i, f: (i, 0)),
            pl.BlockSpec((d, bf), lambda i, f: (0, f)),
            pl.BlockSpec((d, bf), lambda i, f: (0, f)),
            pl.BlockSpec((bf, d), lambda i, f: (f, 0)),
            pl.BlockSpec((1, d), lambda i, f: (0, 0)),
            pl.BlockSpec((1, d), lambda i, f: (0, 0)),
        ],
        out_specs=out_specs,
        out_shape=out_shape,
        scratch_shapes=[pltpu.VMEM((bm, d), BF16), pltpu.VMEM((bm, d), F32)],
        compiler_params=_params(("parallel", "arbitrary")),
        name="ffn_ln",
    )(x, wg, wu, wd, g, b)
    return res if with_bf16 else (res[0], None)


def _mm_kernel(x_ref, w_ref, o_ref, *, act, scale):
    y = _dot(x_ref[...], w_ref[...])
    if scale is not None:
        y = y * scale
    if act == "sigmoid":
        y = _sigmoid(y)
    o_ref[...] = y.astype(o_ref.dtype)


def _matmul(x, w, act, out_dtype, name, scale=None):
    n, k = x.shape
    nout = w.shape[1]
    bm = min(PROJ_ROW_TILE, n)
    bn = next(c for c in (PROJ_COL_TILE, COL_TILE, 2 * LANE, LANE, nout) if nout % c == 0)
    assert n % bm == 0
    return pl.pallas_call(
        functools.partial(_mm_kernel, act=act, scale=scale),
        grid=(n // bm, nout // bn),
        in_specs=[
            pl.BlockSpec((bm, k), lambda i, j: (i, 0)),
            pl.BlockSpec((k, bn), lambda i, j: (0, j)),
        ],
        out_specs=pl.BlockSpec((bm, bn), lambda i, j: (i, j)),
        out_shape=jax.ShapeDtypeStruct((n, nout), out_dtype),
        compiler_params=_params(("parallel", "arbitrary")),
        name=name,
    )(x, w)


def _nsa_proj_kernel(x_ref, wq_ref, wkv_ref, exp_ref, qt_ref, kc_ref, vc_ref, ket_ref, vst_ref, kw_ref,
                     vwt_ref, *, chunk):
    x = x_ref[...]
    hd, g = HEAD_DIM, NSA_GROUPS
    pair = 2 * hd
    yq = _dot(x, wq_ref[...]) * QK_SCALE
    for p in range(NSA_HEADS // 2):
        t = yq[:, p * pair:(p + 1) * pair].T.astype(qt_ref.dtype)
        qt_ref[0, 2 * p] = t[0:hd]
        qt_ref[0, 2 * p + 1] = t[hd:pair]
    ykv = _dot(x, wkv_ref[...])
    col = lambda kind, gg: (kind * g + gg) * hd
    ones = jnp.where(lax.broadcasted_iota(jnp.int32, (ONES_ROWS, x.shape[0]), 0) == 0, 1.0, 0.0)
    for gg in range(g):
        rows = lambda kind: ykv[:, col(kind, gg):col(kind, gg) + hd]
        kc_ref[0, gg] = rows(0).astype(kc_ref.dtype)
        vc_ref[0, gg] = rows(1).astype(vc_ref.dtype)
        ket_ref[0, gg, :, 0:chunk] = exp_ref[...]
        ket_ref[0, gg, :, chunk:chunk + hd] = rows(2).astype(ket_ref.dtype)
        kw_ref[0, gg] = rows(4).astype(kw_ref.dtype)
    for kind, out_ref in ((3, vst_ref), (5, vwt_ref)):
        for gg in range(0, g, 2):
            t = ykv[:, col(kind, gg):col(kind, gg) + pair].T.astype(out_ref.dtype)
            for k in range(2):
                out_ref[0, gg + k, 0:hd] = t[k * hd:(k + 1) * hd]
                out_ref[0, gg + k, hd:hd + ONES_ROWS] = ones.astype(out_ref.dtype)


def _nsa_proj(hb, w_q, w_kv, expand, bsz, s):
    n, d = hb.shape
    g, hd = NSA_GROUPS, HEAD_DIM
    chunk = expand.shape[1]
    bm = min(ROW_TILE, s)
    nj = s // bm
    rows_spec = lambda w: pl.BlockSpec((1, g, bm, w), lambda b, j: (b, 0, j, 0))
    cols_spec = lambda h, r: pl.BlockSpec((1, h, r, bm), lambda b, j: (b, 0, 0, j))
    full = lambda shape: pl.BlockSpec(shape, lambda b, j: (0, 0), pipeline_mode=pl.Buffered(1))
    rows_shape = lambda w: jax.ShapeDtypeStruct((bsz, g, s, w), BF16)
    cols_shape = lambda h, r: jax.ShapeDtypeStruct((bsz, h, r, s), BF16)
    return pl.pallas_call(
        functools.partial(_nsa_proj_kernel, chunk=chunk),
        grid=(bsz, nj),
        in_specs=[
            pl.BlockSpec((bm, d), lambda b, j: (b * nj + j, 0)),
            full(w_q.shape), full(w_kv.shape),
            pl.BlockSpec((bm, chunk), lambda b, j: (j, 0)),
        ],
        out_specs=[cols_spec(NSA_HEADS, hd), rows_spec(hd), rows_spec(hd), rows_spec(chunk + hd),
                   cols_spec(g, hd + ONES_ROWS), rows_spec(hd), cols_spec(g, hd + ONES_ROWS)],
        out_shape=[cols_shape(NSA_HEADS, hd), rows_shape(hd), rows_shape(hd), rows_shape(chunk + hd),
                   cols_shape(g, hd + ONES_ROWS), rows_shape(hd), cols_shape(g, hd + ONES_ROWS)],
        compiler_params=_params(("parallel", "arbitrary")),
        name="nsa_proj",
    )(hb, w_q, w_kv, expand)


def _compress_kernel(x_ref, pe_ref, w1_ref, b1_ref, w2_ref, o_ref, *, nh, half):
    x = x_ref[0, 0].astype(F32)
    xa = (x + pe_ref[0:1, :]).astype(BF16)
    xb = (x + pe_ref[1:2, :]).astype(BF16)
    ha = _dot(xa, w1_ref[0:half, :])
    hb = _dot(xb, w1_ref[half:2 * half, :])
    hid = ha + pltpu.roll(hb, nh - 1, 0) + b1_ref[...]
    c = math.sqrt(2.0 / math.pi)
    act = 0.5 * hid * (1.0 + jnp.tanh(c * (hid + 0.044715 * (hid * hid * hid))))
    o_ref[0, 0] = _dot(act.astype(BF16), w2_ref[...]).astype(o_ref.dtype)


def _compress(x, pe2, w1, b1, w2):
    bsz, g, nh, half = x.shape
    hid = w1.shape[1]
    hd = w2.shape[1]
    return pl.pallas_call(
        functools.partial(_compress_kernel, nh=nh, half=half),
        grid=(bsz, g),
        in_specs=[
            pl.BlockSpec((1, 1, nh, half), lambda b, gg: (b, gg, 0, 0)),
            pl.BlockSpec((2, half), lambda b, gg: (0, 0)),
            pl.BlockSpec((2 * half, hid), lambda b, gg: (0, 0)),
            pl.BlockSpec((1, hid), lambda b, gg: (0, 0)),
            pl.BlockSpec((hid, hd), lambda b, gg: (0, 0)),
        ],
        out_specs=pl.BlockSpec((1, 1, nh, hd), lambda b, gg: (b, gg, 0, 0)),
        out_shape=jax.ShapeDtypeStruct((bsz, g, nh, hd), BF16),
        compiler_params=_params(("parallel", "parallel")),
        name="compress",
    )(x, pe2, w1, b1, w2)


def _stack_heads(qt_ref):
    return jnp.concatenate([qt_ref[0, r] for r in range(NSA_REP)], axis=1)


def _per_head(x):
    return jnp.tile(x, (1, NSA_REP))


def _cmp_topk_kernel(qt_ref, kc_ref, vct_ref, ovt_ref, oct_ref, selt_ref, *, tq, nh, nb, n_sel, nq):
    i = pl.program_id(2)
    for part in range(CAUSAL_PARTS):
        lo, hi = part * nq // CAUSAL_PARTS, (part + 1) * nq // CAUSAL_PARTS

        @pl.when(jnp.logical_and(i >= lo, i < hi))
        def _(part=part):
            _cmp_topk_part(qt_ref, kc_ref, vct_ref, ovt_ref, oct_ref, selt_ref, i * tq, tq=tq,
                           nh=(part + 1) * nh // CAUSAL_PARTS, nb=(part + 1) * nb // CAUSAL_PARTS,
                           nb_all=nb, n_sel=n_sel)


def _cmp_topk_part(qt_ref, kc_ref, vct_ref, ovt_ref, oct_ref, selt_ref, t0, *, tq, nh, nb, nb_all, n_sel):
    cend = lax.broadcasted_iota(jnp.int32, (nh, tq), 0) * CMP_STRIDE + (CMP_BLOCK - 1)
    tpos = t0 + lax.broadcasted_iota(jnp.int32, (nh, tq), 1)
    bias = jnp.where(cend <= tpos, 0.0, NEG)
    seen = jnp.where(t0 + lax.broadcasted_iota(jnp.int32, (1, tq), 1) >= CMP_BLOCK - 1, 1.0, 0.0)
    qt = _stack_heads(qt_ref)
    s = _dot(kc_ref[0, 0, 0:nh, :], qt) + _per_head(bias)
    e = jnp.exp2(s - jnp.max(s, axis=0, keepdims=True))
    ea = _dot(vct_ref[0, 0, :, 0:nh], e.astype(BF16))
    inv = _per_head(seen) / jnp.maximum(ea[HEAD_DIM:HEAD_DIM + 1], TINY)
    oct_ref[0, 0] = ea[0:HEAD_DIM] * inv
    p = e * inv
    psum = p[:, 0:tq]
    for r in range(1, NSA_REP):
        psum = psum + p[:, r * tq:(r + 1) * tq]
    imp = _dot(ovt_ref[0:nb, 0:nh], psum.astype(BF16))
    blk = lax.broadcasted_iota(jnp.int32, (nb, tq), 0).astype(F32)
    cur = ((t0 + lax.broadcasted_iota(jnp.int32, (nb, tq), 1)) // SLC_BLOCK).astype(F32)
    forced = (blk == 0.0) | (blk == cur) | (blk == cur - 1.0)
    x = jnp.where(forced, -jnp.inf, jnp.where(blk <= cur, imp, NEG))
    selb = jnp.where(forced, 0.0, NEG)
    for _ in range(n_sel - N_FORCED):
        mx = jnp.max(x, axis=0, keepdims=True)
        idx = jnp.min(jnp.where(x == mx, blk, float(nb)), axis=0, keepdims=True)
        hit = blk == idx
        selb = jnp.where(hit, 0.0, selb)
        x = jnp.where(hit, -jnp.inf, x)
    selt_ref[0, 0, 0:nb, :] = selb.astype(selt_ref.dtype)
    if nb < nb_all:
        selt_ref[0, 0, nb:nb_all, :] = jnp.full((nb_all - nb, tq), NEG, selt_ref.dtype)


def _cmp_topk(qt, kc, vct, ovt):
    bsz, _, hd, s = qt.shape
    g, nh = kc.shape[1], kc.shape[2]
    nb = ovt.shape[0]
    assert nb >= SLC_TOPN and nh % CAUSAL_PARTS == 0 and nb % CAUSAL_PARTS == 0
    tq = min(Q_TILE, s)
    cols = NSA_REP * tq
    return pl.pallas_call(
        functools.partial(_cmp_topk_kernel, tq=tq, nh=nh, nb=nb, n_sel=SLC_TOPN, nq=s // tq),
        grid=(bsz, g, s // tq),
        in_specs=[
            pl.BlockSpec((1, NSA_REP, hd, tq), lambda b, gg, i: (b, gg, 0, i)),
            pl.BlockSpec((1, 1, nh, hd), lambda b, gg, i: (b, gg, 0, 0)),
            pl.BlockSpec((1, 1, hd + ONES_ROWS, nh), lambda b, gg, i: (b, gg, 0, 0)),
            pl.BlockSpec((nb, nh), lambda b, gg, i: (0, 0)),
        ],
        out_specs=[
            pl.BlockSpec((1, 1, hd, cols), lambda b, gg, i: (b, gg, 0, i)),
            pl.BlockSpec((1, 1, nb, tq), lambda b, gg, i: (b, gg, 0, i)),
        ],
        out_shape=[
            jax.ShapeDtypeStruct((bsz, g, hd, NSA_REP * s), F32),
            jax.ShapeDtypeStruct((bsz, g, nb, s), BF16),
        ],
        compiler_params=_params(("parallel", "parallel", "arbitrary")),
        name="cmp_topk",
    )(qt, kc, vct, ovt)


def _sel_win_kernel(qt_ref, ket_ref, vst_ref, kw_ref, vwt_ref, selt_ref, oct_ref, gatet_ref,
                    o_ref, rhs_sc, m_sc, acc_sc, sa_sc, sb_sc, ma_sc, mb_sc,
                    *, tq, tk, chunk, nvar, nchunk, wlen):
    t0 = pl.program_id(2) * tq
    qt = _stack_heads(qt_ref)
    for c in range(nchunk):
        selt = selt_ref[0, 0, c * chunk:(c + 1) * chunk, :]
        rhs_sc[c] = jnp.concatenate([_per_head(selt), qt], axis=0)

    m_sc[...] = jnp.full_like(m_sc, NEG)
    acc_sc[...] = jnp.zeros_like(acc_sc)

    def scores(kt, causal):
        k0 = pl.multiple_of(kt * tk, tk)
        s = _dot(ket_ref[0, 0, pl.ds(k0, tk), :], rhs_sc[kt // nvar])
        if causal:
            kpos = k0 + lax.broadcasted_iota(jnp.int32, (tk, tq), 0)
            tcol = t0 + lax.broadcasted_iota(jnp.int32, (tk, tq), 1)
            s = s + _per_head(jnp.where(kpos <= tcol, 0.0, NEG))
        return s

    def put_scores(slot, kt, causal):
        s_ref, mx_ref = slot
        s = scores(kt, causal)
        s_ref[...] = s
        mx_ref[...] = jnp.max(s, axis=0, keepdims=True)

    def accumulate(slot, kt):
        s_ref, mx_ref = slot
        k0 = pl.multiple_of(kt * tk, tk)
        m_old = m_sc[...]
        m_new = jnp.maximum(m_old, mx_ref[...])
        alpha = jnp.exp2(m_old - m_new)
        p = jnp.exp2(s_ref[...] - m_new)
        acc_sc[...] = alpha * acc_sc[...] + _dot(vst_ref[0, 0, :, pl.ds(k0, tk)], p.astype(BF16))
        m_sc[...] = m_new

    def stage(dst, kt_new, causal, src, kt_old):
        put_scores(dst, kt_new, causal)
        accumulate(src, kt_old)

    slot_a, slot_b = (sa_sc, ma_sc), (sb_sc, mb_sc)
    nfull = (t0 + tq - 1) // tk

    @pl.when(nfull == 0)
    def _():
        put_scores(slot_a, 0, True)
        accumulate(slot_a, 0)

    @pl.when(nfull > 0)
    def _():
        put_scores(slot_a, 0, False)

        def body(j, carry):
            stage(slot_b, 2 * j + 1, False, slot_a, 2 * j)
            stage(slot_a, 2 * j + 2, False, slot_b, 2 * j + 1)
            return carry

        lax.fori_loop(0, (nfull - 1) // 2, body, 0)

        @pl.when(nfull % 2 == 1)
        def _():
            stage(slot_b, nfull, True, slot_a, nfull - 1)
            accumulate(slot_b, nfull)

        @pl.when(nfull % 2 == 0)
        def _():
            stage(slot_b, nfull - 1, False, slot_a, nfull - 2)
            stage(slot_a, nfull, True, slot_b, nfull - 1)
            accumulate(slot_a, nfull)

    o_s = acc_sc[0:HEAD_DIM] * (1.0 / jnp.maximum(acc_sc[HEAD_DIM:HEAD_DIM + 1], TINY))

    wk = wlen + tq
    w0 = pl.multiple_of(jnp.maximum(t0 - wlen, 0), tq)
    diff = (t0 + lax.broadcasted_iota(jnp.int32, (wk, tq), 1)) - (
        w0 + lax.broadcasted_iota(jnp.int32, (wk, tq), 0))
    wbias = jnp.where(jnp.logical_and(diff >= 0, diff < wlen), 0.0, NEG)
    sw = _dot(kw_ref[0, 0, pl.ds(w0, wk), :], qt) + _per_head(wbias)
    ew = jnp.exp2(sw - jnp.max(sw, axis=0, keepdims=True))
    ow = _dot(vwt_ref[0, 0, :, pl.ds(w0, wk)], ew.astype(BF16))
    o_w = ow[0:HEAD_DIM] * (1.0 / jnp.maximum(ow[HEAD_DIM:HEAD_DIM + 1], TINY))

    gate = gatet_ref[0, 0]
    o_c = oct_ref[0, 0]
    outs = []
    for r in range(NSA_REP):
        cs = slice(r * tq, (r + 1) * tq)
        outs.append(gate[3 * r:3 * r + 1] * o_c[:, cs] + gate[3 * r + 1:3 * r + 2] * o_s[:, cs]
                    + gate[3 * r + 2:3 * r + 3] * o_w[:, cs])
    pairs = [jnp.concatenate(outs[r:r + 2], axis=0).T for r in range(0, NSA_REP, 2)]
    o_ref[0] = jnp.concatenate(pairs, axis=1).astype(o_ref.dtype)


def _sel_win(qt, ket, vst, kw, vwt, selt, oct, gatet, tk):
    bsz, _, hd, s = qt.shape
    g = ket.shape[1]
    nb = selt.shape[2]
    chunk = ket.shape[3] - hd
    nchunk = nb // chunk
    nvar = chunk * SLC_BLOCK // tk
    tq = min(Q_TILE, s)
    cols = NSA_REP * tq
    resident = lambda shape: pl.BlockSpec(shape, lambda b, gg, i: (b, gg, 0, 0),
                                          pipeline_mode=pl.Buffered(1))
    return pl.pallas_call(
        functools.partial(_sel_win_kernel, tq=tq, tk=tk, chunk=chunk, nvar=nvar, nchunk=nchunk,
                          wlen=WINDOW),
        grid=(bsz, g, s // tq),
        in_specs=[
            pl.BlockSpec((1, NSA_REP, hd, tq), lambda b, gg, i: (b, gg, 0, i)),
            resident((1, 1, s, chunk + hd)),
            resident((1, 1, hd + ONES_ROWS, s)),
            resident((1, 1, s, hd)),
            resident((1, 1, hd + ONES_ROWS, s)),
            pl.BlockSpec((1, 1, nb, tq), lambda b, gg, i: (b, gg, 0, i)),
            pl.BlockSpec((1, 1, hd, cols), lambda b, gg, i: (b, gg, 0, i)),
            pl.BlockSpec((1, 1, GATE_ROWS, tq), lambda b, gg, i: (b, gg, 0, i)),
        ],
        out_specs=pl.BlockSpec((1, tq, NSA_REP * hd), lambda b, gg, i: (b, i, gg)),
        out_shape=jax.ShapeDtypeStruct((bsz, s, g * NSA_REP * hd), BF16),
        scratch_shapes=[
            pltpu.VMEM((nchunk, chunk + hd, cols), BF16),
            pltpu.VMEM((1, cols), F32),
            pltpu.VMEM((hd + ONES_ROWS, cols), F32),
            pltpu.VMEM((tk, cols), F32),
            pltpu.VMEM((tk, cols), F32),
            pltpu.VMEM((1, cols), F32),
            pltpu.VMEM((1, cols), F32),
        ],
        compiler_params=_params(("parallel", "parallel", "arbitrary")),
        name="sel_win",
    )(qt, ket, vst, kw, vwt, selt, oct, gatet)


def _retention_kernel(q_ref, k_ref, v_ref, gb_ref, cos_ref, sin_ref, dmat_ref, qdec_ref, kdec_ref,
                      cdec_ref, gng_ref, gnb_ref, o_ref, state_sc, *, nchunk, c):
    @pl.when(pl.program_id(2) == 0)
    def _():
        state_sc[...] = jnp.zeros_like(state_sc)

    dmat = dmat_ref[0]
    qdec = qdec_ref[0]
    kdec = kdec_ref[0]
    cdec = cdec_ref[0]
    kscale = RET_DIM ** -0.5
    half = RET_DIM // 2
    for n in range(nchunk):
        rows = pl.ds(n * c, c)
        cos = cos_ref[rows, :]
        sin = sin_ref[rows, :]
        qf = q_ref[0, rows, :].astype(F32)
        kf = k_ref[0, rows, :].astype(F32)
        qr = qf * cos + pltpu.roll(qf, half, 1) * sin
        kr = (kf * cos + pltpu.roll(kf, half, 1) * sin) * kscale
        v = v_ref[0, rows, :]
        inner = _dot_nt(qr.astype(BF16), kr.astype(BF16)) * dmat
        state = state_sc[...]
        y = _dot(inner.astype(BF16), v) + _dot((qr * qdec).astype(BF16), state.astype(BF16))
        state_sc[...] = state * cdec + _dot_tn((kr * kdec).astype(BF16), v)
        mu = jnp.mean(y, axis=-1, keepdims=True)
        d = y - mu
        var = jnp.mean(d * d, axis=-1, keepdims=True)
        yn = d * lax.rsqrt(var + LN_EPS) * gng_ref[...] + gnb_ref[...]
        gb = gb_ref[0, rows, :].astype(F32)
        o_ref[0, rows, :] = ((gb * _sigmoid(gb)) * yn).astype(o_ref.dtype)


def _retention(ret, cos2, sin2, dmat, qdec, kdec, cdec, gng, gnb):
    bsz, s, _ = ret.shape
    h = RET_HEADS
    c = RET_CHUNK
    tc = min(RET_TILE, s)
    spec = lambda off: pl.BlockSpec((1, tc, RET_DIM), lambda b, hh, j, off=off: (b, j, off + hh))
    hspec = lambda shp: pl.BlockSpec((1,) + shp, lambda b, hh, j: (hh, 0, 0))
    return pl.pallas_call(
        functools.partial(_retention_kernel, nchunk=tc // c, c=c),
        grid=(bsz, h, s // tc),
        in_specs=[
            spec(0), spec(h), spec(2 * h), spec(3 * h),
            pl.BlockSpec((tc, RET_DIM), lambda b, hh, j: (j, 0)),
            pl.BlockSpec((tc, RET_DIM), lambda b, hh, j: (j, 0)),
            hspec((c, c)), hspec((c, RET_DIM)), hspec((c, RET_DIM)), hspec((1, RET_DIM)),
            pl.BlockSpec((1, RET_DIM), lambda b, hh, j: (0, hh)),
            pl.BlockSpec((1, RET_DIM), lambda b, hh, j: (0, hh)),
        ],
        out_specs=pl.BlockSpec((1, tc, RET_DIM), lambda b, hh, j: (b, j, hh)),
        out_shape=jax.ShapeDtypeStruct((bsz, s, h * RET_DIM), BF16),
        scratch_shapes=[pltpu.VMEM((RET_DIM, RET_DIM), F32)],
        compiler_params=_params(("parallel", "parallel", "arbitrary")),
        name="retention",
    )(ret, ret, ret, ret, cos2, sin2, dmat, qdec, kdec, cdec, gng, gnb)


def _merge_kernel(h_ref, oa_ref, ob_ref, wga_ref, wgb_ref, wpa_ref, wpb_ref, o_ref):
    hb = h_ref[...]
    ga = _sigmoid(_dot(hb, wga_ref[...]))
    gb = _sigmoid(_dot(hb, wgb_ref[...]))
    merged = ga * _dot(oa_ref[...], wpa_ref[...]) + gb * _dot(ob_ref[...], wpb_ref[...])
    o_ref[...] = merged.astype(o_ref.dtype)


def _merge(hb, oa, ob, wga, wgb, wpa, wpb):
    n, d = hb.shape
    bm = min(PROJ_ROW_TILE, n)
    bn = min(COL_TILE, d)
    ka, kb = oa.shape[1], ob.shape[1]
    return pl.pallas_call(
        _merge_kernel,
        grid=(n // bm, d // bn),
        in_specs=[
            pl.BlockSpec((bm, d), lambda i, j: (i, 0)),
            pl.BlockSpec((bm, ka), lambda i, j: (i, 0)),
            pl.BlockSpec((bm, kb), lambda i, j: (i, 0)),
            pl.BlockSpec((d, bn), lambda i, j: (0, j)),
            pl.BlockSpec((d, bn), lambda i, j: (0, j)),
            pl.BlockSpec((ka, bn), lambda i, j: (0, j)),
            pl.BlockSpec((kb, bn), lambda i, j: (0, j)),
        ],
        out_specs=pl.BlockSpec((bm, bn), lambda i, j: (i, j)),
        out_shape=jax.ShapeDtypeStruct((n, d), BF16),
        compiler_params=_params(("parallel", "arbitrary")),
        name="merge",
    )(hb, oa, ob, wga, wgb, wpa, wpb)


def _proj_ln_kernel(x_ref, m_ref, w_ref, g_ref, b_ref, o_ref, *, alpha):
    y = alpha * x_ref[...] + _dot(m_ref[...], w_ref[...])
    o_ref[...] = _layer_norm_rows(y, g_ref[...], b_ref[...])


def _proj_ln(x, m, w, g, b, alpha):
    n, d = x.shape
    bm = min(ROW_TILE, n)
    return pl.pallas_call(
        functools.partial(_proj_ln_kernel, alpha=alpha),
        grid=(n // bm,),
        in_specs=[
            pl.BlockSpec((bm, d), lambda i: (i, 0)),
            pl.BlockSpec((bm, d), lambda i: (i, 0)),
            pl.BlockSpec((d, d), lambda i: (0, 0)),
            pl.BlockSpec((1, d), lambda i: (0, 0)),
            pl.BlockSpec((1, d), lambda i: (0, 0)),
        ],
        out_specs=pl.BlockSpec((bm, d), lambda i: (i, 0)),
        out_shape=jax.ShapeDtypeStruct((n, d), F32),
        compiler_params=_params(("parallel",)),
        name="proj_ln",
    )(x, m, w, g, b)


def _overlap_matrix(nh, nb):
    c0 = np.arange(nh) * CMP_STRIDE
    c1 = c0 + CMP_BLOCK
    s0 = np.arange(nb) * SLC_BLOCK
    s1 = s0 + SLC_BLOCK
    ov = (c0[:, None] < s1[None, :]) & (c1[:, None] > s0[None, :])
    ov[nh - 1, :] = False
    return jnp.asarray(ov.T, BF16)


def _expand_matrix(nb, s):
    chunk = min(LANE, nb)
    blk_in_chunk = (np.arange(s) // SLC_BLOCK) % chunk
    return jnp.asarray(blk_in_chunk[:, None] == np.arange(chunk)[None, :], BF16)


def _retention_tables(s):
    h, c, d = RET_HEADS, RET_CHUNK, RET_DIM
    inv = ROPE_BASE ** (-jnp.arange(0, d, 2, dtype=F32) / d)
    ang = jnp.arange(s)[:, None].astype(F32) * inv[None, :]
    cos, sin = jnp.cos(ang), jnp.sin(ang)
    cos2 = jnp.concatenate([cos, cos], -1)
    sin2 = jnp.concatenate([-sin, sin], -1)
    log_g = jnp.log1p(-jnp.exp2(-5.0 - jnp.arange(h, dtype=F32)))
    i = jnp.arange(c, dtype=F32)
    diff = i[:, None] - i[None, :]
    dmat = jnp.where(diff >= 0, jnp.exp(jnp.maximum(diff, 0.0)[None] * log_g[:, None, None]), 0.0)
    kdec = jnp.exp((c - 1 - i)[None, :] * log_g[:, None])
    qdec = jnp.exp((i + 1)[None, :] * log_g[:, None])
    cdec = jnp.exp(c * log_g)
    bc = lambda t: jnp.broadcast_to(t[:, :, None], (h, c, d))
    return cos2, sin2, dmat, bc(qdec), bc(kdec), jnp.broadcast_to(cdec[:, None, None], (h, 1, d))


def _pad_cols(w, mult):
    pad = (-w.shape[1]) % mult
    return jnp.pad(w, ((0, 0), (0, pad))) if pad else w


def _mixer(hf, hb, bsz, s, w_in, cmp_k, cmp_v, ret_gn_g, ret_gn_b, w_merge_gate, w_proj_a, w_proj_b,
           w_o, ln_g, ln_b, alpha, tables):
    n, d = hf.shape
    g, hd = NSA_GROUPS, HEAD_DIM
    ovt, expand, ret_tabs = tables
    o_nsa = NSA_Q + 6 * NSA_KV
    w_q = w_in[:, :NSA_Q].astype(BF16)
    w_kv = w_in[:, NSA_Q:o_nsa].astype(BF16)
    w_gate = _pad_cols(w_in[:, o_nsa:o_nsa + NSA_GATE], LANE).astype(BF16)
    w_ret = w_in[:, o_nsa + NSA_GATE:].astype(BF16)

    qt, kc_in, vc_in, ket, vst, kw, vwt = _nsa_proj(hb, w_q, w_kv, expand, bsz, s)
    ga = _matmul(hb, w_gate, "sigmoid", F32, "proj_gate")
    ret = _matmul(hb, w_ret, None, BF16, "proj_ret")

    nh = s // CMP_STRIDE
    half = CMP_STRIDE * hd

    def compress(t, prm):
        pe, w1, b1, w2 = prm
        return _compress(t.reshape(bsz, g, nh, half), pe.reshape(2, half), w1.astype(BF16),
                         b1.reshape(1, -1), w2.astype(BF16))

    kc = compress(kc_in, cmp_k)
    ones_rows = jnp.zeros((bsz, g, ONES_ROWS, nh), BF16).at[:, :, 0].set(1.0)
    vct = jnp.concatenate([jnp.transpose(compress(vc_in, cmp_v), (0, 1, 3, 2)), ones_rows], axis=2)
    oct, selt = _cmp_topk(qt, kc, vct, ovt)

    gatet = jnp.transpose(ga[:, :NSA_GATE].reshape(bsz, s, g, 3 * NSA_REP), (0, 2, 3, 1))
    gatet = jnp.pad(gatet, ((0, 0), (0, 0), (0, GATE_ROWS - 3 * NSA_REP), (0, 0)))
    o_a = _sel_win(qt, ket, vst, kw, vwt, selt, oct, gatet, min(K_TILE, s))

    o_b = _retention(ret.reshape(bsz, s, 4 * RET_W), *ret_tabs,
                     ret_gn_g.reshape(1, -1), ret_gn_b.reshape(1, -1))

    merged = _merge(hb, o_a.reshape(n, NSA_Q), o_b.reshape(n, RET_W),
                    w_merge_gate[:, :d].astype(BF16), w_merge_gate[:, d:].astype(BF16),
                    w_proj_a.astype(BF16), w_proj_b.astype(BF16))
    return _proj_ln(hf, merged, w_o.astype(BF16), ln_g.reshape(1, -1), ln_b.reshape(1, -1), alpha)


def kernel(x, ffn1_w_gate, ffn1_w_up, ffn1_w_down, ln1_g, ln1_b, w_in, cmp_k_pe, cmp_k_w1, cmp_k_b1,
           cmp_k_w2, cmp_v_pe, cmp_v_w1, cmp_v_b1, cmp_v_w2, ret_gn_g, ret_gn_b, w_merge_gate, w_proj_a,
           w_proj_b, w_o, ln2_g, ln2_b, ffn2_w_gate, ffn2_w_up, ffn2_w_down, ln3_g, ln3_b):
    bsz, s, d = x.shape
    depth = ffn1_w_gate.shape[0]
    f = ffn1_w_gate.shape[2]
    alpha = (2 * depth) ** 0.25
    fpad = (-f) % min(FF_TILE, f)
    nb = s // SLC_BLOCK
    tables = (_overlap_matrix(s // CMP_STRIDE, nb), _expand_matrix(nb, s), _retention_tables(s))

    def ffn_weights(wg, wu, wd):
        return (jnp.pad(wg, ((0, 0), (0, fpad))).astype(BF16), jnp.pad(wu, ((0, 0), (0, fpad))).astype(BF16),
                jnp.pad(wd, ((0, fpad), (0, 0))).astype(BF16))

    row = lambda t: t.reshape(1, -1)
    xf = x.reshape(bsz * s, d)
    for l in range(depth):
        hf, hb = _ffn_ln(xf, *ffn_weights(ffn1_w_gate[l], ffn1_w_up[l], ffn1_w_down[l]),
                         row(ln1_g[l]), row(ln1_b[l]), alpha, True)
        xf = _mixer(hf, hb, bsz, s, w_in[l],
                    (cmp_k_pe[l], cmp_k_w1[l], cmp_k_b1[l], cmp_k_w2[l]),
                    (cmp_v_pe[l], cmp_v_w1[l], cmp_v_b1[l], cmp_v_w2[l]),
                    ret_gn_g[l], ret_gn_b[l], w_merge_gate[l], w_proj_a[l], w_proj_b[l], w_o[l],
                    ln2_g[l], ln2_b[l], alpha, tables)
        xf, _ = _ffn_ln(xf, *ffn_weights(ffn2_w_gate[l], ffn2_w_up[l], ffn2_w_down[l]),
                        row(ln3_g[l]), row(ln3_b[l]), alpha, False)
    return xf.reshape(bsz, s, d)
```

```python
import functools
import math

import jax
import jax.numpy as jnp
import numpy as np
from jax import lax
from jax.experimental import pallas as pl
from jax.experimental.pallas import tpu as pltpu

NSA_HEADS = 16
NSA_GROUPS = 4
NSA_REP = NSA_HEADS // NSA_GROUPS
HEAD_DIM = 64
CMP_BLOCK = 32
CMP_STRIDE = 16
SLC_BLOCK = 64
SLC_TOPN = 16
N_FORCED = 3
CAUSAL_PARTS = 4
WINDOW = 512
RET_HEADS = 8
RET_DIM = 128
RET_CHUNK = 128
ROPE_BASE = 10000.0
LN_EPS = 1e-5
NEG = -1e30
TINY = 1e-30
NSA_Q = NSA_HEADS * HEAD_DIM
NSA_KV = NSA_GROUPS * HEAD_DIM
NSA_GATE = 3 * NSA_HEADS
GATE_ROWS = 16
ONES_ROWS = 16
QK_SCALE = HEAD_DIM ** -0.5 * math.log2(math.e)
RET_W = RET_HEADS * RET_DIM

LANE = 128
SUBLANE = 8
VMEM_LIMIT = 52 * 1024 * 1024
ROW_TILE = 512
FF_TILE = 512
COL_TILE = 512
PROJ_ROW_TILE = 1024
PROJ_COL_TILE = 1024
Q_TILE = 256
K_TILE = 512
RET_TILE = 1024

BF16 = jnp.bfloat16
F32 = jnp.float32


def _dot(a, b):
    return jnp.dot(a, b, preferred_element_type=F32)


def _dot_nt(a, b):
    return lax.dot_general(a, b, (((1,), (1,)), ((), ())), preferred_element_type=F32)


def _dot_tn(a, b):
    return lax.dot_general(a, b, (((0,), (0,)), ((), ())), preferred_element_type=F32)


def _sigmoid(x):
    return 1.0 / (1.0 + jnp.exp(-x))


def _layer_norm_rows(y, g, b):
    mu = jnp.mean(y, axis=-1, keepdims=True)
    d = y - mu
    var = jnp.mean(d * d, axis=-1, keepdims=True)
    return d * lax.rsqrt(var + LN_EPS) * g + b


def _params(sem):
    return pltpu.CompilerParams(dimension_semantics=sem, vmem_limit_bytes=VMEM_LIMIT)


def _ffn_ln_kernel(x_ref, wg_ref, wu_ref, wd_ref, g_ref, b_ref, *rest, alpha, nf, with_bf16):
    if with_bf16:
        o_ref, ob_ref, xb_sc, acc_sc = rest
    else:
        o_ref, xb_sc, acc_sc = rest
        ob_ref = None
    f = pl.program_id(1)

    @pl.when(f == 0)
    def _():
        xb_sc[...] = x_ref[...].astype(BF16)
        acc_sc[...] = jnp.zeros_like(acc_sc)

    xb = xb_sc[...]
    a = _dot(xb, wg_ref[...])
    u = _dot(xb, wu_ref[...])
    h = (a * _sigmoid(a)) * u
    acc_sc[...] += _dot(h.astype(BF16), wd_ref[...])

    @pl.when(f == nf - 1)
    def _():
        y = alpha * x_ref[...] + 0.5 * acc_sc[...]
        out = _layer_norm_rows(y, g_ref[...], b_ref[...])
        o_ref[...] = out
        if with_bf16:
            ob_ref[...] = out.astype(BF16)


def _ffn_ln(x, wg, wu, wd, g, b, alpha, with_bf16):
    n, d = x.shape
    fp = wg.shape[1]
    bm = min(ROW_TILE, n)
    bf = min(FF_TILE, fp)
    nf = fp // bf
    out_shape = [jax.ShapeDtypeStruct((n, d), F32)]
    out_specs = [pl.BlockSpec((bm, d), lambda i, f: (i, 0))]
    if with_bf16:
        out_shape.append(jax.ShapeDtypeStruct((n, d), BF16))
        out_specs.append(pl.BlockSpec((bm, d), lambda i, f: (i, 0)))
    res = pl.pallas_call(
        functools.partial(_ffn_ln_kernel, alpha=alpha, nf=nf, with_bf16=with_bf16),
        grid=(n // bm, nf),
        in_specs=[
            pl.BlockSpec((bm, d), lambda i, f: (i, 0)),
            pl.BlockSpec((d, bf), lambda i, f: (0, f)),
            pl.BlockSpec((d, bf), lambda i, f: (0, f)),
            pl.BlockSpec((bf, d), lambda i, f: (f, 0)),
            pl.BlockSpec((1, d), lambda i, f: (0, 0)),
            pl.BlockSpec((1, d), lambda i, f: (0, 0)),
        ],
        out_specs=out_specs,
        out_shape=out_shape,
        scratch_shapes=[pltpu.VMEM((bm, d), BF16), pltpu.VMEM((bm, d), F32)],
        compiler_params=_params(("parallel", "arbitrary")),
        name="ffn_ln",
    )(x, wg, wu, wd, g, b)
    return res if with_bf16 else (res[0], None)


def _mm_kernel(x_ref, w_ref, o_ref, *, act, scale):
    y = _dot(x_ref[...], w_ref[...])
    if scale is not None:
        y = y * scale
    if act == "sigmoid":
        y = _sigmoid(y)
    o_ref[...] = y.astype(o_ref.dtype)


def _matmul(x, w, act, out_dtype, name, scale=None):
    n, k = x.shape
    nout = w.shape[1]
    bm = min(PROJ_ROW_TILE, n)
    bn = next(c for c in (PROJ_COL_TILE, COL_TILE, 2 * LANE, LANE, nout) if nout % c == 0)
    assert n % bm == 0
    return pl.pallas_call(
        functools.partial(_mm_kernel, act=act, scale=scale),
        grid=(n // bm, nout // bn),
        in_specs=[
            pl.BlockSpec((bm, k), lambda i, j: (i, 0)),
            pl.BlockSpec((k, bn), lambda i, j: (0, j)),
        ],
        out_specs=pl.BlockSpec((bm, bn), lambda i, j: (i, j)),
        out_shape=jax.ShapeDtypeStruct((n, nout), out_dtype),
        compiler_params=_params(("parallel", "arbitrary")),
        name=name,
    )(x, w)


def _nsa_proj_kernel(x_ref, wq_ref, wkv_ref, exp_ref, qt_ref, kc_ref, vc_ref, ket_ref, vst_ref, kw_ref,
                     vwt_ref, *, chunk):
    x = x_ref[...]
    hd, g = HEAD_DIM, NSA_GROUPS
    pair = 2 * hd
    yq = _dot(x, wq_ref[...]) * QK_SCALE
    for p in range(NSA_HEADS // 2):
        t = yq[:, p * pair:(p + 1) * pair].T.astype(qt_ref.dtype)
        qt_ref[0, 2 * p] = t[0:hd]
        qt_ref[0, 2 * p + 1] = t[hd:pair]
    ykv = _dot(x, wkv_ref[...])
    col = lambda kind, gg: (kind * g + gg) * hd
    ones = jnp.where(lax.broadcasted_iota(jnp.int32, (ONES_ROWS, x.shape[0]), 0) == 0, 1.0, 0.0)
    for gg in range(g):
        rows = lambda kind: ykv[:, col(kind, gg):col(kind, gg) + hd]
        kc_ref[0, gg] = rows(0).astype(kc_ref.dtype)
        vc_ref[0, gg] = rows(1).astype(vc_ref.dtype)
        ket_ref[0, gg, :, 0:chunk] = exp_ref[...]
        ket_ref[0, gg, :, chunk:chunk + hd] = rows(2).astype(ket_ref.dtype)
        kw_ref[0, gg] = rows(4).astype(kw_ref.dtype)
    for kind, out_ref in ((3, vst_ref), (5, vwt_ref)):
        for gg in range(0, g, 2):
            t = ykv[:, col(kind, gg):col(kind, gg) + pair].T.astype(out_ref.dtype)
            for k in range(2):
                out_ref[0, gg + k, 0:hd] = t[k * hd:(k + 1) * hd]
                out_ref[0, gg + k, hd:hd + ONES_ROWS] = ones.astype(out_ref.dtype)


def _nsa_proj(hb, w_q, w_kv, expand, bsz, s):
    n, d = hb.shape
    g, hd = NSA_GROUPS, HEAD_DIM
    chunk = expand.shape[1]
    bm = min(ROW_TILE, s)
    nj = s // bm
    rows_spec = lambda w: pl.BlockSpec((1, g, bm, w), lambda b, j: (b, 0, j, 0))
    cols_spec = lambda h, r: pl.BlockSpec((1, h, r, bm), lambda b, j: (b, 0, 0, j))
    full = lambda shape: pl.BlockSpec(shape, lambda b, j: (0, 0), pipeline_mode=pl.Buffered(1))
    rows_shape = lambda w: jax.ShapeDtypeStruct((bsz, g, s, w), BF16)
    cols_shape = lambda h, r: jax.ShapeDtypeStruct((bsz, h, r, s), BF16)
    return pl.pallas_call(
        functools.partial(_nsa_proj_kernel, chunk=chunk),
        grid=(bsz, nj),
        in_specs=[
            pl.BlockSpec((bm, d), lambda b, j: (b * nj + j, 0)),
            full(w_q.shape), full(w_kv.shape),
            pl.BlockSpec((bm, chunk), lambda b, j: (j, 0)),
        ],
        out_specs=[cols_spec(NSA_HEADS, hd), rows_spec(hd), rows_spec(hd), rows_spec(chunk + hd),
                   cols_spec(g, hd + ONES_ROWS), rows_spec(hd), cols_spec(g, hd + ONES_ROWS)],
        out_shape=[cols_shape(NSA_HEADS, hd), rows_shape(hd), rows_shape(hd), rows_shape(chunk + hd),
                   cols_shape(g, hd + ONES_ROWS), rows_shape(hd), cols_shape(g, hd + ONES_ROWS)],
        compiler_params=_params(("parallel", "arbitrary")),
        name="nsa_proj",
    )(hb, w_q, w_kv, expand)


def _compress_kernel(x_ref, pe_ref, w1_ref, b1_ref, w2_ref, o_ref, *, nh, half):
    x = x_ref[0, 0].astype(F32)
    xa = (x + pe_ref[0:1, :]).astype(BF16)
    xb = (x + pe_ref[1:2, :]).astype(BF16)
    ha = _dot(xa, w1_ref[0:half, :])
    hb = _dot(xb, w1_ref[half:2 * half, :])
    hid = ha + pltpu.roll(hb, nh - 1, 0) + b1_ref[...]
    c = math.sqrt(2.0 / math.pi)
    act = 0.5 * hid * (1.0 + jnp.tanh(c * (hid + 0.044715 * (hid * hid * hid))))
    o_ref[0, 0] = _dot(act.astype(BF16), w2_ref[...]).astype(o_ref.dtype)


def _compress(x, pe2, w1, b1, w2):
    bsz, g, nh, half = x.shape
    hid = w1.shape[1]
    hd = w2.shape[1]
    return pl.pallas_call(
        functools.partial(_compress_kernel, nh=nh, half=half),
        grid=(bsz, g),
        in_specs=[
            pl.BlockSpec((1, 1, nh, half), lambda b, gg: (b, gg, 0, 0)),
            pl.BlockSpec((2, half), lambda b, gg: (0, 0)),
            pl.BlockSpec((2 * half, hid), lambda b, gg: (0, 0)),
            pl.BlockSpec((1, hid), lambda b, gg: (0, 0)),
            pl.BlockSpec((hid, hd), lambda b, gg: (0, 0)),
        ],
        out_specs=pl.BlockSpec((1, 1, nh, hd), lambda b, gg: (b, gg, 0, 0)),
        out_shape=jax.ShapeDtypeStruct((bsz, g, nh, hd), BF16),
        compiler_params=_params(("parallel", "parallel")),
        name="compress",
    )(x, pe2, w1, b1, w2)


def _stack_heads(qt_ref):
    return jnp.concatenate([qt_ref[0, r] for r in range(NSA_REP)], axis=1)


def _per_head(x):
    return jnp.tile(x, (1, NSA_REP))


def _cmp_topk_kernel(qt_ref, kc_ref, vct_ref, ovt_ref, oct_ref, selt_ref, *, tq, nh, nb, n_sel, nq):
    i = pl.program_id(2)
    for part in range(CAUSAL_PARTS):
        lo, hi = part * nq // CAUSAL_PARTS, (part + 1) * nq // CAUSAL_PARTS

        @pl.when(jnp.logical_and(i >= lo, i < hi))
        def _(part=part):
            _cmp_topk_part(qt_ref, kc_ref, vct_ref, ovt_ref, oct_ref, selt_ref, i * tq, tq=tq,
                           nh=(part + 1) * nh // CAUSAL_PARTS, nb=(part + 1) * nb // CAUSAL_PARTS,
                           nb_all=nb, n_sel=n_sel)


def _cmp_topk_part(qt_ref, kc_ref, vct_ref, ovt_ref, oct_ref, selt_ref, t0, *, tq, nh, nb, nb_all, n_sel):
    cend = lax.broadcasted_iota(jnp.int32, (nh, tq), 0) * CMP_STRIDE + (CMP_BLOCK - 1)
    tpos = t0 + lax.broadcasted_iota(jnp.int32, (nh, tq), 1)
    bias = jnp.where(cend <= tpos, 0.0, NEG)
    seen = jnp.where(t0 + lax.broadcasted_iota(jnp.int32, (1, tq), 1) >= CMP_BLOCK - 1, 1.0, 0.0)
    qt = _stack_heads(qt_ref)
    s = _dot(kc_ref[0, 0, 0:nh, :], qt) + _per_head(bias)
    e = jnp.exp2(s - jnp.max(s, axis=0, keepdims=True))
    ea = _dot(vct_ref[0, 0, :, 0:nh], e.astype(BF16))
    inv = _per_head(seen) / jnp.maximum(ea[HEAD_DIM:HEAD_DIM + 1], TINY)
    oct_ref[0, 0] = ea[0:HEAD_DIM] * inv
    p = e * inv
    psum = p[:, 0:tq]
    for r in range(1, NSA_REP):
        psum = psum + p[:, r * tq:(r + 1) * tq]
    imp = _dot(ovt_ref[0:nb, 0:nh], psum.astype(BF16))
    blk = lax.broadcasted_iota(jnp.int32, (nb, tq), 0).astype(F32)
    cur = ((t0 + lax.broadcasted_iota(jnp.int32, (nb, tq), 1)) // SLC_BLOCK).astype(F32)
    forced = (blk == 0.0) | (blk == cur) | (blk == cur - 1.0)
    x = jnp.where(forced, -jnp.inf, jnp.where(blk <= cur, imp, NEG))
    selb = jnp.where(forced, 0.0, NEG)
    for _ in range(n_sel - N_FORCED):
        mx = jnp.max(x, axis=0, keepdims=True)
        idx = jnp.min(jnp.where(x == mx, blk, float(nb)), axis=0, keepdims=True)
        hit = blk == idx
        selb = jnp.where(hit, 0.0, selb)
        x = jnp.where(hit, -jnp.inf, x)
    selt_ref[0, 0, 0:nb, :] = selb.astype(selt_ref.dtype)
    if nb < nb_all:
        selt_ref[0, 0, nb:nb_all, :] = jnp.full((nb_all - nb, tq), NEG, selt_ref.dtype)


def _cmp_topk(qt, kc, vct, ovt):
    bsz, _, hd, s = qt.shape
    g, nh = kc.shape[1], kc.shape[2]
    nb = ovt.shape[0]
    assert nb >= SLC_TOPN and nh % CAUSAL_PARTS == 0 and nb % CAUSAL_PARTS == 0
    tq = min(Q_TILE, s)
    cols = NSA_REP * tq
    return pl.pallas_call(
        functools.partial(_cmp_topk_kernel, tq=tq, nh=nh, nb=nb, n_sel=SLC_TOPN, nq=s // tq),
        grid=(bsz, g, s // tq),
        in_specs=[
            pl.BlockSpec((1, NSA_REP, hd, tq), lambda b, gg, i: (b, gg, 0, i)),
            pl.BlockSpec((1, 1, nh, hd), lambda b, gg, i: (b, gg, 0, 0)),
            pl.BlockSpec((1, 1, hd + ONES_ROWS, nh), lambda b, gg, i: (b, gg, 0, 0)),
            pl.BlockSpec((nb, nh), lambda b, gg, i: (0, 0)),
        ],
        out_specs=[
            pl.BlockSpec((1, 1, hd, cols), lambda b, gg, i: (b, gg, 0, i)),
            pl.BlockSpec((1, 1, nb, tq), lambda b, gg, i: (b, gg, 0, i)),
        ],
        out_shape=[
            jax.ShapeDtypeStruct((bsz, g, hd, NSA_REP * s), F32),
            jax.ShapeDtypeStruct((bsz, g, nb, s), BF16),
        ],
        compiler_params=_params(("parallel", "parallel", "arbitrary")),
        name="cmp_topk",
    )(qt, kc, vct, ovt)


def _sel_win_kernel(qt_ref, ket_ref, vst_ref, kw_ref, vwt_ref, selt_ref, oct_ref, gatet_ref,
                    o_ref, rhs_sc, m_sc, acc_sc, sa_sc, sb_sc, ma_sc, mb_sc, sw_sc, mw_sc, ow_sc,
                    *, tq, tk, chunk, nvar, nchunk, wlen):
    t0 = pl.program_id(2) * tq
    qt = _stack_heads(qt_ref)
    for c in range(nchunk):
        selt = selt_ref[0, 0, c * chunk:(c + 1) * chunk, :]
        rhs_sc[c] = jnp.concatenate([_per_head(selt), qt], axis=0)

    m_sc[...] = jnp.full_like(m_sc, NEG)
    acc_sc[...] = jnp.zeros_like(acc_sc)

    def scores(kt, causal):
        k0 = pl.multiple_of(kt * tk, tk)
        s = _dot(ket_ref[0, 0, pl.ds(k0, tk), :], rhs_sc[kt // nvar])
        if causal:
            kpos = k0 + lax.broadcasted_iota(jnp.int32, (tk, tq), 0)
            tcol = t0 + lax.broadcasted_iota(jnp.int32, (tk, tq), 1)
            s = s + _per_head(jnp.where(kpos <= tcol, 0.0, NEG))
        return s

    def put_scores(slot, kt, causal):
        s_ref, mx_ref = slot
        s = scores(kt, causal)
        s_ref[...] = s
        mx_ref[...] = jnp.max(s, axis=0, keepdims=True)

    def accumulate(slot, kt):
        s_ref, mx_ref = slot
        k0 = pl.multiple_of(kt * tk, tk)
        m_old = m_sc[...]
        m_new = jnp.maximum(m_old, mx_ref[...])
        alpha = jnp.exp2(m_old - m_new)
        p = jnp.exp2(s_ref[...] - m_new)
        acc_sc[...] = alpha * acc_sc[...] + _dot(vst_ref[0, 0, :, pl.ds(k0, tk)], p.astype(BF16))
        m_sc[...] = m_new

    def stage(dst, kt_new, causal, src, kt_old):
        put_scores(dst, kt_new, causal)
        accumulate(src, kt_old)

    slot_a, slot_b = (sa_sc, ma_sc), (sb_sc, mb_sc)
    nfull = (t0 + tq - 1) // tk

    wk = wlen + tq
    w0 = pl.multiple_of(jnp.maximum(t0 - wlen, 0), tq)
    diff = (t0 + lax.broadcasted_iota(jnp.int32, (wk, tq), 1)) - (
        w0 + lax.broadcasted_iota(jnp.int32, (wk, tq), 0))
    wbias = jnp.where(jnp.logical_and(diff >= 0, diff < wlen), 0.0, NEG)
    sw = _dot(kw_ref[0, 0, pl.ds(w0, wk), :], qt) + _per_head(wbias)
    sw_sc[...] = sw
    mw_sc[...] = jnp.max(sw, axis=0, keepdims=True)

    def window_out():
        ew = jnp.exp2(sw_sc[...] - mw_sc[...])
        ow_sc[...] = _dot(vwt_ref[0, 0, :, pl.ds(w0, wk)], ew.astype(BF16))

    @pl.when(nfull == 0)
    def _():
        put_scores(slot_a, 0, True)
        window_out()
        accumulate(slot_a, 0)

    @pl.when(nfull > 0)
    def _():
        put_scores(slot_a, 0, False)
        window_out()

        def body(j, carry):
            stage(slot_b, 2 * j + 1, False, slot_a, 2 * j)
            stage(slot_a, 2 * j + 2, False, slot_b, 2 * j + 1)
            return carry

        lax.fori_loop(0, (nfull - 1) // 2, body, 0)

        @pl.when(nfull % 2 == 1)
        def _():
            stage(slot_b, nfull, True, slot_a, nfull - 1)
            accumulate(slot_b, nfull)

        @pl.when(nfull % 2 == 0)
        def _():
            stage(slot_b, nfull - 1, False, slot_a, nfull - 2)
            stage(slot_a, nfull, True, slot_b, nfull - 1)
            accumulate(slot_a, nfull)

    o_s = acc_sc[0:HEAD_DIM] * (1.0 / jnp.maximum(acc_sc[HEAD_DIM:HEAD_DIM + 1], TINY))

    o_w = ow_sc[0:HEAD_DIM] * (1.0 / jnp.maximum(ow_sc[HEAD_DIM:HEAD_DIM + 1], TINY))

    gate = gatet_ref[0, 0]
    o_c = oct_ref[0, 0]
    outs = []
    for r in range(NSA_REP):
        cs = slice(r * tq, (r + 1) * tq)
        outs.append(gate[3 * r:3 * r + 1] * o_c[:, cs] + gate[3 * r + 1:3 * r + 2] * o_s[:, cs]
                    + gate[3 * r + 2:3 * r + 3] * o_w[:, cs])
    pairs = [jnp.concatenate(outs[r:r + 2], axis=0).T for r in range(0, NSA_REP, 2)]
    o_ref[0] = jnp.concatenate(pairs, axis=1).astype(o_ref.dtype)


def _sel_win(qt, ket, vst, kw, vwt, selt, oct, gatet, tk):
    bsz, _, hd, s = qt.shape
    g = ket.shape[1]
    nb = selt.shape[2]
    chunk = ket.shape[3] - hd
    nchunk = nb // chunk
    nvar = chunk * SLC_BLOCK // tk
    tq = min(Q_TILE, s)
    cols = NSA_REP * tq
    resident = lambda shape: pl.BlockSpec(shape, lambda b, gg, i: (b, gg, 0, 0),
                                          pipeline_mode=pl.Buffered(1))
    return pl.pallas_call(
        functools.partial(_sel_win_kernel, tq=tq, tk=tk, chunk=chunk, nvar=nvar, nchunk=nchunk,
                          wlen=WINDOW),
        grid=(bsz, g, s // tq),
        in_specs=[
            pl.BlockSpec((1, NSA_REP, hd, tq), lambda b, gg, i: (b, gg, 0, i)),
            resident((1, 1, s, chunk + hd)),
            resident((1, 1, hd + ONES_ROWS, s)),
            resident((1, 1, s, hd)),
            resident((1, 1, hd + ONES_ROWS, s)),
            pl.BlockSpec((1, 1, nb, tq), lambda b, gg, i: (b, gg, 0, i)),
            pl.BlockSpec((1, 1, hd, cols), lambda b, gg, i: (b, gg, 0, i)),
            pl.BlockSpec((1, 1, GATE_ROWS, tq), lambda b, gg, i: (b, gg, 0, i)),
        ],
        out_specs=pl.BlockSpec((1, tq, NSA_REP * hd), lambda b, gg, i: (b, i, gg)),
        out_shape=jax.ShapeDtypeStruct((bsz, s, g * NSA_REP * hd), BF16),
        scratch_shapes=[
            pltpu.VMEM((nchunk, chunk + hd, cols), BF16),
            pltpu.VMEM((1, cols), F32),
            pltpu.VMEM((hd + ONES_ROWS, cols), F32),
            pltpu.VMEM((tk, cols), F32),
            pltpu.VMEM((tk, cols), F32),
            pltpu.VMEM((1, cols), F32),
            pltpu.VMEM((1, cols), F32),
            pltpu.VMEM((WINDOW + tq, cols), F32),
            pltpu.VMEM((1, cols), F32),
            pltpu.VMEM((hd + ONES_ROWS, cols), F32),
        ],
        compiler_params=_params(("parallel", "parallel", "arbitrary")),
        name="sel_win",
    )(qt, ket, vst, kw, vwt, selt, oct, gatet)


def _retention_kernel(q_ref, k_ref, v_ref, gb_ref, cos_ref, sin_ref, dmat_ref, qdec_ref, kdec_ref,
                      cdec_ref, gng_ref, gnb_ref, o_ref, state_sc, *, nchunk, c):
    @pl.when(pl.program_id(2) == 0)
    def _():
        state_sc[...] = jnp.zeros_like(state_sc)

    dmat = dmat_ref[0]
    qdec = qdec_ref[0]
    kdec = kdec_ref[0]
    cdec = cdec_ref[0]
    kscale = RET_DIM ** -0.5
    half = RET_DIM // 2
    for n in range(nchunk):
        rows = pl.ds(n * c, c)
        cos = cos_ref[rows, :]
        sin = sin_ref[rows, :]
        qf = q_ref[0, rows, :].astype(F32)
        kf = k_ref[0, rows, :].astype(F32)
        qr = qf * cos + pltpu.roll(qf, half, 1) * sin
        kr = (kf * cos + pltpu.roll(kf, half, 1) * sin) * kscale
        v = v_ref[0, rows, :]
        inner = _dot_nt(qr.astype(BF16), kr.astype(BF16)) * dmat
        state = state_sc[...]
        y = _dot(inner.astype(BF16), v) + _dot((qr * qdec).astype(BF16), state.astype(BF16))
        state_sc[...] = state * cdec + _dot_tn((kr * kdec).astype(BF16), v)
        mu = jnp.mean(y, axis=-1, keepdims=True)
        d = y - mu
        var = jnp.mean(d * d, axis=-1, keepdims=True)
        yn = d * lax.rsqrt(var + LN_EPS) * gng_ref[...] + gnb_ref[...]
        gb = gb_ref[0, rows, :].astype(F32)
        o_ref[0, rows, :] = ((gb * _sigmoid(gb)) * yn).astype(o_ref.dtype)


def _retention(ret, cos2, sin2, dmat, qdec, kdec, cdec, gng, gnb):
    bsz, s, _ = ret.shape
    h = RET_HEADS
    c = RET_CHUNK
    tc = min(RET_TILE, s)
    spec = lambda off: pl.BlockSpec((1, tc, RET_DIM), lambda b, hh, j, off=off: (b, j, off + hh))
    hspec = lambda shp: pl.BlockSpec((1,) + shp, lambda b, hh, j: (hh, 0, 0))
    return pl.pallas_call(
        functools.partial(_retention_kernel, nchunk=tc // c, c=c),
        grid=(bsz, h, s // tc),
        in_specs=[
            spec(0), spec(h), spec(2 * h), spec(3 * h),
            pl.BlockSpec((tc, RET_DIM), lambda b, hh, j: (j, 0)),
            pl.BlockSpec((tc, RET_DIM), lambda b, hh, j: (j, 0)),
            hspec((c, c)), hspec((c, RET_DIM)), hspec((c, RET_DIM)), hspec((1, RET_DIM)),
            pl.BlockSpec((1, RET_DIM), lambda b, hh, j: (0, hh)),
            pl.BlockSpec((1, RET_DIM), lambda b, hh, j: (0, hh)),
        ],
        out_specs=pl.BlockSpec((1, tc, RET_DIM), lambda b, hh, j: (b, j, hh)),
        out_shape=jax.ShapeDtypeStruct((bsz, s, h * RET_DIM), BF16),
        scratch_shapes=[pltpu.VMEM((RET_DIM, RET_DIM), F32)],
        compiler_params=_params(("parallel", "parallel", "arbitrary")),
        name="retention",
    )(ret, ret, ret, ret, cos2, sin2, dmat, qdec, kdec, cdec, gng, gnb)


def _merge_kernel(h_ref, oa_ref, ob_ref, wga_ref, wgb_ref, wpa_ref, wpb_ref, o_ref):
    hb = h_ref[...]
    ga = _sigmoid(_dot(hb, wga_ref[...]))
    gb = _sigmoid(_dot(hb, wgb_ref[...]))
    merged = ga * _dot(oa_ref[...], wpa_ref[...]) + gb * _dot(ob_ref[...], wpb_ref[...])
    o_ref[...] = merged.astype(o_ref.dtype)


def _merge(hb, oa, ob, wga, wgb, wpa, wpb):
    n, d = hb.shape
    bm = min(PROJ_ROW_TILE, n)
    bn = min(COL_TILE, d)
    ka, kb = oa.shape[1], ob.shape[1]
    return pl.pallas_call(
        _merge_kernel,
        grid=(n // bm, d // bn),
        in_specs=[
            pl.BlockSpec((bm, d), lambda i, j: (i, 0)),
            pl.BlockSpec((bm, ka), lambda i, j: (i, 0)),
            pl.BlockSpec((bm, kb), lambda i, j: (i, 0)),
            pl.BlockSpec((d, bn), lambda i, j: (0, j)),
            pl.BlockSpec((d, bn), lambda i, j: (0, j)),
            pl.BlockSpec((ka, bn), lambda i, j: (0, j)),
            pl.BlockSpec((kb, bn), lambda i, j: (0, j)),
        ],
        out_specs=pl.BlockSpec((bm, bn), lambda i, j: (i, j)),
        out_shape=jax.ShapeDtypeStruct((n, d), BF16),
        compiler_params=_params(("parallel", "arbitrary")),
        name="merge",
    )(hb, oa, ob, wga, wgb, wpa, wpb)


def _proj_ln_kernel(x_ref, m_ref, w_ref, g_ref, b_ref, o_ref, *, alpha):
    y = alpha * x_ref[...] + _dot(m_ref[...], w_ref[...])
    o_ref[...] = _layer_norm_rows(y, g_ref[...], b_ref[...])


def _proj_ln(x, m, w, g, b, alpha):
    n, d = x.shape
    bm = min(ROW_TILE, n)
    return pl.pallas_call(
        functools.partial(_proj_ln_kernel, alpha=alpha),
        grid=(n // bm,),
        in_specs=[
            pl.BlockSpec((bm, d), lambda i: (i, 0)),
            pl.BlockSpec((bm, d), lambda i: (i, 0)),
            pl.BlockSpec((d, d), lambda i: (0, 0)),
            pl.BlockSpec((1, d), lambda i: (0, 0)),
            pl.BlockSpec((1, d), lambda i: (0, 0)),
        ],
        out_specs=pl.BlockSpec((bm, d), lambda i: (i, 0)),
        out_shape=jax.ShapeDtypeStruct((n, d), F32),
        compiler_params=_params(("parallel",)),
        name="proj_ln",
    )(x, m, w, g, b)


def _overlap_matrix(nh, nb):
    c0 = np.arange(nh) * CMP_STRIDE
    c1 = c0 + CMP_BLOCK
    s0 = np.arange(nb) * SLC_BLOCK
    s1 = s0 + SLC_BLOCK
    ov = (c0[:, None] < s1[None, :]) & (c1[:, None] > s0[None, :])
    ov[nh - 1, :] = False
    return jnp.asarray(ov.T, BF16)


def _expand_matrix(nb, s):
    chunk = min(LANE, nb)
    blk_in_chunk = (np.arange(s) // SLC_BLOCK) % chunk
    return jnp.asarray(blk_in_chunk[:, None] == np.arange(chunk)[None, :], BF16)


def _retention_tables(s):
    h, c, d = RET_HEADS, RET_CHUNK, RET_DIM
    inv = ROPE_BASE ** (-jnp.arange(0, d, 2, dtype=F32) / d)
    ang = jnp.arange(s)[:, None].astype(F32) * inv[None, :]
    cos, sin = jnp.cos(ang), jnp.sin(ang)
    cos2 = jnp.concatenate([cos, cos], -1)
    sin2 = jnp.concatenate([-sin, sin], -1)
    log_g = jnp.log1p(-jnp.exp2(-5.0 - jnp.arange(h, dtype=F32)))
    i = jnp.arange(c, dtype=F32)
    diff = i[:, None] - i[None, :]
    dmat = jnp.where(diff >= 0, jnp.exp(jnp.maximum(diff, 0.0)[None] * log_g[:, None, None]), 0.0)
    kdec = jnp.exp((c - 1 - i)[None, :] * log_g[:, None])
    qdec = jnp.exp((i + 1)[None, :] * log_g[:, None])
    cdec = jnp.exp(c * log_g)
    bc = lambda t: jnp.broadcast_to(t[:, :, None], (h, c, d))
    return cos2, sin2, dmat, bc(qdec), bc(kdec), jnp.broadcast_to(cdec[:, None, None], (h, 1, d))


def _pad_cols(w, mult):
    pad = (-w.shape[1]) % mult
    return jnp.pad(w, ((0, 0), (0, pad))) if pad else w


def _mixer(hf, hb, bsz, s, w_in, cmp_k, cmp_v, ret_gn_g, ret_gn_b, w_merge_gate, w_proj_a, w_proj_b,
           w_o, ln_g, ln_b, alpha, tables):
    n, d = hf.shape
    g, hd = NSA_GROUPS, HEAD_DIM
    ovt, expand, ret_tabs = tables
    o_nsa = NSA_Q + 6 * NSA_KV
    w_q = w_in[:, :NSA_Q].astype(BF16)
    w_kv = w_in[:, NSA_Q:o_nsa].astype(BF16)
    w_gate = _pad_cols(w_in[:, o_nsa:o_nsa + NSA_GATE], LANE).astype(BF16)
    w_ret = w_in[:, o_nsa + NSA_GATE:].astype(BF16)

    qt, kc_in, vc_in, ket, vst, kw, vwt = _nsa_proj(hb, w_q, w_kv, expand, bsz, s)
    ga = _matmul(hb, w_gate, "sigmoid", F32, "proj_gate")
    ret = _matmul(hb, w_ret, None, BF16, "proj_ret")

    nh = s // CMP_STRIDE
    half = CMP_STRIDE * hd

    def compress(t, prm):
        pe, w1, b1, w2 = prm
        return _compress(t.reshape(bsz, g, nh, half), pe.reshape(2, half), w1.astype(BF16),
                         b1.reshape(1, -1), w2.astype(BF16))

    kc = compress(kc_in, cmp_k)
    ones_rows = jnp.zeros((bsz, g, ONES_ROWS, nh), BF16).at[:, :, 0].set(1.0)
    vct = jnp.concatenate([jnp.transpose(compress(vc_in, cmp_v), (0, 1, 3, 2)), ones_rows], axis=2)
    oct, selt = _cmp_topk(qt, kc, vct, ovt)

    gatet = jnp.transpose(ga[:, :NSA_GATE].reshape(bsz, s, g, 3 * NSA_REP), (0, 2, 3, 1))
    gatet = jnp.pad(gatet, ((0, 0), (0, 0), (0, GATE_ROWS - 3 * NSA_REP), (0, 0)))
    o_a = _sel_win(qt, ket, vst, kw, vwt, selt, oct, gatet, min(K_TILE, s))

    o_b = _retention(ret.reshape(bsz, s, 4 * RET_W), *ret_tabs,
                     ret_gn_g.reshape(1, -1), ret_gn_b.reshape(1, -1))

    merged = _merge(hb, o_a.reshape(n, NSA_Q), o_b.reshape(n, RET_W),
                    w_merge_gate[:, :d].astype(BF16), w_merge_gate[:, d:].astype(BF16),
                    w_proj_a.astype(BF16), w_proj_b.astype(BF16))
    return _proj_ln(hf, merged, w_o.astype(BF16), ln_g.reshape(1, -1), ln_b.reshape(1, -1), alpha)


def kernel(x, ffn1_w_gate, ffn1_w_up, ffn1_w_down, ln1_g, ln1_b, w_in, cmp_k_pe, cmp_k_w1, cmp_k_b1,
           cmp_k_w2, cmp_v_pe, cmp_v_w1, cmp_v_b1, cmp_v_w2, ret_gn_g, ret_gn_b, w_merge_gate, w_proj_a,
           w_proj_b, w_o, ln2_g, ln2_b, ffn2_w_gate, ffn2_w_up, ffn2_w_down, ln3_g, ln3_b):
    bsz, s, d = x.shape
    depth = ffn1_w_gate.shape[0]
    f = ffn1_w_gate.shape[2]
    alpha = (2 * depth) ** 0.25
    fpad = (-f) % min(FF_TILE, f)
    nb = s // SLC_BLOCK
    tables = (_overlap_matrix(s // CMP_STRIDE, nb), _expand_matrix(nb, s), _retention_tables(s))

    def ffn_weights(wg, wu, wd):
        return (jnp.pad(wg.astype(BF16), ((0, 0), (0, fpad))), jnp.pad(wu.astype(BF16), ((0, 0), (0, fpad))),
                jnp.pad(wd.astype(BF16), ((0, fpad), (0, 0))))

    row = lambda t: t.reshape(1, -1)
    xf = x.reshape(bsz * s, d)
    for l in range(depth):
        hf, hb = _ffn_ln(xf, *ffn_weights(ffn1_w_gate[l], ffn1_w_up[l], ffn1_w_down[l]),
                         row(ln1_g[l]), row(ln1_b[l]), alpha, True)
        xf = _mixer(hf, hb, bsz, s, w_in[l],
                    (cmp_k_pe[l], cmp_k_w1[l], cmp_k_b1[l], cmp_k_w2[l]),
                    (cmp_v_pe[l], cmp_v_w1[l], cmp_v_b1[l], cmp_v_w2[l]),
                    ret_gn_g[l], ret_gn_b[l], w_merge_gate[l], w_proj_a[l], w_proj_b[l], w_o[l],
                    ln2_g[l], ln2_b[l], alpha, tables)
        xf, _ = _ffn_ln(xf, *ffn_weights(ffn2_w_gate[l], ffn2_w_up[l], ffn2_w_down[l]),
                        row(ln3_g[l]), row(ln3_b[l]), alpha, False)
    return xf.reshape(bsz, s, d)
```

```python
import functools
import math

import jax
import jax.numpy as jnp
import numpy as np
from jax import lax
from jax.experimental import pallas as pl
from jax.experimental.pallas import tpu as pltpu

NSA_HEADS = 16
NSA_GROUPS = 4
NSA_REP = NSA_HEADS // NSA_GROUPS
HEAD_DIM = 64
CMP_BLOCK = 32
CMP_STRIDE = 16
SLC_BLOCK = 64
SLC_TOPN = 16
N_FORCED = 3
CAUSAL_PARTS = 4
COL_PARTS = 2
WINDOW = 512
RET_HEADS = 8
RET_DIM = 128
RET_CHUNK = 128
ROPE_BASE = 10000.0
LN_EPS = 1e-5
NEG = -1e30
TINY = 1e-30
NSA_Q = NSA_HEADS * HEAD_DIM
NSA_KV = NSA_GROUPS * HEAD_DIM
NSA_GATE = 3 * NSA_HEADS
GATE_ROWS = 16
ONES_ROWS = 16
QK_SCALE = HEAD_DIM ** -0.5 * math.log2(math.e)
RET_W = RET_HEADS * RET_DIM

LANE = 128
SUBLANE = 8
VMEM_LIMIT = 52 * 1024 * 1024
ROW_TILE = 512
FF_TILE = 512
COL_TILE = 512
PROJ_ROW_TILE = 1024
PROJ_COL_TILE = 1024
Q_TILE = 256
K_TILE = 512
RET_TILE = 1024

BF16 = jnp.bfloat16
F32 = jnp.float32


def _dot(a, b):
    return jnp.dot(a, b, preferred_element_type=F32)


def _dot_nt(a, b):
    return lax.dot_general(a, b, (((1,), (1,)), ((), ())), preferred_element_type=F32)


def _dot_tn(a, b):
    return lax.dot_general(a, b, (((0,), (0,)), ((), ())), preferred_element_type=F32)


def _sigmoid(x):
    return 1.0 / (1.0 + jnp.exp(-x))


def _layer_norm_rows(y, g, b):
    mu = jnp.mean(y, axis=-1, keepdims=True)
    d = y - mu
    var = jnp.mean(d * d, axis=-1, keepdims=True)
    return d * lax.rsqrt(var + LN_EPS) * g + b


def _params(sem):
    return pltpu.CompilerParams(dimension_semantics=sem, vmem_limit_bytes=VMEM_LIMIT)


def _ffn_ln_kernel(x_ref, wg_ref, wu_ref, wd_ref, g_ref, b_ref, *rest, alpha, nf, with_bf16):
    if with_bf16:
        o_ref, ob_ref, xb_sc, acc_sc = rest
    else:
        o_ref, xb_sc, acc_sc = rest
        ob_ref = None
    f = pl.program_id(1)

    @pl.when(f == 0)
    def _():
        xb_sc[...] = x_ref[...].astype(BF16)
        acc_sc[...] = jnp.zeros_like(acc_sc)

    xb = xb_sc[...]
    a = _dot(xb, wg_ref[...])
    u = _dot(xb, wu_ref[...])
    h = (a * _sigmoid(a)) * u
    acc_sc[...] += _dot(h.astype(BF16), wd_ref[...])

    @pl.when(f == nf - 1)
    def _():
        y = alpha * x_ref[...] + 0.5 * acc_sc[...]
        out = _layer_norm_rows(y, g_ref[...], b_ref[...])
        o_ref[...] = out
        if with_bf16:
            ob_ref[...] = out.astype(BF16)


def _ffn_ln(x, wg, wu, wd, g, b, alpha, with_bf16):
    n, d = x.shape
    fp = wg.shape[1]
    bm = min(ROW_TILE, n)
    bf = min(FF_TILE, fp)
    nf = fp // bf
    out_shape = [jax.ShapeDtypeStruct((n, d), F32)]
    out_specs = [pl.BlockSpec((bm, d), lambda i, f: (i, 0))]
    if with_bf16:
        out_shape.append(jax.ShapeDtypeStruct((n, d), BF16))
        out_specs.append(pl.BlockSpec((bm, d), lambda i, f: (i, 0)))
    res = pl.pallas_call(
        functools.partial(_ffn_ln_kernel, alpha=alpha, nf=nf, with_bf16=with_bf16),
        grid=(n // bm, nf),
        in_specs=[
            pl.BlockSpec((bm, d), lambda i, f: (i, 0)),
            pl.BlockSpec((d, bf), lambda i, f: (0, f)),
            pl.BlockSpec((d, bf), lambda i, f: (0, f)),
            pl.BlockSpec((bf, d), lambda i, f: (f, 0)),
            pl.BlockSpec((1, d), lambda i, f: (0, 0)),
            pl.BlockSpec((1, d), lambda i, f: (0, 0)),
        ],
        out_specs=out_specs,
        out_shape=out_shape,
        scratch_shapes=[pltpu.VMEM((bm, d), BF16), pltpu.VMEM((bm, d), F32)],
        compiler_params=_params(("parallel", "arbitrary")),
        name="ffn_ln",
    )(x, wg, wu, wd, g, b)
    return res if with_bf16 else (res[0], None)


def _mm_kernel(x_ref, w_ref, o_ref, *, act, scale):
    y = _dot(x_ref[...], w_ref[...])
    if scale is not None:
        y = y * scale
    if act == "sigmoid":
        y = _sigmoid(y)
    o_ref[...] = y.astype(o_ref.dtype)


def _matmul(x, w, act, out_dtype, name, scale=None):
    n, k = x.shape
    nout = w.shape[1]
    bm = min(PROJ_ROW_TILE, n)
    bn = next(c for c in (PROJ_COL_TILE, COL_TILE, 2 * LANE, LANE, nout) if nout % c == 0)
    assert n % bm == 0
    return pl.pallas_call(
        functools.partial(_mm_kernel, act=act, scale=scale),
        grid=(n // bm, nout // bn),
        in_specs=[
            pl.BlockSpec((bm, k), lambda i, j: (i, 0)),
            pl.BlockSpec((k, bn), lambda i, j: (0, j)),
        ],
        out_specs=pl.BlockSpec((bm, bn), lambda i, j: (i, j)),
        out_shape=jax.ShapeDtypeStruct((n, nout), out_dtype),
        compiler_params=_params(("parallel", "arbitrary")),
        name=name,
    )(x, w)


def _nsa_proj_kernel(x_ref, wq_ref, wkv_ref, exp_ref, qt_ref, kc_ref, vc_ref, ket_ref, vst_ref, kw_ref,
                     vwt_ref, *, chunk):
    x = x_ref[...]
    hd, g = HEAD_DIM, NSA_GROUPS
    pair = 2 * hd
    yq = _dot(x, wq_ref[...]) * QK_SCALE
    for p in range(NSA_HEADS // 2):
        t = yq[:, p * pair:(p + 1) * pair].T.astype(qt_ref.dtype)
        qt_ref[0, 2 * p] = t[0:hd]
        qt_ref[0, 2 * p + 1] = t[hd:pair]
    ykv = _dot(x, wkv_ref[...])
    col = lambda kind, gg: (kind * g + gg) * hd
    ones = jnp.where(lax.broadcasted_iota(jnp.int32, (ONES_ROWS, x.shape[0]), 0) == 0, 1.0, 0.0)
    for gg in range(g):
        rows = lambda kind: ykv[:, col(kind, gg):col(kind, gg) + hd]
        kc_ref[0, gg] = rows(0).astype(kc_ref.dtype)
        vc_ref[0, gg] = rows(1).astype(vc_ref.dtype)
        ket_ref[0, gg, :, 0:chunk] = exp_ref[...]
        ket_ref[0, gg, :, chunk:chunk + hd] = rows(2).astype(ket_ref.dtype)
        kw_ref[0, gg] = rows(4).astype(kw_ref.dtype)
    for kind, out_ref in ((3, vst_ref), (5, vwt_ref)):
        for gg in range(0, g, 2):
            t = ykv[:, col(kind, gg):col(kind, gg) + pair].T.astype(out_ref.dtype)
            for k in range(2):
                out_ref[0, gg + k, 0:hd] = t[k * hd:(k + 1) * hd]
                out_ref[0, gg + k, hd:hd + ONES_ROWS] = ones.astype(out_ref.dtype)


def _nsa_proj(hb, w_q, w_kv, expand, bsz, s):
    n, d = hb.shape
    g, hd = NSA_GROUPS, HEAD_DIM
    chunk = expand.shape[1]
    bm = min(ROW_TILE, s)
    nj = s // bm
    rows_spec = lambda w: pl.BlockSpec((1, g, bm, w), lambda b, j: (b, 0, j, 0))
    cols_spec = lambda h, r: pl.BlockSpec((1, h, r, bm), lambda b, j: (b, 0, 0, j))
    full = lambda shape: pl.BlockSpec(shape, lambda b, j: (0, 0), pipeline_mode=pl.Buffered(1))
    rows_shape = lambda w: jax.ShapeDtypeStruct((bsz, g, s, w), BF16)
    cols_shape = lambda h, r: jax.ShapeDtypeStruct((bsz, h, r, s), BF16)
    return pl.pallas_call(
        functools.partial(_nsa_proj_kernel, chunk=chunk),
        grid=(bsz, nj),
        in_specs=[
            pl.BlockSpec((bm, d), lambda b, j: (b * nj + j, 0)),
            full(w_q.shape), full(w_kv.shape),
            pl.BlockSpec((bm, chunk), lambda b, j: (j, 0)),
        ],
        out_specs=[cols_spec(NSA_HEADS, hd), rows_spec(hd), rows_spec(hd), rows_spec(chunk + hd),
                   cols_spec(g, hd + ONES_ROWS), rows_spec(hd), cols_spec(g, hd + ONES_ROWS)],
        out_shape=[cols_shape(NSA_HEADS, hd), rows_shape(hd), rows_shape(hd), rows_shape(chunk + hd),
                   cols_shape(g, hd + ONES_ROWS), rows_shape(hd), cols_shape(g, hd + ONES_ROWS)],
        compiler_params=_params(("parallel", "arbitrary")),
        name="nsa_proj",
    )(hb, w_q, w_kv, expand)


def _compress_kernel(x_ref, pe_ref, w1_ref, b1_ref, w2_ref, o_ref, *, nh, half):
    x = x_ref[0, 0].astype(F32)
    xa = (x + pe_ref[0:1, :]).astype(BF16)
    xb = (x + pe_ref[1:2, :]).astype(BF16)
    ha = _dot(xa, w1_ref[0:half, :])
    hb = _dot(xb, w1_ref[half:2 * half, :])
    hid = ha + pltpu.roll(hb, nh - 1, 0) + b1_ref[...]
    c = math.sqrt(2.0 / math.pi)
    act = 0.5 * hid * (1.0 + jnp.tanh(c * (hid + 0.044715 * (hid * hid * hid))))
    o_ref[0, 0] = _dot(act.astype(BF16), w2_ref[...]).astype(o_ref.dtype)


def _compress(x, pe2, w1, b1, w2):
    bsz, g, nh, half = x.shape
    hid = w1.shape[1]
    hd = w2.shape[1]
    return pl.pallas_call(
        functools.partial(_compress_kernel, nh=nh, half=half),
        grid=(bsz, g),
        in_specs=[
            pl.BlockSpec((1, 1, nh, half), lambda b, gg: (b, gg, 0, 0)),
            pl.BlockSpec((2, half), lambda b, gg: (0, 0)),
            pl.BlockSpec((2 * half, hid), lambda b, gg: (0, 0)),
            pl.BlockSpec((1, hid), lambda b, gg: (0, 0)),
            pl.BlockSpec((hid, hd), lambda b, gg: (0, 0)),
        ],
        out_specs=pl.BlockSpec((1, 1, nh, hd), lambda b, gg: (b, gg, 0, 0)),
        out_shape=jax.ShapeDtypeStruct((bsz, g, nh, hd), BF16),
        compiler_params=_params(("parallel", "parallel")),
        name="compress",
    )(x, pe2, w1, b1, w2)


def _stack_heads(qt_ref):
    return jnp.concatenate([qt_ref[0, r] for r in range(NSA_REP)], axis=1)


def _per_head(x):
    return jnp.tile(x, (1, NSA_REP))


def _cmp_topk_kernel(qt_ref, kc_ref, vct_ref, ovt_ref, oct_ref, selt_ref, *, tq, nh, nb, n_sel, nq):
    i = pl.program_id(2)
    for part in range(CAUSAL_PARTS):
        lo, hi = part * nq // CAUSAL_PARTS, (part + 1) * nq // CAUSAL_PARTS

        @pl.when(jnp.logical_and(i >= lo, i < hi))
        def _(part=part):
            _cmp_topk_part(qt_ref, kc_ref, vct_ref, ovt_ref, oct_ref, selt_ref, i * tq, tq=tq,
                           nh=(part + 1) * nh // CAUSAL_PARTS, nb=(part + 1) * nb // CAUSAL_PARTS,
                           nb_all=nb, n_sel=n_sel)


def _cmp_topk_part(qt_ref, kc_ref, vct_ref, ovt_ref, oct_ref, selt_ref, t0, *, tq, nh, nb, nb_all, n_sel):
    cend = lax.broadcasted_iota(jnp.int32, (nh, tq), 0) * CMP_STRIDE + (CMP_BLOCK - 1)
    tpos = t0 + lax.broadcasted_iota(jnp.int32, (nh, tq), 1)
    bias = jnp.where(cend <= tpos, 0.0, NEG)
    seen = jnp.where(t0 + lax.broadcasted_iota(jnp.int32, (1, tq), 1) >= CMP_BLOCK - 1, 1.0, 0.0)
    qt = _stack_heads(qt_ref)
    s = _dot(kc_ref[0, 0, 0:nh, :], qt) + _per_head(bias)
    e = jnp.exp2(s - jnp.max(s, axis=0, keepdims=True))
    ea = _dot(vct_ref[0, 0, :, 0:nh], e.astype(BF16))
    inv = _per_head(seen) / jnp.maximum(ea[HEAD_DIM:HEAD_DIM + 1], TINY)
    oct_ref[0, 0] = ea[0:HEAD_DIM] * inv
    p = e * inv
    psum = p[:, 0:tq]
    for r in range(1, NSA_REP):
        psum = psum + p[:, r * tq:(r + 1) * tq]
    imp = _dot(ovt_ref[0:nb, 0:nh], psum.astype(BF16))
    blk = lax.broadcasted_iota(jnp.int32, (nb, tq), 0).astype(F32)
    cur = ((t0 + lax.broadcasted_iota(jnp.int32, (nb, tq), 1)) // SLC_BLOCK).astype(F32)
    forced = (blk == 0.0) | (blk == cur) | (blk == cur - 1.0)
    x = jnp.where(forced, -jnp.inf, jnp.where(blk <= cur, imp, NEG))
    for _ in range(n_sel - N_FORCED):
        mx = jnp.max(x, axis=0, keepdims=True)
        idx = jnp.min(jnp.where(x == mx, blk, float(nb)), axis=0, keepdims=True)
        x = jnp.where(blk == idx, -jnp.inf, x)
    selb = jnp.where(x == -jnp.inf, 0.0, NEG)
    selt_ref[0, 0, 0:nb, :] = selb.astype(selt_ref.dtype)
    if nb < nb_all:
        selt_ref[0, 0, nb:nb_all, :] = jnp.full((nb_all - nb, tq), NEG, selt_ref.dtype)


def _cmp_topk(qt, kc, vct, ovt):
    bsz, _, hd, s = qt.shape
    g, nh = kc.shape[1], kc.shape[2]
    nb = ovt.shape[0]
    assert nb >= SLC_TOPN and nh % CAUSAL_PARTS == 0 and nb % CAUSAL_PARTS == 0
    tq = min(Q_TILE, s)
    cols = NSA_REP * tq
    return pl.pallas_call(
        functools.partial(_cmp_topk_kernel, tq=tq, nh=nh, nb=nb, n_sel=SLC_TOPN, nq=s // tq),
        grid=(bsz, g, s // tq),
        in_specs=[
            pl.BlockSpec((1, NSA_REP, hd, tq), lambda b, gg, i: (b, gg, 0, i)),
            pl.BlockSpec((1, 1, nh, hd), lambda b, gg, i: (b, gg, 0, 0)),
            pl.BlockSpec((1, 1, hd + ONES_ROWS, nh), lambda b, gg, i: (b, gg, 0, 0)),
            pl.BlockSpec((nb, nh), lambda b, gg, i: (0, 0)),
        ],
        out_specs=[
            pl.BlockSpec((1, 1, hd, cols), lambda b, gg, i: (b, gg, 0, i)),
            pl.BlockSpec((1, 1, nb, tq), lambda b, gg, i: (b, gg, 0, i)),
        ],
        out_shape=[
            jax.ShapeDtypeStruct((bsz, g, hd, NSA_REP * s), F32),
            jax.ShapeDtypeStruct((bsz, g, nb, s), BF16),
        ],
        compiler_params=_params(("parallel", "parallel", "arbitrary")),
        name="cmp_topk",
    )(qt, kc, vct, ovt)


def _sel_win_kernel(qt_ref, ket_ref, vst_ref, kw_ref, vwt_ref, selt_ref, oct_ref, gatet_ref,
                    o_ref, rhs_sc, m_sc, acc_sc, sa_sc, sb_sc, ma_sc, mb_sc, sw_sc, mw_sc, ow_sc,
                    *, tq, tk, chunk, nvar, nchunk, wlen):
    t0 = pl.program_id(2) * tq
    qt = _stack_heads(qt_ref)
    for c in range(nchunk):
        selt = selt_ref[0, 0, c * chunk:(c + 1) * chunk, :]
        rhs_sc[c] = jnp.concatenate([_per_head(selt), qt], axis=0)

    m_sc[...] = jnp.full_like(m_sc, NEG)
    acc_sc[...] = jnp.zeros_like(acc_sc)

    cols = NSA_REP * tq
    part_cols = [slice(c * cols // COL_PARTS, (c + 1) * cols // COL_PARTS) for c in range(COL_PARTS)]

    def put_scores(slot, kt, causal, cs):
        s_ref, mx_ref = slot
        k0 = pl.multiple_of(kt * tk, tk)
        s = _dot(ket_ref[0, 0, pl.ds(k0, tk), :], rhs_sc[kt // nvar, :, cs])
        if causal:
            kpos = k0 + lax.broadcasted_iota(jnp.int32, (tk, tq), 0)
            tcol = t0 + lax.broadcasted_iota(jnp.int32, (tk, tq), 1)
            s = s + jnp.tile(jnp.where(kpos <= tcol, 0.0, NEG), (1, NSA_REP // COL_PARTS))
        s_ref[:, cs] = s
        mx_ref[:, cs] = jnp.max(s, axis=0, keepdims=True)

    def accumulate(slot, kt, cs):
        s_ref, mx_ref = slot
        k0 = pl.multiple_of(kt * tk, tk)
        m_old = m_sc[:, cs]
        m_new = jnp.maximum(m_old, mx_ref[:, cs])
        alpha = jnp.exp2(m_old - m_new)
        p = jnp.exp2(s_ref[:, cs] - m_new)
        acc_sc[:, cs] = alpha * acc_sc[:, cs] + _dot(vst_ref[0, 0, :, pl.ds(k0, tk)], p.astype(BF16))
        m_sc[:, cs] = m_new

    def put_all(slot, kt, causal):
        for cs in part_cols:
            put_scores(slot, kt, causal, cs)

    def accumulate_all(slot, kt):
        for cs in part_cols:
            accumulate(slot, kt, cs)

    def stage(dst, kt_new, causal, src, kt_old):
        for cs in part_cols:
            put_scores(dst, kt_new, causal, cs)
            accumulate(src, kt_old, cs)

    slot_a, slot_b = (sa_sc, ma_sc), (sb_sc, mb_sc)
    nfull = (t0 + tq - 1) // tk

    wk = wlen + tq
    w0 = pl.multiple_of(jnp.maximum(t0 - wlen, 0), tq)
    diff = (t0 + lax.broadcasted_iota(jnp.int32, (wk, tq), 1)) - (
        w0 + lax.broadcasted_iota(jnp.int32, (wk, tq), 0))
    wbias = jnp.where(jnp.logical_and(diff >= 0, diff < wlen), 0.0, NEG)
    sw = _dot(kw_ref[0, 0, pl.ds(w0, wk), :], qt) + _per_head(wbias)
    sw_sc[...] = sw
    mw_sc[...] = jnp.max(sw, axis=0, keepdims=True)

    def window_out():
        ew = jnp.exp2(sw_sc[...] - mw_sc[...])
        ow_sc[...] = _dot(vwt_ref[0, 0, :, pl.ds(w0, wk)], ew.astype(BF16))

    @pl.when(nfull == 0)
    def _():
        put_all(slot_a, 0, True)
        window_out()
        accumulate_all(slot_a, 0)

    @pl.when(nfull > 0)
    def _():
        put_all(slot_a, 0, False)
        window_out()

        def body(j, carry):
            stage(slot_b, 2 * j + 1, False, slot_a, 2 * j)
            stage(slot_a, 2 * j + 2, False, slot_b, 2 * j + 1)
            return carry

        lax.fori_loop(0, (nfull - 1) // 2, body, 0)

        @pl.when(nfull % 2 == 1)
        def _():
            stage(slot_b, nfull, True, slot_a, nfull - 1)
            accumulate_all(slot_b, nfull)

        @pl.when(nfull % 2 == 0)
        def _():
            stage(slot_b, nfull - 1, False, slot_a, nfull - 2)
            stage(slot_a, nfull, True, slot_b, nfull - 1)
            accumulate_all(slot_a, nfull)

    o_s = acc_sc[0:HEAD_DIM] * (1.0 / jnp.maximum(acc_sc[HEAD_DIM:HEAD_DIM + 1], TINY))

    o_w = ow_sc[0:HEAD_DIM] * (1.0 / jnp.maximum(ow_sc[HEAD_DIM:HEAD_DIM + 1], TINY))

    gate = gatet_ref[0, 0]
    o_c = oct_ref[0, 0]
    outs = []
    for r in range(NSA_REP):
        cs = slice(r * tq, (r + 1) * tq)
        outs.append(gate[3 * r:3 * r + 1] * o_c[:, cs] + gate[3 * r + 1:3 * r + 2] * o_s[:, cs]
                    + gate[3 * r + 2:3 * r + 3] * o_w[:, cs])
    pairs = [jnp.concatenate(outs[r:r + 2], axis=0).T for r in range(0, NSA_REP, 2)]
    o_ref[0] = jnp.concatenate(pairs, axis=1).astype(o_ref.dtype)


def _sel_win(qt, ket, vst, kw, vwt, selt, oct, gatet, tk):
    bsz, _, hd, s = qt.shape
    g = ket.shape[1]
    nb = selt.shape[2]
    chunk = ket.shape[3] - hd
    nchunk = nb // chunk
    nvar = chunk * SLC_BLOCK // tk
    tq = min(Q_TILE, s)
    cols = NSA_REP * tq
    resident = lambda shape: pl.BlockSpec(shape, lambda b, gg, i: (b, gg, 0, 0),
                                          pipeline_mode=pl.Buffered(1))
    return pl.pallas_call(
        functools.partial(_sel_win_kernel, tq=tq, tk=tk, chunk=chunk, nvar=nvar, nchunk=nchunk,
                          wlen=WINDOW),
        grid=(bsz, g, s // tq),
        in_specs=[
            pl.BlockSpec((1, NSA_REP, hd, tq), lambda b, gg, i: (b, gg, 0, i)),
            resident((1, 1, s, chunk + hd)),
            resident((1, 1, hd + ONES_ROWS, s)),
            resident((1, 1, s, hd)),
            resident((1, 1, hd + ONES_ROWS, s)),
            pl.BlockSpec((1, 1, nb, tq), lambda b, gg, i: (b, gg, 0, i)),
            pl.BlockSpec((1, 1, hd, cols), lambda b, gg, i: (b, gg, 0, i)),
            pl.BlockSpec((1, 1, GATE_ROWS, tq), lambda b, gg, i: (b, gg, 0, i)),
        ],
        out_specs=pl.BlockSpec((1, tq, NSA_REP * hd), lambda b, gg, i: (b, i, gg)),
        out_shape=jax.ShapeDtypeStruct((bsz, s, g * NSA_REP * hd), BF16),
        scratch_shapes=[
            pltpu.VMEM((nchunk, chunk + hd, cols), BF16),
            pltpu.VMEM((1, cols), F32),
            pltpu.VMEM((hd + ONES_ROWS, cols), F32),
            pltpu.VMEM((tk, cols), F32),
            pltpu.VMEM((tk, cols), F32),
            pltpu.VMEM((1, cols), F32),
            pltpu.VMEM((1, cols), F32),
            pltpu.VMEM((WINDOW + tq, cols), F32),
            pltpu.VMEM((1, cols), F32),
            pltpu.VMEM((hd + ONES_ROWS, cols), F32),
        ],
        compiler_params=_params(("parallel", "parallel", "arbitrary")),
        name="sel_win",
    )(qt, ket, vst, kw, vwt, selt, oct, gatet)


def _retention_kernel(q_ref, k_ref, v_ref, gb_ref, cos_ref, sin_ref, dmat_ref, qdec_ref, kdec_ref,
                      cdec_ref, gng_ref, gnb_ref, o_ref, state_sc, *, nchunk, c):
    @pl.when(pl.program_id(2) == 0)
    def _():
        state_sc[...] = jnp.zeros_like(state_sc)

    dmat = dmat_ref[0]
    qdec = qdec_ref[0]
    kdec = kdec_ref[0]
    cdec = cdec_ref[0]
    kscale = RET_DIM ** -0.5
    half = RET_DIM // 2
    for n in range(nchunk):
        rows = pl.ds(n * c, c)
        cos = cos_ref[rows, :]
        sin = sin_ref[rows, :]
        qf = q_ref[0, rows, :].astype(F32)
        kf = k_ref[0, rows, :].astype(F32)
        qr = qf * cos + pltpu.roll(qf, half, 1) * sin
        kr = (kf * cos + pltpu.roll(kf, half, 1) * sin) * kscale
        v = v_ref[0, rows, :]
        inner = _dot_nt(qr.astype(BF16), kr.astype(BF16)) * dmat
        state = state_sc[...]
        y = _dot(inner.astype(BF16), v) + _dot((qr * qdec).astype(BF16), state.astype(BF16))
        state_sc[...] = state * cdec + _dot_tn((kr * kdec).astype(BF16), v)
        mu = jnp.mean(y, axis=-1, keepdims=True)
        d = y - mu
        var = jnp.mean(d * d, axis=-1, keepdims=True)
        yn = d * lax.rsqrt(var + LN_EPS) * gng_ref[...] + gnb_ref[...]
        gb = gb_ref[0, rows, :].astype(F32)
        o_ref[0, rows, :] = ((gb * _sigmoid(gb)) * yn).astype(o_ref.dtype)


def _retention(ret, cos2, sin2, dmat, qdec, kdec, cdec, gng, gnb):
    bsz, s, _ = ret.shape
    h = RET_HEADS
    c = RET_CHUNK
    tc = min(RET_TILE, s)
    spec = lambda off: pl.BlockSpec((1, tc, RET_DIM), lambda b, hh, j, off=off: (b, j, off + hh))
    hspec = lambda shp: pl.BlockSpec((1,) + shp, lambda b, hh, j: (hh, 0, 0))
    return pl.pallas_call(
        functools.partial(_retention_kernel, nchunk=tc // c, c=c),
        grid=(bsz, h, s // tc),
        in_specs=[
            spec(0), spec(h), spec(2 * h), spec(3 * h),
            pl.BlockSpec((tc, RET_DIM), lambda b, hh, j: (j, 0)),
            pl.BlockSpec((tc, RET_DIM), lambda b, hh, j: (j, 0)),
            hspec((c, c)), hspec((c, RET_DIM)), hspec((c, RET_DIM)), hspec((1, RET_DIM)),
            pl.BlockSpec((1, RET_DIM), lambda b, hh, j: (0, hh)),
            pl.BlockSpec((1, RET_DIM), lambda b, hh, j: (0, hh)),
        ],
        out_specs=pl.BlockSpec((1, tc, RET_DIM), lambda b, hh, j: (b, j, hh)),
        out_shape=jax.ShapeDtypeStruct((bsz, s, h * RET_DIM), BF16),
        scratch_shapes=[pltpu.VMEM((RET_DIM, RET_DIM), F32)],
        compiler_params=_params(("parallel", "parallel", "arbitrary")),
        name="retention",
    )(ret, ret, ret, ret, cos2, sin2, dmat, qdec, kdec, cdec, gng, gnb)


def _merge_kernel(h_ref, oa_ref, ob_ref, wga_ref, wgb_ref, wpa_ref, wpb_ref, o_ref):
    hb = h_ref[...]
    ga = _sigmoid(_dot(hb, wga_ref[...]))
    gb = _sigmoid(_dot(hb, wgb_ref[...]))
    merged = ga * _dot(oa_ref[...], wpa_ref[...]) + gb * _dot(ob_ref[...], wpb_ref[...])
    o_ref[...] = merged.astype(o_ref.dtype)


def _merge(hb, oa, ob, wga, wgb, wpa, wpb):
    n, d = hb.shape
    bm = min(PROJ_ROW_TILE, n)
    bn = min(COL_TILE, d)
    ka, kb = oa.shape[1], ob.shape[1]
    return pl.pallas_call(
        _merge_kernel,
        grid=(n // bm, d // bn),
        in_specs=[
            pl.BlockSpec((bm, d), lambda i, j: (i, 0)),
            pl.BlockSpec((bm, ka), lambda i, j: (i, 0)),
            pl.BlockSpec((bm, kb), lambda i, j: (i, 0)),
            pl.BlockSpec((d, bn), lambda i, j: (0, j)),
            pl.BlockSpec((d, bn), lambda i, j: (0, j)),
            pl.BlockSpec((ka, bn), lambda i, j: (0, j)),
            pl.BlockSpec((kb, bn), lambda i, j: (0, j)),
        ],
        out_specs=pl.BlockSpec((bm, bn), lambda i, j: (i, j)),
        out_shape=jax.ShapeDtypeStruct((n, d), BF16),
        compiler_params=_params(("parallel", "arbitrary")),
        name="merge",
    )(hb, oa, ob, wga, wgb, wpa, wpb)


def _proj_ln_kernel(x_ref, m_ref, w_ref, g_ref, b_ref, o_ref, *, alpha):
    y = alpha * x_ref[...] + _dot(m_ref[...], w_ref[...])
    o_ref[...] = _layer_norm_rows(y, g_ref[...], b_ref[...])


def _proj_ln(x, m, w, g, b, alpha):
    n, d = x.shape
    bm = min(ROW_TILE, n)
    return pl.pallas_call(
        functools.partial(_proj_ln_kernel, alpha=alpha),
        grid=(n // bm,),
        in_specs=[
            pl.BlockSpec((bm, d), lambda i: (i, 0)),
            pl.BlockSpec((bm, d), lambda i: (i, 0)),
            pl.BlockSpec((d, d), lambda i: (0, 0)),
            pl.BlockSpec((1, d), lambda i: (0, 0)),
            pl.BlockSpec((1, d), lambda i: (0, 0)),
        ],
        out_specs=pl.BlockSpec((bm, d), lambda i: (i, 0)),
        out_shape=jax.ShapeDtypeStruct((n, d), F32),
        compiler_params=_params(("parallel",)),
        name="proj_ln",
    )(x, m, w, g, b)


def _overlap_matrix(nh, nb):
    c0 = np.arange(nh) * CMP_STRIDE
    c1 = c0 + CMP_BLOCK
    s0 = np.arange(nb) * SLC_BLOCK
    s1 = s0 + SLC_BLOCK
    ov = (c0[:, None] < s1[None, :]) & (c1[:, None] > s0[None, :])
    ov[nh - 1, :] = False
    return jnp.asarray(ov.T, BF16)


def _expand_matrix(nb, s):
    chunk = min(LANE, nb)
    blk_in_chunk = (np.arange(s) // SLC_BLOCK) % chunk
    return jnp.asarray(blk_in_chunk[:, None] == np.arange(chunk)[None, :], BF16)


def _retention_tables(s):
    h, c, d = RET_HEADS, RET_CHUNK, RET_DIM
    inv = ROPE_BASE ** (-jnp.arange(0, d, 2, dtype=F32) / d)
    ang = jnp.arange(s)[:, None].astype(F32) * inv[None, :]
    cos, sin = jnp.cos(ang), jnp.sin(ang)
    cos2 = jnp.concatenate([cos, cos], -1)
    sin2 = jnp.concatenate([-sin, sin], -1)
    log_g = jnp.log1p(-jnp.exp2(-5.0 - jnp.arange(h, dtype=F32)))
    i = jnp.arange(c, dtype=F32)
    diff = i[:, None] - i[None, :]
    dmat = jnp.where(diff >= 0, jnp.exp(jnp.maximum(diff, 0.0)[None] * log_g[:, None, None]), 0.0)
    kdec = jnp.exp((c - 1 - i)[None, :] * log_g[:, None])
    qdec = jnp.exp((i + 1)[None, :] * log_g[:, None])
    cdec = jnp.exp(c * log_g)
    bc = lambda t: jnp.broadcast_to(t[:, :, None], (h, c, d))
    return cos2, sin2, dmat, bc(qdec), bc(kdec), jnp.broadcast_to(cdec[:, None, None], (h, 1, d))


def _pad_cols(w, mult):
    pad = (-w.shape[1]) % mult
    return jnp.pad(w, ((0, 0), (0, pad))) if pad else w


def _mixer(hf, hb, bsz, s, w_in, cmp_k, cmp_v, ret_gn_g, ret_gn_b, w_merge_gate, w_proj_a, w_proj_b,
           w_o, ln_g, ln_b, alpha, tables):
    n, d = hf.shape
    g, hd = NSA_GROUPS, HEAD_DIM
    ovt, expand, ret_tabs = tables
    o_nsa = NSA_Q + 6 * NSA_KV
    w_q = w_in[:, :NSA_Q].astype(BF16)
    w_kv = w_in[:, NSA_Q:o_nsa].astype(BF16)
    w_gate = _pad_cols(w_in[:, o_nsa:o_nsa + NSA_GATE], LANE).astype(BF16)
    w_ret = w_in[:, o_nsa + NSA_GATE:].astype(BF16)

    qt, kc_in, vc_in, ket, vst, kw, vwt = _nsa_proj(hb, w_q, w_kv, expand, bsz, s)
    ga = _matmul(hb, w_gate, "sigmoid", F32, "proj_gate")
    ret = _matmul(hb, w_ret, None, BF16, "proj_ret")

    nh = s // CMP_STRIDE
    half = CMP_STRIDE * hd

    def compress(t, prm):
        pe, w1, b1, w2 = prm
        return _compress(t.reshape(bsz, g, nh, half), pe.reshape(2, half), w1.astype(BF16),
                         b1.reshape(1, -1), w2.astype(BF16))

    kc = compress(kc_in, cmp_k)
    ones_rows = jnp.zeros((bsz, g, ONES_ROWS, nh), BF16).at[:, :, 0].set(1.0)
    vct = jnp.concatenate([jnp.transpose(compress(vc_in, cmp_v), (0, 1, 3, 2)), ones_rows], axis=2)
    oct, selt = _cmp_topk(qt, kc, vct, ovt)

    gatet = jnp.transpose(ga[:, :NSA_GATE].reshape(bsz, s, g, 3 * NSA_REP), (0, 2, 3, 1))
    gatet = jnp.pad(gatet, ((0, 0), (0, 0), (0, GATE_ROWS - 3 * NSA_REP), (0, 0)))
    o_a = _sel_win(qt, ket, vst, kw, vwt, selt, oct, gatet, min(K_TILE, s))

    o_b = _retention(ret.reshape(bsz, s, 4 * RET_W), *ret_tabs,
                     ret_gn_g.reshape(1, -1), ret_gn_b.reshape(1, -1))

    merged = _merge(hb, o_a.reshape(n, NSA_Q), o_b.reshape(n, RET_W),
                    w_merge_gate[:, :d].astype(BF16), w_merge_gate[:, d:].astype(BF16),
                    w_proj_a.astype(BF16), w_proj_b.astype(BF16))
    return _proj_ln(hf, merged, w_o.astype(BF16), ln_g.reshape(1, -1), ln_b.reshape(1, -1), alpha)


def kernel(x, ffn1_w_gate, ffn1_w_up, ffn1_w_down, ln1_g, ln1_b, w_in, cmp_k_pe, cmp_k_w1, cmp_k_b1,
           cmp_k_w2, cmp_v_pe, cmp_v_w1, cmp_v_b1, cmp_v_w2, ret_gn_g, ret_gn_b, w_merge_gate, w_proj_a,
           w_proj_b, w_o, ln2_g, ln2_b, ffn2_w_gate, ffn2_w_up, ffn2_w_down, ln3_g, ln3_b):
    bsz, s, d = x.shape
    depth = ffn1_w_gate.shape[0]
    f = ffn1_w_gate.shape[2]
    alpha = (2 * depth) ** 0.25
    fpad = (-f) % min(FF_TILE, f)
    nb = s // SLC_BLOCK
    tables = (_overlap_matrix(s // CMP_STRIDE, nb), _expand_matrix(nb, s), _retention_tables(s))

    def ffn_weights(wg, wu, wd):
        return (jnp.pad(wg.astype(BF16), ((0, 0), (0, fpad))), jnp.pad(wu.astype(BF16), ((0, 0), (0, fpad))),
                jnp.pad(wd.astype(BF16), ((0, fpad), (0, 0))))

    row = lambda t: t.reshape(1, -1)
    xf = x.reshape(bsz * s, d)
    for l in range(depth):
        hf, hb = _ffn_ln(xf, *ffn_weights(ffn1_w_gate[l], ffn1_w_up[l], ffn1_w_down[l]),
                         row(ln1_g[l]), row(ln1_b[l]), alpha, True)
        xf = _mixer(hf, hb, bsz, s, w_in[l],
                    (cmp_k_pe[l], cmp_k_w1[l], cmp_k_b1[l], cmp_k_w2[l]),
                    (cmp_v_pe[l], cmp_v_w1[l], cmp_v_b1[l], cmp_v_w2[l]),
                    ret_gn_g[l], ret_gn_b[l], w_merge_gate[l], w_proj_a[l], w_proj_b[l], w_o[l],
                    ln2_g[l], ln2_b[l], alpha, tables)
        xf, _ = _ffn_ln(xf, *ffn_weights(ffn2_w_gate[l], ffn2_w_up[l], ffn2_w_down[l]),
                        row(ln3_g[l]), row(ln3_b[l]), alpha, False)
    return xf.reshape(bsz, s, d)
```

```python
import functools
import math

import jax
import jax.numpy as jnp
import numpy as np
from jax import lax
from jax.experimental import pallas as pl
from jax.experimental.pallas import tpu as pltpu

NSA_HEADS = 16
NSA_GROUPS = 4
NSA_REP = NSA_HEADS // NSA_GROUPS
HEAD_DIM = 64
CMP_BLOCK = 32
CMP_STRIDE = 16
SLC_BLOCK = 64
SLC_TOPN = 16
N_FORCED = 3
CAUSAL_PARTS = 4
COL_PARTS = 2
SEL_CHUNK = 64
WINDOW = 512
RET_HEADS = 8
RET_DIM = 128
RET_CHUNK = 128
ROPE_BASE = 10000.0
LN_EPS = 1e-5
NEG = -1e30
TINY = 1e-30
NSA_Q = NSA_HEADS * HEAD_DIM
NSA_KV = NSA_GROUPS * HEAD_DIM
NSA_GATE = 3 * NSA_HEADS
GATE_ROWS = 16
ONES_ROWS = 16
QK_SCALE = HEAD_DIM ** -0.5 * math.log2(math.e)
RET_W = RET_HEADS * RET_DIM

LANE = 128
SUBLANE = 8
VMEM_LIMIT = 52 * 1024 * 1024
ROW_TILE = 512
FF_TILE = 512
COL_TILE = 512
PROJ_ROW_TILE = 1024
PROJ_COL_TILE = 1024
Q_TILE = 256
K_TILE = 512
RET_TILE = 1024

BF16 = jnp.bfloat16
F32 = jnp.float32


def _dot(a, b):
    return jnp.dot(a, b, preferred_element_type=F32)


def _dot_nt(a, b):
    return lax.dot_general(a, b, (((1,), (1,)), ((), ())), preferred_element_type=F32)


def _dot_tn(a, b):
    return lax.dot_general(a, b, (((0,), (0,)), ((), ())), preferred_element_type=F32)


def _sigmoid(x):
    return 1.0 / (1.0 + jnp.exp(-x))


def _layer_norm_rows(y, g, b):
    mu = jnp.mean(y, axis=-1, keepdims=True)
    d = y - mu
    var = jnp.mean(d * d, axis=-1, keepdims=True)
    return d * lax.rsqrt(var + LN_EPS) * g + b


def _params(sem):
    return pltpu.CompilerParams(dimension_semantics=sem, vmem_limit_bytes=VMEM_LIMIT)


def _ffn_ln_kernel(x_ref, wg_ref, wu_ref, wd_ref, g_ref, b_ref, *rest, alpha, nf, with_bf16):
    if with_bf16:
        o_ref, ob_ref, xb_sc, acc_sc = rest
    else:
        o_ref, xb_sc, acc_sc = rest
        ob_ref = None
    f = pl.program_id(1)

    @pl.when(f == 0)
    def _():
        xb_sc[...] = x_ref[...].astype(BF16)
        acc_sc[...] = jnp.zeros_like(acc_sc)

    xb = xb_sc[...]
    a = _dot(xb, wg_ref[...])
    u = _dot(xb, wu_ref[...])
    h = (a * _sigmoid(a)) * u
    acc_sc[...] += _dot(h.astype(BF16), wd_ref[...])

    @pl.when(f == nf - 1)
    def _():
        y = alpha * x_ref[...] + 0.5 * acc_sc[...]
        out = _layer_norm_rows(y, g_ref[...], b_ref[...])
        o_ref[...] = out
        if with_bf16:
            ob_ref[...] = out.astype(BF16)


def _ffn_ln(x, wg, wu, wd, g, b, alpha, with_bf16):
    n, d = x.shape
    fp = wg.shape[1]
    bm = min(ROW_TILE, n)
    bf = min(FF_TILE, fp)
    nf = fp // bf
    out_shape = [jax.ShapeDtypeStruct((n, d), F32)]
    out_specs = [pl.BlockSpec((bm, d), lambda i, f: (i, 0))]
    if with_bf16:
        out_shape.append(jax.ShapeDtypeStruct((n, d), BF16))
        out_specs.append(pl.BlockSpec((bm, d), lambda i, f: (i, 0)))
    res = pl.pallas_call(
        functools.partial(_ffn_ln_kernel, alpha=alpha, nf=nf, with_bf16=with_bf16),
        grid=(n // bm, nf),
        in_specs=[
            pl.BlockSpec((bm, d), lambda i, f: (i, 0)),
            pl.BlockSpec((d, bf), lambda i, f: (0, f)),
            pl.BlockSpec((d, bf), lambda i, f: (0, f)),
            pl.BlockSpec((bf, d), lambda i, f: (f, 0)),
            pl.BlockSpec((1, d), lambda i, f: (0, 0)),
            pl.BlockSpec((1, d), lambda i, f: (0, 0)),
        ],
        out_specs=out_specs,
        out_shape=out_shape,
        scratch_shapes=[pltpu.VMEM((bm, d), BF16), pltpu.VMEM((bm, d), F32)],
        compiler_params=_params(("parallel", "arbitrary")),
        name="ffn_ln",
    )(x, wg, wu, wd, g, b)
    return res if with_bf16 else (res[0], None)


def _mm_kernel(x_ref, w_ref, o_ref, *, act, scale):
    y = _dot(x_ref[...], w_ref[...])
    if scale is not None:
        y = y * scale
    if act == "sigmoid":
        y = _sigmoid(y)
    o_ref[...] = y.astype(o_ref.dtype)


def _matmul(x, w, act, out_dtype, name, scale=None):
    n, k = x.shape
    nout = w.shape[1]
    bm = min(PROJ_ROW_TILE, n)
    bn = next(c for c in (PROJ_COL_TILE, COL_TILE, 2 * LANE, LANE, nout) if nout % c == 0)
    assert n % bm == 0
    return pl.pallas_call(
        functools.partial(_mm_kernel, act=act, scale=scale),
        grid=(n // bm, nout // bn),
        in_specs=[
            pl.BlockSpec((bm, k), lambda i, j: (i, 0)),
            pl.BlockSpec((k, bn), lambda i, j: (0, j)),
        ],
        out_specs=pl.BlockSpec((bm, bn), lambda i, j: (i, j)),
        out_shape=jax.ShapeDtypeStruct((n, nout), out_dtype),
        compiler_params=_params(("parallel", "arbitrary")),
        name=name,
    )(x, w)


def _nsa_proj_kernel(x_ref, wq_ref, wkv_ref, exp_ref, qt_ref, kc_ref, vc_ref, ket_ref, vst_ref, kw_ref,
                     vwt_ref, *, chunk):
    x = x_ref[...]
    hd, g = HEAD_DIM, NSA_GROUPS
    pair = 2 * hd
    yq = _dot(x, wq_ref[...]) * QK_SCALE
    for p in range(NSA_HEADS // 2):
        t = yq[:, p * pair:(p + 1) * pair].T.astype(qt_ref.dtype)
        qt_ref[0, 2 * p] = t[0:hd]
        qt_ref[0, 2 * p + 1] = t[hd:pair]
    ykv = _dot(x, wkv_ref[...])
    col = lambda kind, gg: (kind * g + gg) * hd
    ones = jnp.where(lax.broadcasted_iota(jnp.int32, (ONES_ROWS, x.shape[0]), 0) == 0, 1.0, 0.0)
    for gg in range(g):
        rows = lambda kind: ykv[:, col(kind, gg):col(kind, gg) + hd]
        kc_ref[0, gg] = rows(0).astype(kc_ref.dtype)
        vc_ref[0, gg] = rows(1).astype(vc_ref.dtype)
        ket_ref[0, gg, :, 0:chunk] = exp_ref[...]
        ket_ref[0, gg, :, chunk:chunk + hd] = rows(2).astype(ket_ref.dtype)
        kw_ref[0, gg] = rows(4).astype(kw_ref.dtype)
    for kind, out_ref in ((3, vst_ref), (5, vwt_ref)):
        for gg in range(0, g, 2):
            t = ykv[:, col(kind, gg):col(kind, gg) + pair].T.astype(out_ref.dtype)
            for k in range(2):
                out_ref[0, gg + k, 0:hd] = t[k * hd:(k + 1) * hd]
                out_ref[0, gg + k, hd:hd + ONES_ROWS] = ones.astype(out_ref.dtype)


def _nsa_proj(hb, w_q, w_kv, expand, bsz, s):
    n, d = hb.shape
    g, hd = NSA_GROUPS, HEAD_DIM
    chunk = expand.shape[1]
    bm = min(ROW_TILE, s)
    nj = s // bm
    rows_spec = lambda w: pl.BlockSpec((1, g, bm, w), lambda b, j: (b, 0, j, 0))
    cols_spec = lambda h, r: pl.BlockSpec((1, h, r, bm), lambda b, j: (b, 0, 0, j))
    full = lambda shape: pl.BlockSpec(shape, lambda b, j: (0, 0), pipeline_mode=pl.Buffered(1))
    rows_shape = lambda w: jax.ShapeDtypeStruct((bsz, g, s, w), BF16)
    cols_shape = lambda h, r: jax.ShapeDtypeStruct((bsz, h, r, s), BF16)
    return pl.pallas_call(
        functools.partial(_nsa_proj_kernel, chunk=chunk),
        grid=(bsz, nj),
        in_specs=[
            pl.BlockSpec((bm, d), lambda b, j: (b * nj + j, 0)),
            full(w_q.shape), full(w_kv.shape),
            pl.BlockSpec((bm, chunk), lambda b, j: (j, 0)),
        ],
        out_specs=[cols_spec(NSA_HEADS, hd), rows_spec(hd), rows_spec(hd), rows_spec(chunk + hd),
                   cols_spec(g, hd + ONES_ROWS), rows_spec(hd), cols_spec(g, hd + ONES_ROWS)],
        out_shape=[cols_shape(NSA_HEADS, hd), rows_shape(hd), rows_shape(hd), rows_shape(chunk + hd),
                   cols_shape(g, hd + ONES_ROWS), rows_shape(hd), cols_shape(g, hd + ONES_ROWS)],
        compiler_params=_params(("parallel", "arbitrary")),
        name="nsa_proj",
    )(hb, w_q, w_kv, expand)


def _compress_kernel(x_ref, pe_ref, w1_ref, b1_ref, w2_ref, o_ref, *, nh, half):
    x = x_ref[0, 0].astype(F32)
    xa = (x + pe_ref[0:1, :]).astype(BF16)
    xb = (x + pe_ref[1:2, :]).astype(BF16)
    ha = _dot(xa, w1_ref[0:half, :])
    hb = _dot(xb, w1_ref[half:2 * half, :])
    hid = ha + pltpu.roll(hb, nh - 1, 0) + b1_ref[...]
    c = math.sqrt(2.0 / math.pi)
    act = 0.5 * hid * (1.0 + jnp.tanh(c * (hid + 0.044715 * (hid * hid * hid))))
    o_ref[0, 0] = _dot(act.astype(BF16), w2_ref[...]).astype(o_ref.dtype)


def _compress(x, pe2, w1, b1, w2):
    bsz, g, nh, half = x.shape
    hid = w1.shape[1]
    hd = w2.shape[1]
    return pl.pallas_call(
        functools.partial(_compress_kernel, nh=nh, half=half),
        grid=(bsz, g),
        in_specs=[
            pl.BlockSpec((1, 1, nh, half), lambda b, gg: (b, gg, 0, 0)),
            pl.BlockSpec((2, half), lambda b, gg: (0, 0)),
            pl.BlockSpec((2 * half, hid), lambda b, gg: (0, 0)),
            pl.BlockSpec((1, hid), lambda b, gg: (0, 0)),
            pl.BlockSpec((hid, hd), lambda b, gg: (0, 0)),
        ],
        out_specs=pl.BlockSpec((1, 1, nh, hd), lambda b, gg: (b, gg, 0, 0)),
        out_shape=jax.ShapeDtypeStruct((bsz, g, nh, hd), BF16),
        compiler_params=_params(("parallel", "parallel")),
        name="compress",
    )(x, pe2, w1, b1, w2)


def _stack_heads(qt_ref):
    return jnp.concatenate([qt_ref[0, r] for r in range(NSA_REP)], axis=1)


def _per_head(x):
    return jnp.tile(x, (1, NSA_REP))


def _cmp_topk_kernel(qt_ref, kc_ref, vct_ref, ovt_ref, oct_ref, selt_ref, *, tq, nh, nb, n_sel, nq):
    i = pl.program_id(2)
    for part in range(CAUSAL_PARTS):
        lo, hi = part * nq // CAUSAL_PARTS, (part + 1) * nq // CAUSAL_PARTS

        @pl.when(jnp.logical_and(i >= lo, i < hi))
        def _(part=part):
            _cmp_topk_part(qt_ref, kc_ref, vct_ref, ovt_ref, oct_ref, selt_ref, i * tq, tq=tq,
                           nh=(part + 1) * nh // CAUSAL_PARTS, nb=(part + 1) * nb // CAUSAL_PARTS,
                           nb_all=nb, n_sel=n_sel)


def _cmp_topk_part(qt_ref, kc_ref, vct_ref, ovt_ref, oct_ref, selt_ref, t0, *, tq, nh, nb, nb_all, n_sel):
    cend = lax.broadcasted_iota(jnp.int32, (nh, tq), 0) * CMP_STRIDE + (CMP_BLOCK - 1)
    tpos = t0 + lax.broadcasted_iota(jnp.int32, (nh, tq), 1)
    bias = jnp.where(cend <= tpos, 0.0, NEG)
    seen = jnp.where(t0 + lax.broadcasted_iota(jnp.int32, (1, tq), 1) >= CMP_BLOCK - 1, 1.0, 0.0)
    qt = _stack_heads(qt_ref)
    s = _dot(kc_ref[0, 0, 0:nh, :], qt) + _per_head(bias)
    e = jnp.exp2(s - jnp.max(s, axis=0, keepdims=True))
    ea = _dot(vct_ref[0, 0, :, 0:nh], e.astype(BF16))
    inv = _per_head(seen) / jnp.maximum(ea[HEAD_DIM:HEAD_DIM + 1], TINY)
    oct_ref[0, 0] = ea[0:HEAD_DIM] * inv
    p = e * inv
    psum = p[:, 0:tq]
    for r in range(1, NSA_REP):
        psum = psum + p[:, r * tq:(r + 1) * tq]
    imp = _dot(ovt_ref[0:nb, 0:nh], psum.astype(BF16))
    blk = lax.broadcasted_iota(jnp.int32, (nb, tq), 0).astype(F32)
    cur = ((t0 + lax.broadcasted_iota(jnp.int32, (nb, tq), 1)) // SLC_BLOCK).astype(F32)
    forced = (blk == 0.0) | (blk == cur) | (blk == cur - 1.0)
    x = jnp.where(forced, -jnp.inf, jnp.where(blk <= cur, imp, NEG))
    for _ in range(n_sel - N_FORCED):
        mx = jnp.max(x, axis=0, keepdims=True)
        idx = jnp.min(jnp.where(x == mx, blk, float(nb)), axis=0, keepdims=True)
        x = jnp.where(blk == idx, -jnp.inf, x)
    selb = jnp.where(x == -jnp.inf, 0.0, NEG)
    selt_ref[0, 0, 0:nb, :] = selb.astype(selt_ref.dtype)
    if nb < nb_all:
        selt_ref[0, 0, nb:nb_all, :] = jnp.full((nb_all - nb, tq), NEG, selt_ref.dtype)


def _cmp_topk(qt, kc, vct, ovt):
    bsz, _, hd, s = qt.shape
    g, nh = kc.shape[1], kc.shape[2]
    nb = ovt.shape[0]
    assert nb >= SLC_TOPN and nh % CAUSAL_PARTS == 0 and nb % CAUSAL_PARTS == 0
    tq = min(Q_TILE, s)
    cols = NSA_REP * tq
    return pl.pallas_call(
        functools.partial(_cmp_topk_kernel, tq=tq, nh=nh, nb=nb, n_sel=SLC_TOPN, nq=s // tq),
        grid=(bsz, g, s // tq),
        in_specs=[
            pl.BlockSpec((1, NSA_REP, hd, tq), lambda b, gg, i: (b, gg, 0, i)),
            pl.BlockSpec((1, 1, nh, hd), lambda b, gg, i: (b, gg, 0, 0)),
            pl.BlockSpec((1, 1, hd + ONES_ROWS, nh), lambda b, gg, i: (b, gg, 0, 0)),
            pl.BlockSpec((nb, nh), lambda b, gg, i: (0, 0)),
        ],
        out_specs=[
            pl.BlockSpec((1, 1, hd, cols), lambda b, gg, i: (b, gg, 0, i)),
            pl.BlockSpec((1, 1, nb, tq), lambda b, gg, i: (b, gg, 0, i)),
        ],
        out_shape=[
            jax.ShapeDtypeStruct((bsz, g, hd, NSA_REP * s), F32),
            jax.ShapeDtypeStruct((bsz, g, nb, s), BF16),
        ],
        compiler_params=_params(("parallel", "parallel", "arbitrary")),
        name="cmp_topk",
    )(qt, kc, vct, ovt)


def _sel_win_kernel(qt_ref, ket_ref, vst_ref, kw_ref, vwt_ref, selt_ref, oct_ref, gatet_ref,
                    o_ref, rhs_sc, m_sc, acc_sc, sa_sc, sb_sc, ma_sc, mb_sc, sw_sc, mw_sc, ow_sc,
                    *, tq, tk, chunk, nvar, nchunk, wlen):
    t0 = pl.program_id(2) * tq
    qt = _stack_heads(qt_ref)
    for c in range(nchunk):
        selt = selt_ref[0, 0, c * chunk:(c + 1) * chunk, :]
        rhs_sc[c] = jnp.concatenate([_per_head(selt), qt], axis=0)

    m_sc[...] = jnp.full_like(m_sc, NEG)
    acc_sc[...] = jnp.zeros_like(acc_sc)

    cols = NSA_REP * tq
    part_cols = [slice(c * cols // COL_PARTS, (c + 1) * cols // COL_PARTS) for c in range(COL_PARTS)]

    def put_scores(slot, kt, causal, cs):
        s_ref, mx_ref = slot
        k0 = pl.multiple_of(kt * tk, tk)
        s = _dot(ket_ref[0, 0, pl.ds(k0, tk), :], rhs_sc[kt // nvar, :, cs])
        if causal:
            kpos = k0 + lax.broadcasted_iota(jnp.int32, (tk, tq), 0)
            tcol = t0 + lax.broadcasted_iota(jnp.int32, (tk, tq), 1)
            s = s + jnp.tile(jnp.where(kpos <= tcol, 0.0, NEG), (1, NSA_REP // COL_PARTS))
        s_ref[:, cs] = s
        mx_ref[:, cs] = jnp.max(s, axis=0, keepdims=True)

    def accumulate(slot, kt, cs):
        s_ref, mx_ref = slot
        k0 = pl.multiple_of(kt * tk, tk)
        m_old = m_sc[:, cs]
        m_new = jnp.maximum(m_old, mx_ref[:, cs])
        alpha = jnp.exp2(m_old - m_new)
        p = jnp.exp2(s_ref[:, cs] - m_new)
        acc_sc[:, cs] = alpha * acc_sc[:, cs] + _dot(vst_ref[0, 0, :, pl.ds(k0, tk)], p.astype(BF16))
        m_sc[:, cs] = m_new

    def put_all(slot, kt, causal):
        for cs in part_cols:
            put_scores(slot, kt, causal, cs)

    def accumulate_all(slot, kt):
        for cs in part_cols:
            accumulate(slot, kt, cs)

    def stage(dst, kt_new, causal, src, kt_old):
        for cs in part_cols:
            put_scores(dst, kt_new, causal, cs)
            accumulate(src, kt_old, cs)

    slot_a, slot_b = (sa_sc, ma_sc), (sb_sc, mb_sc)
    nfull = (t0 + tq - 1) // tk

    wk = wlen + tq
    w0 = pl.multiple_of(jnp.maximum(t0 - wlen, 0), tq)
    diff = (t0 + lax.broadcasted_iota(jnp.int32, (wk, tq), 1)) - (
        w0 + lax.broadcasted_iota(jnp.int32, (wk, tq), 0))
    wbias = jnp.where(jnp.logical_and(diff >= 0, diff < wlen), 0.0, NEG)
    sw = _dot(kw_ref[0, 0, pl.ds(w0, wk), :], qt) + _per_head(wbias)
    sw_sc[...] = sw
    mw_sc[...] = jnp.max(sw, axis=0, keepdims=True)

    def window_out():
        ew = jnp.exp2(sw_sc[...] - mw_sc[...])
        ow_sc[...] = _dot(vwt_ref[0, 0, :, pl.ds(w0, wk)], ew.astype(BF16))

    @pl.when(nfull == 0)
    def _():
        put_all(slot_a, 0, True)
        window_out()
        accumulate_all(slot_a, 0)

    @pl.when(nfull > 0)
    def _():
        put_all(slot_a, 0, False)
        window_out()

        def body(j, carry):
            stage(slot_b, 2 * j + 1, False, slot_a, 2 * j)
            stage(slot_a, 2 * j + 2, False, slot_b, 2 * j + 1)
            return carry

        lax.fori_loop(0, (nfull - 1) // 2, body, 0)

        @pl.when(nfull % 2 == 1)
        def _():
            stage(slot_b, nfull, True, slot_a, nfull - 1)
            accumulate_all(slot_b, nfull)

        @pl.when(nfull % 2 == 0)
        def _():
            stage(slot_b, nfull - 1, False, slot_a, nfull - 2)
            stage(slot_a, nfull, True, slot_b, nfull - 1)
            accumulate_all(slot_a, nfull)

    o_s = acc_sc[0:HEAD_DIM] * (1.0 / jnp.maximum(acc_sc[HEAD_DIM:HEAD_DIM + 1], TINY))

    o_w = ow_sc[0:HEAD_DIM] * (1.0 / jnp.maximum(ow_sc[HEAD_DIM:HEAD_DIM + 1], TINY))

    gate = gatet_ref[0, 0]
    o_c = oct_ref[0, 0]
    outs = []
    for r in range(NSA_REP):
        cs = slice(r * tq, (r + 1) * tq)
        outs.append(gate[3 * r:3 * r + 1] * o_c[:, cs] + gate[3 * r + 1:3 * r + 2] * o_s[:, cs]
                    + gate[3 * r + 2:3 * r + 3] * o_w[:, cs])
    pairs = [jnp.concatenate(outs[r:r + 2], axis=0).T for r in range(0, NSA_REP, 2)]
    o_ref[0] = jnp.concatenate(pairs, axis=1).astype(o_ref.dtype)


def _sel_win(qt, ket, vst, kw, vwt, selt, oct, gatet, tk):
    bsz, _, hd, s = qt.shape
    g = ket.shape[1]
    nb = selt.shape[2]
    chunk = ket.shape[3] - hd
    nchunk = nb // chunk
    nvar = chunk * SLC_BLOCK // tk
    tq = min(Q_TILE, s)
    cols = NSA_REP * tq
    resident = lambda shape: pl.BlockSpec(shape, lambda b, gg, i: (b, gg, 0, 0),
                                          pipeline_mode=pl.Buffered(1))
    return pl.pallas_call(
        functools.partial(_sel_win_kernel, tq=tq, tk=tk, chunk=chunk, nvar=nvar, nchunk=nchunk,
                          wlen=WINDOW),
        grid=(bsz, g, s // tq),
        in_specs=[
            pl.BlockSpec((1, NSA_REP, hd, tq), lambda b, gg, i: (b, gg, 0, i)),
            resident((1, 1, s, chunk + hd)),
            resident((1, 1, hd + ONES_ROWS, s)),
            resident((1, 1, s, hd)),
            resident((1, 1, hd + ONES_ROWS, s)),
            pl.BlockSpec((1, 1, nb, tq), lambda b, gg, i: (b, gg, 0, i)),
            pl.BlockSpec((1, 1, hd, cols), lambda b, gg, i: (b, gg, 0, i)),
            pl.BlockSpec((1, 1, GATE_ROWS, tq), lambda b, gg, i: (b, gg, 0, i)),
        ],
        out_specs=pl.BlockSpec((1, tq, NSA_REP * hd), lambda b, gg, i: (b, i, gg)),
        out_shape=jax.ShapeDtypeStruct((bsz, s, g * NSA_REP * hd), BF16),
        scratch_shapes=[
            pltpu.VMEM((nchunk, chunk + hd, cols), BF16),
            pltpu.VMEM((1, cols), F32),
            pltpu.VMEM((hd + ONES_ROWS, cols), F32),
            pltpu.VMEM((tk, cols), F32),
            pltpu.VMEM((tk, cols), F32),
            pltpu.VMEM((1, cols), F32),
            pltpu.VMEM((1, cols), F32),
            pltpu.VMEM((WINDOW + tq, cols), F32),
            pltpu.VMEM((1, cols), F32),
            pltpu.VMEM((hd + ONES_ROWS, cols), F32),
        ],
        compiler_params=_params(("parallel", "parallel", "arbitrary")),
        name="sel_win",
    )(qt, ket, vst, kw, vwt, selt, oct, gatet)


def _retention_kernel(q_ref, k_ref, v_ref, gb_ref, cos_ref, sin_ref, dmat_ref, qdec_ref, kdec_ref,
                      cdec_ref, gng_ref, gnb_ref, o_ref, state_sc, *, nchunk, c):
    @pl.when(pl.program_id(2) == 0)
    def _():
        state_sc[...] = jnp.zeros_like(state_sc)

    dmat = dmat_ref[0]
    qdec = qdec_ref[0]
    kdec = kdec_ref[0]
    cdec = cdec_ref[0]
    kscale = RET_DIM ** -0.5
    half = RET_DIM // 2
    for n in range(nchunk):
        rows = pl.ds(n * c, c)
        cos = cos_ref[rows, :]
        sin = sin_ref[rows, :]
        qf = q_ref[0, rows, :].astype(F32)
        kf = k_ref[0, rows, :].astype(F32)
        qr = qf * cos + pltpu.roll(qf, half, 1) * sin
        kr = (kf * cos + pltpu.roll(kf, half, 1) * sin) * kscale
        v = v_ref[0, rows, :]
        inner = _dot_nt(qr.astype(BF16), kr.astype(BF16)) * dmat
        state = state_sc[...]
        y = _dot(inner.astype(BF16), v) + _dot((qr * qdec).astype(BF16), state.astype(BF16))
        state_sc[...] = state * cdec + _dot_tn((kr * kdec).astype(BF16), v)
        mu = jnp.mean(y, axis=-1, keepdims=True)
        d = y - mu
        var = jnp.mean(d * d, axis=-1, keepdims=True)
        yn = d * lax.rsqrt(var + LN_EPS) * gng_ref[...] + gnb_ref[...]
        gb = gb_ref[0, rows, :].astype(F32)
        o_ref[0, rows, :] = ((gb * _sigmoid(gb)) * yn).astype(o_ref.dtype)


def _retention(ret, cos2, sin2, dmat, qdec, kdec, cdec, gng, gnb):
    bsz, s, _ = ret.shape
    h = RET_HEADS
    c = RET_CHUNK
    tc = min(RET_TILE, s)
    spec = lambda off: pl.BlockSpec((1, tc, RET_DIM), lambda b, hh, j, off=off: (b, j, off + hh))
    hspec = lambda shp: pl.BlockSpec((1,) + shp, lambda b, hh, j: (hh, 0, 0))
    return pl.pallas_call(
        functools.partial(_retention_kernel, nchunk=tc // c, c=c),
        grid=(bsz, h, s // tc),
        in_specs=[
            spec(0), spec(h), spec(2 * h), spec(3 * h),
            pl.BlockSpec((tc, RET_DIM), lambda b, hh, j: (j, 0)),
            pl.BlockSpec((tc, RET_DIM), lambda b, hh, j: (j, 0)),
            hspec((c, c)), hspec((c, RET_DIM)), hspec((c, RET_DIM)), hspec((1, RET_DIM)),
            pl.BlockSpec((1, RET_DIM), lambda b, hh, j: (0, hh)),
            pl.BlockSpec((1, RET_DIM), lambda b, hh, j: (0, hh)),
        ],
        out_specs=pl.BlockSpec((1, tc, RET_DIM), lambda b, hh, j: (b, j, hh)),
        out_shape=jax.ShapeDtypeStruct((bsz, s, h * RET_DIM), BF16),
        scratch_shapes=[pltpu.VMEM((RET_DIM, RET_DIM), F32)],
        compiler_params=_params(("parallel", "parallel", "arbitrary")),
        name="retention",
    )(ret, ret, ret, ret, cos2, sin2, dmat, qdec, kdec, cdec, gng, gnb)


def _merge_kernel(h_ref, oa_ref, ob_ref, wga_ref, wgb_ref, wpa_ref, wpb_ref, o_ref):
    hb = h_ref[...]
    ga = _sigmoid(_dot(hb, wga_ref[...]))
    gb = _sigmoid(_dot(hb, wgb_ref[...]))
    merged = ga * _dot(oa_ref[...], wpa_ref[...]) + gb * _dot(ob_ref[...], wpb_ref[...])
    o_ref[...] = merged.astype(o_ref.dtype)


def _merge(hb, oa, ob, wga, wgb, wpa, wpb):
    n, d = hb.shape
    bm = min(PROJ_ROW_TILE, n)
    bn = min(COL_TILE, d)
    ka, kb = oa.shape[1], ob.shape[1]
    return pl.pallas_call(
        _merge_kernel,
        grid=(n // bm, d // bn),
        in_specs=[
            pl.BlockSpec((bm, d), lambda i, j: (i, 0)),
            pl.BlockSpec((bm, ka), lambda i, j: (i, 0)),
            pl.BlockSpec((bm, kb), lambda i, j: (i, 0)),
            pl.BlockSpec((d, bn), lambda i, j: (0, j)),
            pl.BlockSpec((d, bn), lambda i, j: (0, j)),
            pl.BlockSpec((ka, bn), lambda i, j: (0, j)),
            pl.BlockSpec((kb, bn), lambda i, j: (0, j)),
        ],
        out_specs=pl.BlockSpec((bm, bn), lambda i, j: (i, j)),
        out_shape=jax.ShapeDtypeStruct((n, d), BF16),
        compiler_params=_params(("parallel", "arbitrary")),
        name="merge",
    )(hb, oa, ob, wga, wgb, wpa, wpb)


def _proj_ln_kernel(x_ref, m_ref, w_ref, g_ref, b_ref, o_ref, *, alpha):
    y = alpha * x_ref[...] + _dot(m_ref[...], w_ref[...])
    o_ref[...] = _layer_norm_rows(y, g_ref[...], b_ref[...])


def _proj_ln(x, m, w, g, b, alpha):
    n, d = x.shape
    bm = min(ROW_TILE, n)
    return pl.pallas_call(
        functools.partial(_proj_ln_kernel, alpha=alpha),
        grid=(n // bm,),
        in_specs=[
            pl.BlockSpec((bm, d), lambda i: (i, 0)),
            pl.BlockSpec((bm, d), lambda i: (i, 0)),
            pl.BlockSpec((d, d), lambda i: (0, 0)),
            pl.BlockSpec((1, d), lambda i: (0, 0)),
            pl.BlockSpec((1, d), lambda i: (0, 0)),
        ],
        out_specs=pl.BlockSpec((bm, d), lambda i: (i, 0)),
        out_shape=jax.ShapeDtypeStruct((n, d), F32),
        compiler_params=_params(("parallel",)),
        name="proj_ln",
    )(x, m, w, g, b)


def _overlap_matrix(nh, nb):
    c0 = np.arange(nh) * CMP_STRIDE
    c1 = c0 + CMP_BLOCK
    s0 = np.arange(nb) * SLC_BLOCK
    s1 = s0 + SLC_BLOCK
    ov = (c0[:, None] < s1[None, :]) & (c1[:, None] > s0[None, :])
    ov[nh - 1, :] = False
    return jnp.asarray(ov.T, BF16)


def _expand_matrix(nb, s):
    chunk = min(SEL_CHUNK, nb)
    blk_in_chunk = (np.arange(s) // SLC_BLOCK) % chunk
    return jnp.asarray(blk_in_chunk[:, None] == np.arange(chunk)[None, :], BF16)


def _retention_tables(s):
    h, c, d = RET_HEADS, RET_CHUNK, RET_DIM
    inv = ROPE_BASE ** (-jnp.arange(0, d, 2, dtype=F32) / d)
    ang = jnp.arange(s)[:, None].astype(F32) * inv[None, :]
    cos, sin = jnp.cos(ang), jnp.sin(ang)
    cos2 = jnp.concatenate([cos, cos], -1)
    sin2 = jnp.concatenate([-sin, sin], -1)
    log_g = jnp.log1p(-jnp.exp2(-5.0 - jnp.arange(h, dtype=F32)))
    i = jnp.arange(c, dtype=F32)
    diff = i[:, None] - i[None, :]
    dmat = jnp.where(diff >= 0, jnp.exp(jnp.maximum(diff, 0.0)[None] * log_g[:, None, None]), 0.0)
    kdec = jnp.exp((c - 1 - i)[None, :] * log_g[:, None])
    qdec = jnp.exp((i + 1)[None, :] * log_g[:, None])
    cdec = jnp.exp(c * log_g)
    bc = lambda t: jnp.broadcast_to(t[:, :, None], (h, c, d))
    return cos2, sin2, dmat, bc(qdec), bc(kdec), jnp.broadcast_to(cdec[:, None, None], (h, 1, d))


def _pad_cols(w, mult):
    pad = (-w.shape[1]) % mult
    return jnp.pad(w, ((0, 0), (0, pad))) if pad else w


def _mixer(hf, hb, bsz, s, w_in, cmp_k, cmp_v, ret_gn_g, ret_gn_b, w_merge_gate, w_proj_a, w_proj_b,
           w_o, ln_g, ln_b, alpha, tables):
    n, d = hf.shape
    g, hd = NSA_GROUPS, HEAD_DIM
    ovt, expand, ret_tabs = tables
    o_nsa = NSA_Q + 6 * NSA_KV
    w_q = w_in[:, :NSA_Q].astype(BF16)
    w_kv = w_in[:, NSA_Q:o_nsa].astype(BF16)
    w_gate = _pad_cols(w_in[:, o_nsa:o_nsa + NSA_GATE], LANE).astype(BF16)
    w_ret = w_in[:, o_nsa + NSA_GATE:].astype(BF16)

    qt, kc_in, vc_in, ket, vst, kw, vwt = _nsa_proj(hb, w_q, w_kv, expand, bsz, s)
    ga = _matmul(hb, w_gate, "sigmoid", F32, "proj_gate")
    ret = _matmul(hb, w_ret, None, BF16, "proj_ret")

    nh = s // CMP_STRIDE
    half = CMP_STRIDE * hd

    def compress(t, prm):
        pe, w1, b1, w2 = prm
        return _compress(t.reshape(bsz, g, nh, half), pe.reshape(2, half), w1.astype(BF16),
                         b1.reshape(1, -1), w2.astype(BF16))

    kc = compress(kc_in, cmp_k)
    ones_rows = jnp.zeros((bsz, g, ONES_ROWS, nh), BF16).at[:, :, 0].set(1.0)
    vct = jnp.concatenate([jnp.transpose(compress(vc_in, cmp_v), (0, 1, 3, 2)), ones_rows], axis=2)
    oct, selt = _cmp_topk(qt, kc, vct, ovt)

    gatet = jnp.transpose(ga[:, :NSA_GATE].reshape(bsz, s, g, 3 * NSA_REP), (0, 2, 3, 1))
    gatet = jnp.pad(gatet, ((0, 0), (0, 0), (0, GATE_ROWS - 3 * NSA_REP), (0, 0)))
    o_a = _sel_win(qt, ket, vst, kw, vwt, selt, oct, gatet, min(K_TILE, s))

    o_b = _retention(ret.reshape(bsz, s, 4 * RET_W), *ret_tabs,
                     ret_gn_g.reshape(1, -1), ret_gn_b.reshape(1, -1))

    merged = _merge(hb, o_a.reshape(n, NSA_Q), o_b.reshape(n, RET_W),
                    w_merge_gate[:, :d].astype(BF16), w_merge_gate[:, d:].astype(BF16),
                    w_proj_a.astype(BF16), w_proj_b.astype(BF16))
    return _proj_ln(hf, merged, w_o.astype(BF16), ln_g.reshape(1, -1), ln_b.reshape(1, -1), alpha)


def kernel(x, ffn1_w_gate, ffn1_w_up, ffn1_w_down, ln1_g, ln1_b, w_in, cmp_k_pe, cmp_k_w1, cmp_k_b1,
           cmp_k_w2, cmp_v_pe, cmp_v_w1, cmp_v_b1, cmp_v_w2, ret_gn_g, ret_gn_b, w_merge_gate, w_proj_a,
           w_proj_b, w_o, ln2_g, ln2_b, ffn2_w_gate, ffn2_w_up, ffn2_w_down, ln3_g, ln3_b):
    bsz, s, d = x.shape
    depth = ffn1_w_gate.shape[0]
    f = ffn1_w_gate.shape[2]
    alpha = (2 * depth) ** 0.25
    fpad = (-f) % min(FF_TILE, f)
    nb = s // SLC_BLOCK
    tables = (_overlap_matrix(s // CMP_STRIDE, nb), _expand_matrix(nb, s), _retention_tables(s))

    def ffn_weights(wg, wu, wd):
        return (jnp.pad(wg.astype(BF16), ((0, 0), (0, fpad))), jnp.pad(wu.astype(BF16), ((0, 0), (0, fpad))),
                jnp.pad(wd.astype(BF16), ((0, fpad), (0, 0))))

    row = lambda t: t.reshape(1, -1)
    xf = x.reshape(bsz * s, d)
    for l in range(depth):
        hf, hb = _ffn_ln(xf, *ffn_weights(ffn1_w_gate[l], ffn1_w_up[l], ffn1_w_down[l]),
                         row(ln1_g[l]), row(ln1_b[l]), alpha, True)
        xf = _mixer(hf, hb, bsz, s, w_in[l],
                    (cmp_k_pe[l], cmp_k_w1[l], cmp_k_b1[l], cmp_k_w2[l]),
                    (cmp_v_pe[l], cmp_v_w1[l], cmp_v_b1[l], cmp_v_w2[l]),
                    ret_gn_g[l], ret_gn_b[l], w_merge_gate[l], w_proj_a[l], w_proj_b[l], w_o[l],
                    ln2_g[l], ln2_b[l], alpha, tables)
        xf, _ = _ffn_ln(xf, *ffn_weights(ffn2_w_gate[l], ffn2_w_up[l], ffn2_w_down[l]),
                        row(ln3_g[l]), row(ln3_b[l]), alpha, False)
    return xf.reshape(bsz, s, d)
```

```python
import functools
import math

import jax
import jax.numpy as jnp
import numpy as np
from jax import lax
from jax.experimental import pallas as pl
from jax.experimental.pallas import tpu as pltpu

NSA_HEADS = 16
NSA_GROUPS = 4
NSA_REP = NSA_HEADS // NSA_GROUPS
HEAD_DIM = 64
CMP_BLOCK = 32
CMP_STRIDE = 16
SLC_BLOCK = 64
SLC_TOPN = 16
N_FORCED = 3
CAUSAL_PARTS = 8
COL_PARTS = 2
SEL_CHUNK = 64
WINDOW = 512
RET_HEADS = 8
RET_DIM = 128
RET_CHUNK = 128
ROPE_BASE = 10000.0
LN_EPS = 1e-5
NEG = -1e30
TINY = 1e-30
NSA_Q = NSA_HEADS * HEAD_DIM
NSA_KV = NSA_GROUPS * HEAD_DIM
NSA_GATE = 3 * NSA_HEADS
GATE_ROWS = 16
ONES_ROWS = 16
QK_SCALE = HEAD_DIM ** -0.5 * math.log2(math.e)
RET_W = RET_HEADS * RET_DIM

LANE = 128
VMEM_LIMIT = 52 * 1024 * 1024
ROW_TILE = 512
FF_TILE = 512
COL_TILE = 512
PROJ_ROW_TILE = 1024
PROJ_COL_TILE = 1024
Q_TILE = 256
K_TILE = 512
RET_TILE = 1024

BF16 = jnp.bfloat16
F32 = jnp.float32


def _dot(a, b):
    return jnp.dot(a, b, preferred_element_type=F32)


def _dot_nt(a, b):
    return lax.dot_general(a, b, (((1,), (1,)), ((), ())), preferred_element_type=F32)


def _dot_tn(a, b):
    return lax.dot_general(a, b, (((0,), (0,)), ((), ())), preferred_element_type=F32)


def _sigmoid(x):
    return 1.0 / (1.0 + jnp.exp(-x))


def _layer_norm_rows(y, g, b):
    mu = jnp.mean(y, axis=-1, keepdims=True)
    d = y - mu
    var = jnp.mean(d * d, axis=-1, keepdims=True)
    return d * lax.rsqrt(var + LN_EPS) * g + b


def _params(sem):
    return pltpu.CompilerParams(dimension_semantics=sem, vmem_limit_bytes=VMEM_LIMIT)


def _ffn_ln_kernel(x_ref, wg_ref, wu_ref, wd_ref, g_ref, b_ref, *rest, alpha, nf, with_bf16):
    if with_bf16:
        o_ref, ob_ref, xb_sc, acc_sc = rest
    else:
        o_ref, xb_sc, acc_sc = rest
        ob_ref = None
    f = pl.program_id(1)

    @pl.when(f == 0)
    def _():
        xb_sc[...] = x_ref[...].astype(BF16)
        acc_sc[...] = jnp.zeros_like(acc_sc)

    xb = xb_sc[...]
    a = _dot(xb, wg_ref[...])
    u = _dot(xb, wu_ref[...])
    h = (a * _sigmoid(a)) * u
    acc_sc[...] += _dot(h.astype(BF16), wd_ref[...])

    @pl.when(f == nf - 1)
    def _():
        y = alpha * x_ref[...] + 0.5 * acc_sc[...]
        out = _layer_norm_rows(y, g_ref[...], b_ref[...])
        o_ref[...] = out
        if with_bf16:
            ob_ref[...] = out.astype(BF16)


def _ffn_ln(x, wg, wu, wd, g, b, alpha, with_bf16):
    n, d = x.shape
    fp = wg.shape[1]
    bm = min(ROW_TILE, n)
    bf = min(FF_TILE, fp)
    nf = fp // bf
    out_shape = [jax.ShapeDtypeStruct((n, d), F32)]
    out_specs = [pl.BlockSpec((bm, d), lambda i, f: (i, 0))]
    if with_bf16:
        out_shape.append(jax.ShapeDtypeStruct((n, d), BF16))
        out_specs.append(pl.BlockSpec((bm, d), lambda i, f: (i, 0)))
    res = pl.pallas_call(
        functools.partial(_ffn_ln_kernel, alpha=alpha, nf=nf, with_bf16=with_bf16),
        grid=(n // bm, nf),
        in_specs=[
            pl.BlockSpec((bm, d), lambda i, f: (i, 0)),
            pl.BlockSpec((d, bf), lambda i, f: (0, f)),
            pl.BlockSpec((d, bf), lambda i, f: (0, f)),
            pl.BlockSpec((bf, d), lambda i, f: (f, 0)),
            pl.BlockSpec((1, d), lambda i, f: (0, 0)),
            pl.BlockSpec((1, d), lambda i, f: (0, 0)),
        ],
        out_specs=out_specs,
        out_shape=out_shape,
        scratch_shapes=[pltpu.VMEM((bm, d), BF16), pltpu.VMEM((bm, d), F32)],
        compiler_params=_params(("parallel", "arbitrary")),
        name="ffn_ln",
    )(x, wg, wu, wd, g, b)
    return res if with_bf16 else (res[0], None)


def _mm_kernel(x_ref, w_ref, o_ref):
    o_ref[...] = _dot(x_ref[...], w_ref[...]).astype(o_ref.dtype)


def _matmul(x, w, out_dtype, name):
    n, k = x.shape
    nout = w.shape[1]
    bm = min(PROJ_ROW_TILE, n)
    bn = next(c for c in (PROJ_COL_TILE, COL_TILE, 2 * LANE, LANE, nout) if nout % c == 0)
    assert n % bm == 0
    return pl.pallas_call(
        _mm_kernel,
        grid=(n // bm, nout // bn),
        in_specs=[
            pl.BlockSpec((bm, k), lambda i, j: (i, 0)),
            pl.BlockSpec((k, bn), lambda i, j: (0, j)),
        ],
        out_specs=pl.BlockSpec((bm, bn), lambda i, j: (i, j)),
        out_shape=jax.ShapeDtypeStruct((n, nout), out_dtype),
        compiler_params=_params(("parallel", "arbitrary")),
        name=name,
    )(x, w)


def _nsa_proj_kernel(x_ref, wq_ref, wkv_ref, wg_ref, exp_ref, qt_ref, kc_ref, vc_ref, ket_ref, vst_ref,
                     kw_ref, vwt_ref, gt_ref, *, chunk):
    x = x_ref[...]
    hd, g = HEAD_DIM, NSA_GROUPS
    pair = 2 * hd
    tg = _sigmoid(_dot(x, wg_ref[...])).T
    for gg in range(g):
        gt_ref[0, gg] = tg[gg * GATE_ROWS:(gg + 1) * GATE_ROWS]
    yq = _dot(x, wq_ref[...]) * QK_SCALE
    for p in range(NSA_HEADS // 2):
        t = yq[:, p * pair:(p + 1) * pair].T.astype(qt_ref.dtype)
        qt_ref[0, 2 * p] = t[0:hd]
        qt_ref[0, 2 * p + 1] = t[hd:pair]
    ykv = _dot(x, wkv_ref[...])
    col = lambda kind, gg: (kind * g + gg) * hd
    ones = jnp.where(lax.broadcasted_iota(jnp.int32, (ONES_ROWS, x.shape[0]), 0) == 0, 1.0, 0.0)
    for gg in range(g):
        rows = lambda kind: ykv[:, col(kind, gg):col(kind, gg) + hd]
        kc_ref[0, gg] = rows(0).astype(kc_ref.dtype)
        vc_ref[0, gg] = rows(1).astype(vc_ref.dtype)
        ket_ref[0, gg, :, 0:chunk] = exp_ref[...]
        ket_ref[0, gg, :, chunk:chunk + hd] = rows(2).astype(ket_ref.dtype)
        kw_ref[0, gg] = rows(4).astype(kw_ref.dtype)
    for kind, out_ref in ((3, vst_ref), (5, vwt_ref)):
        for gg in range(0, g, 2):
            t = ykv[:, col(kind, gg):col(kind, gg) + pair].T.astype(out_ref.dtype)
            for k in range(2):
                out_ref[0, gg + k, 0:hd] = t[k * hd:(k + 1) * hd]
                out_ref[0, gg + k, hd:hd + ONES_ROWS] = ones.astype(out_ref.dtype)


def _nsa_proj(hb, w_q, w_kv, w_g, expand, bsz, s):
    n, d = hb.shape
    g, hd = NSA_GROUPS, HEAD_DIM
    chunk = expand.shape[1]
    bm = min(ROW_TILE, s)
    nj = s // bm
    rows_spec = lambda w: pl.BlockSpec((1, g, bm, w), lambda b, j: (b, 0, j, 0))
    cols_spec = lambda h, r: pl.BlockSpec((1, h, r, bm), lambda b, j: (b, 0, 0, j))
    full = lambda shape: pl.BlockSpec(shape, lambda b, j: (0, 0), pipeline_mode=pl.Buffered(1))
    rows_shape = lambda w: jax.ShapeDtypeStruct((bsz, g, s, w), BF16)
    cols_shape = lambda h, r: jax.ShapeDtypeStruct((bsz, h, r, s), BF16)
    return pl.pallas_call(
        functools.partial(_nsa_proj_kernel, chunk=chunk),
        grid=(bsz, nj),
        in_specs=[
            pl.BlockSpec((bm, d), lambda b, j: (b * nj + j, 0)),
            full(w_q.shape), full(w_kv.shape), full(w_g.shape),
            pl.BlockSpec((bm, chunk), lambda b, j: (j, 0)),
        ],
        out_specs=[cols_spec(NSA_HEADS, hd), rows_spec(hd), rows_spec(hd), rows_spec(chunk + hd),
                   cols_spec(g, hd + ONES_ROWS), rows_spec(hd), cols_spec(g, hd + ONES_ROWS),
                   cols_spec(g, GATE_ROWS)],
        out_shape=[cols_shape(NSA_HEADS, hd), rows_shape(hd), rows_shape(hd), rows_shape(chunk + hd),
                   cols_shape(g, hd + ONES_ROWS), rows_shape(hd), cols_shape(g, hd + ONES_ROWS),
                   jax.ShapeDtypeStruct((bsz, g, GATE_ROWS, s), F32)],
        compiler_params=_params(("parallel", "arbitrary")),
        name="nsa_proj",
    )(hb, w_q, w_kv, w_g, expand)


def _compress_kernel(x_ref, pe_ref, w1_ref, b1_ref, w2_ref, o_ref, *, nh, half):
    x = x_ref[0, 0].astype(F32)
    xa = (x + pe_ref[0:1, :]).astype(BF16)
    xb = (x + pe_ref[1:2, :]).astype(BF16)
    ha = _dot(xa, w1_ref[0:half, :])
    hb = _dot(xb, w1_ref[half:2 * half, :])
    hid = ha + pltpu.roll(hb, nh - 1, 0) + b1_ref[...]
    c = math.sqrt(2.0 / math.pi)
    act = 0.5 * hid * (1.0 + jnp.tanh(c * (hid + 0.044715 * (hid * hid * hid))))
    o_ref[0, 0] = _dot(act.astype(BF16), w2_ref[...]).astype(o_ref.dtype)


def _compress(x, pe2, w1, b1, w2):
    bsz, g, nh, half = x.shape
    hid = w1.shape[1]
    hd = w2.shape[1]
    return pl.pallas_call(
        functools.partial(_compress_kernel, nh=nh, half=half),
        grid=(bsz, g),
        in_specs=[
            pl.BlockSpec((1, 1, nh, half), lambda b, gg: (b, gg, 0, 0)),
            pl.BlockSpec((2, half), lambda b, gg: (0, 0)),
            pl.BlockSpec((2 * half, hid), lambda b, gg: (0, 0)),
            pl.BlockSpec((1, hid), lambda b, gg: (0, 0)),
            pl.BlockSpec((hid, hd), lambda b, gg: (0, 0)),
        ],
        out_specs=pl.BlockSpec((1, 1, nh, hd), lambda b, gg: (b, gg, 0, 0)),
        out_shape=jax.ShapeDtypeStruct((bsz, g, nh, hd), BF16),
        compiler_params=_params(("parallel", "parallel")),
        name="compress",
    )(x, pe2, w1, b1, w2)


def _stack_heads(qt_ref):
    return jnp.concatenate([qt_ref[0, r] for r in range(NSA_REP)], axis=1)


def _per_head(x):
    return jnp.tile(x, (1, NSA_REP))


def _cmp_topk_kernel(qt_ref, kc_ref, vct_ref, ovt_ref, oct_ref, selt_ref, *, tq, nh, nb, n_sel, nq):
    i = pl.program_id(2)
    for part in range(CAUSAL_PARTS):
        lo, hi = part * nq // CAUSAL_PARTS, (part + 1) * nq // CAUSAL_PARTS

        @pl.when(jnp.logical_and(i >= lo, i < hi))
        def _(part=part):
            _cmp_topk_part(qt_ref, kc_ref, vct_ref, ovt_ref, oct_ref, selt_ref, i * tq, tq=tq,
                           nh=(part + 1) * nh // CAUSAL_PARTS, nb=(part + 1) * nb // CAUSAL_PARTS,
                           nb_all=nb, n_sel=n_sel)


def _cmp_topk_part(qt_ref, kc_ref, vct_ref, ovt_ref, oct_ref, selt_ref, t0, *, tq, nh, nb, nb_all, n_sel):
    cend = lax.broadcasted_iota(jnp.int32, (nh, tq), 0) * CMP_STRIDE + (CMP_BLOCK - 1)
    tpos = t0 + lax.broadcasted_iota(jnp.int32, (nh, tq), 1)
    bias = jnp.where(cend <= tpos, 0.0, NEG)
    seen = jnp.where(t0 + lax.broadcasted_iota(jnp.int32, (1, tq), 1) >= CMP_BLOCK - 1, 1.0, 0.0)
    qt = _stack_heads(qt_ref)
    s = _dot(kc_ref[0, 0, 0:nh, :], qt) + _per_head(bias)
    e = jnp.exp2(s - jnp.max(s, axis=0, keepdims=True))
    ea = _dot(vct_ref[0, 0, :, 0:nh], e.astype(BF16))
    inv = _per_head(seen) / jnp.maximum(ea[HEAD_DIM:HEAD_DIM + 1], TINY)
    oct_ref[0, 0] = ea[0:HEAD_DIM] * inv
    p = e * inv
    psum = p[:, 0:tq]
    for r in range(1, NSA_REP):
        psum = psum + p[:, r * tq:(r + 1) * tq]
    imp = _dot(ovt_ref[0:nb, 0:nh], psum.astype(BF16))
    blk = lax.broadcasted_iota(jnp.int32, (nb, tq), 0).astype(F32)
    cur = ((t0 + lax.broadcasted_iota(jnp.int32, (nb, tq), 1)) // SLC_BLOCK).astype(F32)
    forced = (blk == 0.0) | (blk == cur) | (blk == cur - 1.0)
    x = jnp.where(forced, -jnp.inf, jnp.where(blk <= cur, imp, NEG))
    for _ in range(n_sel - N_FORCED):
        mx = jnp.max(x, axis=0, keepdims=True)
        idx = jnp.min(jnp.where(x == mx, blk, float(nb)), axis=0, keepdims=True)
        x = jnp.where(blk == idx, -jnp.inf, x)
    selb = jnp.where(x == -jnp.inf, 0.0, NEG)
    selt_ref[0, 0, 0:nb, :] = selb.astype(selt_ref.dtype)
    if nb < nb_all:
        selt_ref[0, 0, nb:nb_all, :] = jnp.full((nb_all - nb, tq), NEG, selt_ref.dtype)


def _cmp_topk(qt, kc, vct, ovt):
    bsz, _, hd, s = qt.shape
    g, nh = kc.shape[1], kc.shape[2]
    nb = ovt.shape[0]
    assert nb >= SLC_TOPN and nh % CAUSAL_PARTS == 0 and nb % CAUSAL_PARTS == 0
    tq = min(Q_TILE, s)
    cols = NSA_REP * tq
    return pl.pallas_call(
        functools.partial(_cmp_topk_kernel, tq=tq, nh=nh, nb=nb, n_sel=SLC_TOPN, nq=s // tq),
        grid=(bsz, g, s // tq),
        in_specs=[
            pl.BlockSpec((1, NSA_REP, hd, tq), lambda b, gg, i: (b, gg, 0, i)),
            pl.BlockSpec((1, 1, nh, hd), lambda b, gg, i: (b, gg, 0, 0)),
            pl.BlockSpec((1, 1, hd + ONES_ROWS, nh), lambda b, gg, i: (b, gg, 0, 0)),
            pl.BlockSpec((nb, nh), lambda b, gg, i: (0, 0)),
        ],
        out_specs=[
            pl.BlockSpec((1, 1, hd, cols), lambda b, gg, i: (b, gg, 0, i)),
            pl.BlockSpec((1, 1, nb, tq), lambda b, gg, i: (b, gg, 0, i)),
        ],
        out_shape=[
            jax.ShapeDtypeStruct((bsz, g, hd, NSA_REP * s), F32),
            jax.ShapeDtypeStruct((bsz, g, nb, s), BF16),
        ],
        compiler_params=_params(("parallel", "parallel", "arbitrary")),
        name="cmp_topk",
    )(qt, kc, vct, ovt)


def _sel_win_kernel(qt_ref, ket_ref, vst_ref, kw_ref, vwt_ref, selt_ref, oct_ref, gatet_ref,
                    o_ref, rhs_sc, m_sc, acc_sc, sa_sc, sb_sc, ma_sc, mb_sc, sw_sc, mw_sc, ow_sc,
                    *, tq, tk, chunk, nvar, nchunk, wlen):
    t0 = pl.program_id(2) * tq
    qt = _stack_heads(qt_ref)
    for c in range(nchunk):
        selt = selt_ref[0, 0, c * chunk:(c + 1) * chunk, :]
        rhs_sc[c] = jnp.concatenate([_per_head(selt), qt], axis=0)

    m_sc[...] = jnp.full_like(m_sc, NEG)
    acc_sc[...] = jnp.zeros_like(acc_sc)

    cols = NSA_REP * tq
    part_cols = [slice(c * cols // COL_PARTS, (c + 1) * cols // COL_PARTS) for c in range(COL_PARTS)]

    def put_scores(slot, kt, causal, cs):
        s_ref, mx_ref = slot
        k0 = pl.multiple_of(kt * tk, tk)
        s = _dot(ket_ref[0, 0, pl.ds(k0, tk), :], rhs_sc[kt // nvar, :, cs])
        if causal:
            kpos = k0 + lax.broadcasted_iota(jnp.int32, (tk, tq), 0)
            tcol = t0 + lax.broadcasted_iota(jnp.int32, (tk, tq), 1)
            s = s + jnp.tile(jnp.where(kpos <= tcol, 0.0, NEG), (1, NSA_REP // COL_PARTS))
        s_ref[:, cs] = s
        mx_ref[:, cs] = jnp.max(s, axis=0, keepdims=True)

    def accumulate(slot, kt, cs):
        s_ref, mx_ref = slot
        k0 = pl.multiple_of(kt * tk, tk)
        m_old = m_sc[:, cs]
        m_new = jnp.maximum(m_old, mx_ref[:, cs])
        alpha = jnp.exp2(m_old - m_new)
        p = jnp.exp2(s_ref[:, cs] - m_new)
        acc_sc[:, cs] = alpha * acc_sc[:, cs] + _dot(vst_ref[0, 0, :, pl.ds(k0, tk)], p.astype(BF16))
        m_sc[:, cs] = m_new

    def put_all(slot, kt, causal):
        for cs in part_cols:
            put_scores(slot, kt, causal, cs)

    def accumulate_all(slot, kt):
        for cs in part_cols:
            accumulate(slot, kt, cs)

    def stage(dst, kt_new, causal, src, kt_old):
        for cs in part_cols:
            put_scores(dst, kt_new, causal, cs)
            accumulate(src, kt_old, cs)

    slot_a, slot_b = (sa_sc, ma_sc), (sb_sc, mb_sc)
    nfull = (t0 + tq - 1) // tk

    wk = wlen + tq
    w0 = pl.multiple_of(jnp.maximum(t0 - wlen, 0), tq)
    diff = (t0 + lax.broadcasted_iota(jnp.int32, (wk, tq), 1)) - (
        w0 + lax.broadcasted_iota(jnp.int32, (wk, tq), 0))
    wbias = jnp.where(jnp.logical_and(diff >= 0, diff < wlen), 0.0, NEG)
    sw = _dot(kw_ref[0, 0, pl.ds(w0, wk), :], qt) + _per_head(wbias)
    sw_sc[...] = sw
    mw_sc[...] = jnp.max(sw, axis=0, keepdims=True)

    def window_out():
        ew = jnp.exp2(sw_sc[...] - mw_sc[...])
        ow_sc[...] = _dot(vwt_ref[0, 0, :, pl.ds(w0, wk)], ew.astype(BF16))

    @pl.when(nfull == 0)
    def _():
        put_all(slot_a, 0, True)
        window_out()
        accumulate_all(slot_a, 0)

    @pl.when(nfull > 0)
    def _():
        put_all(slot_a, 0, False)
        window_out()

        def body(j, carry):
            stage(slot_b, 2 * j + 1, False, slot_a, 2 * j)
            stage(slot_a, 2 * j + 2, False, slot_b, 2 * j + 1)
            return carry

        lax.fori_loop(0, (nfull - 1) // 2, body, 0)

        @pl.when(nfull % 2 == 1)
        def _():
            stage(slot_b, nfull, True, slot_a, nfull - 1)
            accumulate_all(slot_b, nfull)

        @pl.when(nfull % 2 == 0)
        def _():
            stage(slot_b, nfull - 1, False, slot_a, nfull - 2)
            stage(slot_a, nfull, True, slot_b, nfull - 1)
            accumulate_all(slot_a, nfull)

    o_s = acc_sc[0:HEAD_DIM] * (1.0 / jnp.maximum(acc_sc[HEAD_DIM:HEAD_DIM + 1], TINY))

    o_w = ow_sc[0:HEAD_DIM] * (1.0 / jnp.maximum(ow_sc[HEAD_DIM:HEAD_DIM + 1], TINY))

    gate = gatet_ref[0, 0]
    o_c = oct_ref[0, 0]
    outs = []
    for r in range(NSA_REP):
        cs = slice(r * tq, (r + 1) * tq)
        outs.append(gate[3 * r:3 * r + 1] * o_c[:, cs] + gate[3 * r + 1:3 * r + 2] * o_s[:, cs]
                    + gate[3 * r + 2:3 * r + 3] * o_w[:, cs])
    pairs = [jnp.concatenate(outs[r:r + 2], axis=0).T for r in range(0, NSA_REP, 2)]
    o_ref[0] = jnp.concatenate(pairs, axis=1).astype(o_ref.dtype)


def _sel_win(qt, ket, vst, kw, vwt, selt, oct, gatet, tk):
    bsz, _, hd, s = qt.shape
    g = ket.shape[1]
    nb = selt.shape[2]
    chunk = ket.shape[3] - hd
    nchunk = nb // chunk
    nvar = chunk * SLC_BLOCK // tk
    tq = min(Q_TILE, s)
    cols = NSA_REP * tq
    resident = lambda shape: pl.BlockSpec(shape, lambda b, gg, i: (b, gg, 0, 0),
                                          pipeline_mode=pl.Buffered(1))
    return pl.pallas_call(
        functools.partial(_sel_win_kernel, tq=tq, tk=tk, chunk=chunk, nvar=nvar, nchunk=nchunk,
                          wlen=WINDOW),
        grid=(bsz, g, s // tq),
        in_specs=[
            pl.BlockSpec((1, NSA_REP, hd, tq), lambda b, gg, i: (b, gg, 0, i)),
            resident((1, 1, s, chunk + hd)),
            resident((1, 1, hd + ONES_ROWS, s)),
            resident((1, 1, s, hd)),
            resident((1, 1, hd + ONES_ROWS, s)),
            pl.BlockSpec((1, 1, nb, tq), lambda b, gg, i: (b, gg, 0, i)),
            pl.BlockSpec((1, 1, hd, cols), lambda b, gg, i: (b, gg, 0, i)),
            pl.BlockSpec((1, 1, GATE_ROWS, tq), lambda b, gg, i: (b, gg, 0, i)),
        ],
        out_specs=pl.BlockSpec((1, tq, NSA_REP * hd), lambda b, gg, i: (b, i, gg)),
        out_shape=jax.ShapeDtypeStruct((bsz, s, g * NSA_REP * hd), BF16),
        scratch_shapes=[
            pltpu.VMEM((nchunk, chunk + hd, cols), BF16),
            pltpu.VMEM((1, cols), F32),
            pltpu.VMEM((hd + ONES_ROWS, cols), F32),
            pltpu.VMEM((tk, cols), F32),
            pltpu.VMEM((tk, cols), F32),
            pltpu.VMEM((1, cols), F32),
            pltpu.VMEM((1, cols), F32),
            pltpu.VMEM((WINDOW + tq, cols), F32),
            pltpu.VMEM((1, cols), F32),
            pltpu.VMEM((hd + ONES_ROWS, cols), F32),
        ],
        compiler_params=_params(("parallel", "parallel", "arbitrary")),
        name="sel_win",
    )(qt, ket, vst, kw, vwt, selt, oct, gatet)


def _retention_kernel(q_ref, k_ref, v_ref, gb_ref, cos_ref, sin_ref, dmat_ref, qdec_ref, kdec_ref,
                      cdec_ref, gng_ref, gnb_ref, o_ref, state_sc, *, nchunk, c):
    @pl.when(pl.program_id(2) == 0)
    def _():
        state_sc[...] = jnp.zeros_like(state_sc)

    dmat = dmat_ref[0]
    qdec = qdec_ref[0]
    kdec = kdec_ref[0]
    cdec = cdec_ref[0]
    kscale = RET_DIM ** -0.5
    half = RET_DIM // 2
    for n in range(nchunk):
        rows = pl.ds(n * c, c)
        cos = cos_ref[rows, :]
        sin = sin_ref[rows, :]
        qf = q_ref[0, rows, :].astype(F32)
        kf = k_ref[0, rows, :].astype(F32)
        qr = qf * cos + pltpu.roll(qf, half, 1) * sin
        kr = (kf * cos + pltpu.roll(kf, half, 1) * sin) * kscale
        v = v_ref[0, rows, :]
        inner = _dot_nt(qr.astype(BF16), kr.astype(BF16)) * dmat
        state = state_sc[...]
        y = _dot(inner.astype(BF16), v) + _dot((qr * qdec).astype(BF16), state.astype(BF16))
        state_sc[...] = state * cdec + _dot_tn((kr * kdec).astype(BF16), v)
        mu = jnp.mean(y, axis=-1, keepdims=True)
        d = y - mu
        var = jnp.mean(d * d, axis=-1, keepdims=True)
        yn = d * lax.rsqrt(var + LN_EPS) * gng_ref[...] + gnb_ref[...]
        gb = gb_ref[0, rows, :].astype(F32)
        o_ref[0, rows, :] = ((gb * _sigmoid(gb)) * yn).astype(o_ref.dtype)


def _retention(ret, cos2, sin2, dmat, qdec, kdec, cdec, gng, gnb):
    bsz, s, _ = ret.shape
    h = RET_HEADS
    c = RET_CHUNK
    tc = min(RET_TILE, s)
    spec = lambda off: pl.BlockSpec((1, tc, RET_DIM), lambda b, hh, j, off=off: (b, j, off + hh))
    hspec = lambda shp: pl.BlockSpec((1,) + shp, lambda b, hh, j: (hh, 0, 0))
    return pl.pallas_call(
        functools.partial(_retention_kernel, nchunk=tc // c, c=c),
        grid=(bsz, h, s // tc),
        in_specs=[
            spec(0), spec(h), spec(2 * h), spec(3 * h),
            pl.BlockSpec((tc, RET_DIM), lambda b, hh, j: (j, 0)),
            pl.BlockSpec((tc, RET_DIM), lambda b, hh, j: (j, 0)),
            hspec((c, c)), hspec((c, RET_DIM)), hspec((c, RET_DIM)), hspec((1, RET_DIM)),
            pl.BlockSpec((1, RET_DIM), lambda b, hh, j: (0, hh)),
            pl.BlockSpec((1, RET_DIM), lambda b, hh, j: (0, hh)),
        ],
        out_specs=pl.BlockSpec((1, tc, RET_DIM), lambda b, hh, j: (b, j, hh)),
        out_shape=jax.ShapeDtypeStruct((bsz, s, h * RET_DIM), BF16),
        scratch_shapes=[pltpu.VMEM((RET_DIM, RET_DIM), F32)],
        compiler_params=_params(("parallel", "parallel", "arbitrary")),
        name="retention",
    )(ret, ret, ret, ret, cos2, sin2, dmat, qdec, kdec, cdec, gng, gnb)


def _merge_kernel(h_ref, oa_ref, ob_ref, wga_ref, wgb_ref, wpa_ref, wpb_ref, o_ref):
    hb = h_ref[...]
    ga = _sigmoid(_dot(hb, wga_ref[...]))
    gb = _sigmoid(_dot(hb, wgb_ref[...]))
    merged = ga * _dot(oa_ref[...], wpa_ref[...]) + gb * _dot(ob_ref[...], wpb_ref[...])
    o_ref[...] = merged.astype(o_ref.dtype)


def _merge(hb, oa, ob, wga, wgb, wpa, wpb):
    n, d = hb.shape
    bm = min(PROJ_ROW_TILE, n)
    bn = min(COL_TILE, d)
    ka, kb = oa.shape[1], ob.shape[1]
    return pl.pallas_call(
        _merge_kernel,
        grid=(n // bm, d // bn),
        in_specs=[
            pl.BlockSpec((bm, d), lambda i, j: (i, 0)),
            pl.BlockSpec((bm, ka), lambda i, j: (i, 0)),
            pl.BlockSpec((bm, kb), lambda i, j: (i, 0)),
            pl.BlockSpec((d, bn), lambda i, j: (0, j)),
            pl.BlockSpec((d, bn), lambda i, j: (0, j)),
            pl.BlockSpec((ka, bn), lambda i, j: (0, j)),
            pl.BlockSpec((kb, bn), lambda i, j: (0, j)),
        ],
        out_specs=pl.BlockSpec((bm, bn), lambda i, j: (i, j)),
        out_shape=jax.ShapeDtypeStruct((n, d), BF16),
        compiler_params=_params(("parallel", "arbitrary")),
        name="merge",
    )(hb, oa, ob, wga, wgb, wpa, wpb)


def _proj_ln_kernel(x_ref, m_ref, w_ref, g_ref, b_ref, o_ref, *, alpha):
    y = alpha * x_ref[...] + _dot(m_ref[...], w_ref[...])
    o_ref[...] = _layer_norm_rows(y, g_ref[...], b_ref[...])


def _proj_ln(x, m, w, g, b, alpha):
    n, d = x.shape
    bm = min(ROW_TILE, n)
    return pl.pallas_call(
        functools.partial(_proj_ln_kernel, alpha=alpha),
        grid=(n // bm,),
        in_specs=[
            pl.BlockSpec((bm, d), lambda i: (i, 0)),
            pl.BlockSpec((bm, d), lambda i: (i, 0)),
            pl.BlockSpec((d, d), lambda i: (0, 0)),
            pl.BlockSpec((1, d), lambda i: (0, 0)),
            pl.BlockSpec((1, d), lambda i: (0, 0)),
        ],
        out_specs=pl.BlockSpec((bm, d), lambda i: (i, 0)),
        out_shape=jax.ShapeDtypeStruct((n, d), F32),
        compiler_params=_params(("parallel",)),
        name="proj_ln",
    )(x, m, w, g, b)


def _overlap_matrix(nh, nb):
    c0 = np.arange(nh) * CMP_STRIDE
    c1 = c0 + CMP_BLOCK
    s0 = np.arange(nb) * SLC_BLOCK
    s1 = s0 + SLC_BLOCK
    ov = (c0[:, None] < s1[None, :]) & (c1[:, None] > s0[None, :])
    ov[nh - 1, :] = False
    return jnp.asarray(ov.T, BF16)


def _expand_matrix(nb, s):
    chunk = min(SEL_CHUNK, nb)
    blk_in_chunk = (np.arange(s) // SLC_BLOCK) % chunk
    return jnp.asarray(blk_in_chunk[:, None] == np.arange(chunk)[None, :], BF16)


def _retention_tables(s):
    h, c, d = RET_HEADS, RET_CHUNK, RET_DIM
    inv = ROPE_BASE ** (-jnp.arange(0, d, 2, dtype=F32) / d)
    ang = jnp.arange(s)[:, None].astype(F32) * inv[None, :]
    cos, sin = jnp.cos(ang), jnp.sin(ang)
    cos2 = jnp.concatenate([cos, cos], -1)
    sin2 = jnp.concatenate([-sin, sin], -1)
    log_g = jnp.log1p(-jnp.exp2(-5.0 - jnp.arange(h, dtype=F32)))
    i = jnp.arange(c, dtype=F32)
    diff = i[:, None] - i[None, :]
    dmat = jnp.where(diff >= 0, jnp.exp(jnp.maximum(diff, 0.0)[None] * log_g[:, None, None]), 0.0)
    kdec = jnp.exp((c - 1 - i)[None, :] * log_g[:, None])
    qdec = jnp.exp((i + 1)[None, :] * log_g[:, None])
    cdec = jnp.exp(c * log_g)
    bc = lambda t: jnp.broadcast_to(t[:, :, None], (h, c, d))
    return cos2, sin2, dmat, bc(qdec), bc(kdec), jnp.broadcast_to(cdec[:, None, None], (h, 1, d))


def _pad_cols(w, mult):
    pad = (-w.shape[1]) % mult
    return jnp.pad(w, ((0, 0), (0, pad))) if pad else w


def _mixer(hf, hb, bsz, s, w_in, cmp_k, cmp_v, ret_gn_g, ret_gn_b, w_merge_gate, w_proj_a, w_proj_b,
           w_o, ln_g, ln_b, alpha, tables):
    n, d = hf.shape
    g, hd = NSA_GROUPS, HEAD_DIM
    ovt, expand, ret_tabs = tables
    o_nsa = NSA_Q + 6 * NSA_KV
    w_q = w_in[:, :NSA_Q].astype(BF16)
    w_kv = w_in[:, NSA_Q:o_nsa].astype(BF16)
    w_gate = w_in[:, o_nsa:o_nsa + NSA_GATE].reshape(d, g, 3 * NSA_REP)
    w_gate = jnp.pad(w_gate, ((0, 0), (0, 0), (0, GATE_ROWS - 3 * NSA_REP))).reshape(d, g * GATE_ROWS)
    w_gate = _pad_cols(w_gate, LANE).astype(BF16)
    w_ret = w_in[:, o_nsa + NSA_GATE:].astype(BF16)

    qt, kc_in, vc_in, ket, vst, kw, vwt, gatet = _nsa_proj(hb, w_q, w_kv, w_gate, expand, bsz, s)
    ret = _matmul(hb, w_ret, BF16, "proj_ret")

    nh = s // CMP_STRIDE
    half = CMP_STRIDE * hd

    def compress(t, prm):
        pe, w1, b1, w2 = prm
        return _compress(t.reshape(bsz, g, nh, half), pe.reshape(2, half), w1.astype(BF16),
                         b1.reshape(1, -1), w2.astype(BF16))

    kc = compress(kc_in, cmp_k)
    ones_rows = jnp.zeros((bsz, g, ONES_ROWS, nh), BF16).at[:, :, 0].set(1.0)
    vct = jnp.concatenate([jnp.transpose(compress(vc_in, cmp_v), (0, 1, 3, 2)), ones_rows], axis=2)
    oct, selt = _cmp_topk(qt, kc, vct, ovt)

    o_a = _sel_win(qt, ket, vst, kw, vwt, selt, oct, gatet, min(K_TILE, s))

    o_b = _retention(ret.reshape(bsz, s, 4 * RET_W), *ret_tabs,
                     ret_gn_g.reshape(1, -1), ret_gn_b.reshape(1, -1))

    merged = _merge(hb, o_a.reshape(n, NSA_Q), o_b.reshape(n, RET_W),
                    w_merge_gate[:, :d].astype(BF16), w_merge_gate[:, d:].astype(BF16),
                    w_proj_a.astype(BF16), w_proj_b.astype(BF16))
    return _proj_ln(hf, merged, w_o.astype(BF16), ln_g.reshape(1, -1), ln_b.reshape(1, -1), alpha)


def kernel(x, ffn1_w_gate, ffn1_w_up, ffn1_w_down, ln1_g, ln1_b, w_in, cmp_k_pe, cmp_k_w1, cmp_k_b1,
           cmp_k_w2, cmp_v_pe, cmp_v_w1, cmp_v_b1, cmp_v_w2, ret_gn_g, ret_gn_b, w_merge_gate, w_proj_a,
           w_proj_b, w_o, ln2_g, ln2_b, ffn2_w_gate, ffn2_w_up, ffn2_w_down, ln3_g, ln3_b):
    bsz, s, d = x.shape
    depth = ffn1_w_gate.shape[0]
    f = ffn1_w_gate.shape[2]
    alpha = (2 * depth) ** 0.25
    fpad = (-f) % min(FF_TILE, f)
    nb = s // SLC_BLOCK
    tables = (_overlap_matrix(s // CMP_STRIDE, nb), _expand_matrix(nb, s), _retention_tables(s))

    def ffn_weights(wg, wu, wd):
        return (jnp.pad(wg.astype(BF16), ((0, 0), (0, fpad))), jnp.pad(wu.astype(BF16), ((0, 0), (0, fpad))),
                jnp.pad(wd.astype(BF16), ((0, fpad), (0, 0))))

    row = lambda t: t.reshape(1, -1)
    xf = x.reshape(bsz * s, d)
    for l in range(depth):
        hf, hb = _ffn_ln(xf, *ffn_weights(ffn1_w_gate[l], ffn1_w_up[l], ffn1_w_down[l]),
                         row(ln1_g[l]), row(ln1_b[l]), alpha, True)
        xf = _mixer(hf, hb, bsz, s, w_in[l],
                    (cmp_k_pe[l], cmp_k_w1[l], cmp_k_b1[l], cmp_k_w2[l]),
                    (cmp_v_pe[l], cmp_v_w1[l], cmp_v_b1[l], cmp_v_w2[l]),
                    ret_gn_g[l], ret_gn_b[l], w_merge_gate[l], w_proj_a[l], w_proj_b[l], w_o[l],
                    ln2_g[l], ln2_b[l], alpha, tables)
        xf, _ = _ffn_ln(xf, *ffn_weights(ffn2_w_gate[l], ffn2_w_up[l], ffn2_w_down[l]),
                        row(ln3_g[l]), row(ln3_b[l]), alpha, False)
    return xf.reshape(bsz, s, d)
```

```python
import functools
import math

import jax
import jax.numpy as jnp
import numpy as np
from jax import lax
from jax.experimental import pallas as pl
from jax.experimental.pallas import tpu as pltpu

NSA_HEADS = 16
NSA_GROUPS = 4
NSA_REP = NSA_HEADS // NSA_GROUPS
HEAD_DIM = 64
CMP_BLOCK = 32
CMP_STRIDE = 16
SLC_BLOCK = 64
SLC_TOPN = 16
N_FORCED = 3
CAUSAL_PARTS = 8
COL_PARTS = 2
SEL_CHUNK = 64
WINDOW = 512
RET_HEADS = 8
RET_DIM = 128
RET_CHUNK = 128
ROPE_BASE = 10000.0
LN_EPS = 1e-5
NEG = -1e30
TINY = 1e-30
NSA_Q = NSA_HEADS * HEAD_DIM
NSA_KV = NSA_GROUPS * HEAD_DIM
NSA_GATE = 3 * NSA_HEADS
GATE_ROWS = 16
ONES_ROWS = 16
QK_SCALE = HEAD_DIM ** -0.5 * math.log2(math.e)
RET_W = RET_HEADS * RET_DIM

LANE = 128
VMEM_LIMIT = 52 * 1024 * 1024
ROW_TILE = 512
FFN_ROW_TILE = 1024
FF_TILE = 256
COL_TILE = 512
PROJ_ROW_TILE = 1024
PROJ_COL_TILE = 1024
Q_TILE = 256
K_TILE = 512
RET_TILE = 1024

BF16 = jnp.bfloat16
F32 = jnp.float32


def _dot(a, b):
    return jnp.dot(a, b, preferred_element_type=F32)


def _dot_nt(a, b):
    return lax.dot_general(a, b, (((1,), (1,)), ((), ())), preferred_element_type=F32)


def _dot_tn(a, b):
    return lax.dot_general(a, b, (((0,), (0,)), ((), ())), preferred_element_type=F32)


def _sigmoid(x):
    return 1.0 / (1.0 + jnp.exp(-x))


def _layer_norm_rows(y, g, b):
    mu = jnp.mean(y, axis=-1, keepdims=True)
    d = y - mu
    var = jnp.mean(d * d, axis=-1, keepdims=True)
    return d * lax.rsqrt(var + LN_EPS) * g + b


def _params(sem):
    return pltpu.CompilerParams(dimension_semantics=sem, vmem_limit_bytes=VMEM_LIMIT)


def _ffn_ln_kernel(x_ref, wgu_ref, wd_ref, g_ref, b_ref, *rest, alpha, nf, bf, with_bf16):
    if with_bf16:
        o_ref, ob_ref, xb_sc = rest
    else:
        o_ref, xb_sc = rest
        ob_ref = None
    f = pl.program_id(1)

    @pl.when(f == 0)
    def _():
        xb_sc[...] = x_ref[...].astype(BF16)
        o_ref[...] = jnp.zeros_like(o_ref)

    au = _dot(xb_sc[...], wgu_ref[...])
    a, u = au[:, 0:bf], au[:, bf:2 * bf]
    h = (a * _sigmoid(a)) * u
    o_ref[...] += _dot(h.astype(BF16), wd_ref[...])

    @pl.when(f == nf - 1)
    def _():
        y = alpha * x_ref[...] + 0.5 * o_ref[...]
        out = _layer_norm_rows(y, g_ref[...], b_ref[...])
        o_ref[...] = out
        if with_bf16:
            ob_ref[...] = out.astype(BF16)


def _ffn_ln(x, wgu, wd, g, b, alpha, with_bf16):
    n, d = x.shape
    fp = wd.shape[0]
    bm = min(FFN_ROW_TILE, n)
    bf = min(FF_TILE, fp)
    nf = fp // bf
    out_shape = [jax.ShapeDtypeStruct((n, d), F32)]
    out_specs = [pl.BlockSpec((bm, d), lambda i, f: (i, 0))]
    if with_bf16:
        out_shape.append(jax.ShapeDtypeStruct((n, d), BF16))
        out_specs.append(pl.BlockSpec((bm, d), lambda i, f: (i, 0)))
    res = pl.pallas_call(
        functools.partial(_ffn_ln_kernel, alpha=alpha, nf=nf, bf=bf, with_bf16=with_bf16),
        grid=(n // bm, nf),
        in_specs=[
            pl.BlockSpec((bm, d), lambda i, f: (i, 0), pipeline_mode=pl.Buffered(1)),
            pl.BlockSpec((d, 2 * bf), lambda i, f: (0, f)),
            pl.BlockSpec((bf, d), lambda i, f: (f, 0)),
            pl.BlockSpec((1, d), lambda i, f: (0, 0)),
            pl.BlockSpec((1, d), lambda i, f: (0, 0)),
        ],
        out_specs=out_specs,
        out_shape=out_shape,
        scratch_shapes=[pltpu.VMEM((bm, d), BF16)],
        compiler_params=_params(("parallel", "arbitrary")),
        name="ffn_ln",
    )(x, wgu, wd, g, b)
    return res if with_bf16 else (res[0], None)


def _mm_kernel(x_ref, w_ref, o_ref):
    o_ref[...] = _dot(x_ref[...], w_ref[...]).astype(o_ref.dtype)


def _matmul(x, w, out_dtype, name):
    n, k = x.shape
    nout = w.shape[1]
    bm = min(PROJ_ROW_TILE, n)
    bn = next(c for c in (PROJ_COL_TILE, COL_TILE, 2 * LANE, LANE, nout) if nout % c == 0)
    assert n % bm == 0
    return pl.pallas_call(
        _mm_kernel,
        grid=(n // bm, nout // bn),
        in_specs=[
            pl.BlockSpec((bm, k), lambda i, j: (i, 0)),
            pl.BlockSpec((k, bn), lambda i, j: (0, j)),
        ],
        out_specs=pl.BlockSpec((bm, bn), lambda i, j: (i, j)),
        out_shape=jax.ShapeDtypeStruct((n, nout), out_dtype),
        compiler_params=_params(("parallel", "arbitrary")),
        name=name,
    )(x, w)


def _nsa_proj_kernel(x_ref, wq_ref, wkv_ref, wg_ref, exp_ref, qt_ref, kc_ref, vc_ref, ket_ref, vst_ref,
                     kw_ref, vwt_ref, gt_ref, *, chunk):
    x = x_ref[...]
    hd, g = HEAD_DIM, NSA_GROUPS
    pair = 2 * hd
    tg = _sigmoid(_dot(x, wg_ref[...])).T
    for gg in range(g):
        gt_ref[0, gg] = tg[gg * GATE_ROWS:(gg + 1) * GATE_ROWS]
    yq = _dot(x, wq_ref[...]) * QK_SCALE
    for p in range(NSA_HEADS // 2):
        t = yq[:, p * pair:(p + 1) * pair].T.astype(qt_ref.dtype)
        qt_ref[0, 2 * p] = t[0:hd]
        qt_ref[0, 2 * p + 1] = t[hd:pair]
    ykv = _dot(x, wkv_ref[...])
    col = lambda kind, gg: (kind * g + gg) * hd
    ones = jnp.where(lax.broadcasted_iota(jnp.int32, (ONES_ROWS, x.shape[0]), 0) == 0, 1.0, 0.0)
    for gg in range(g):
        rows = lambda kind: ykv[:, col(kind, gg):col(kind, gg) + hd]
        kc_ref[0, gg] = rows(0).astype(kc_ref.dtype)
        vc_ref[0, gg] = rows(1).astype(vc_ref.dtype)
        ket_ref[0, gg, :, 0:chunk] = exp_ref[...]
        ket_ref[0, gg, :, chunk:chunk + hd] = rows(2).astype(ket_ref.dtype)
        kw_ref[0, gg] = rows(4).astype(kw_ref.dtype)
    for kind, out_ref in ((3, vst_ref), (5, vwt_ref)):
        for gg in range(0, g, 2):
            t = ykv[:, col(kind, gg):col(kind, gg) + pair].T.astype(out_ref.dtype)
            for k in range(2):
                out_ref[0, gg + k, 0:hd] = t[k * hd:(k + 1) * hd]
                out_ref[0, gg + k, hd:hd + ONES_ROWS] = ones.astype(out_ref.dtype)


def _nsa_proj(hb, w_q, w_kv, w_g, expand, bsz, s):
    n, d = hb.shape
    g, hd = NSA_GROUPS, HEAD_DIM
    chunk = expand.shape[1]
    bm = min(ROW_TILE, s)
    nj = s // bm
    rows_spec = lambda w: pl.BlockSpec((1, g, bm, w), lambda b, j: (b, 0, j, 0))
    cols_spec = lambda h, r: pl.BlockSpec((1, h, r, bm), lambda b, j: (b, 0, 0, j))
    full = lambda shape: pl.BlockSpec(shape, lambda b, j: (0, 0), pipeline_mode=pl.Buffered(1))
    rows_shape = lambda w: jax.ShapeDtypeStruct((bsz, g, s, w), BF16)
    cols_shape = lambda h, r: jax.ShapeDtypeStruct((bsz, h, r, s), BF16)
    return pl.pallas_call(
        functools.partial(_nsa_proj_kernel, chunk=chunk),
        grid=(bsz, nj),
        in_specs=[
            pl.BlockSpec((bm, d), lambda b, j: (b * nj + j, 0)),
            full(w_q.shape), full(w_kv.shape), full(w_g.shape),
            pl.BlockSpec((bm, chunk), lambda b, j: (j, 0)),
        ],
        out_specs=[cols_spec(NSA_HEADS, hd), rows_spec(hd), rows_spec(hd), rows_spec(chunk + hd),
                   cols_spec(g, hd + ONES_ROWS), rows_spec(hd), cols_spec(g, hd + ONES_ROWS),
                   cols_spec(g, GATE_ROWS)],
        out_shape=[cols_shape(NSA_HEADS, hd), rows_shape(hd), rows_shape(hd), rows_shape(chunk + hd),
                   cols_shape(g, hd + ONES_ROWS), rows_shape(hd), cols_shape(g, hd + ONES_ROWS),
                   jax.ShapeDtypeStruct((bsz, g, GATE_ROWS, s), F32)],
        compiler_params=_params(("parallel", "arbitrary")),
        name="nsa_proj",
    )(hb, w_q, w_kv, w_g, expand)


def _compress_kernel(x_ref, pe_ref, w1_ref, b1_ref, w2_ref, o_ref, *, nh, half):
    x = x_ref[0, 0].astype(F32)
    xa = (x + pe_ref[0:1, :]).astype(BF16)
    xb = (x + pe_ref[1:2, :]).astype(BF16)
    ha = _dot(xa, w1_ref[0:half, :])
    hb = _dot(xb, w1_ref[half:2 * half, :])
    hid = ha + pltpu.roll(hb, nh - 1, 0) + b1_ref[...]
    c = math.sqrt(2.0 / math.pi)
    act = 0.5 * hid * (1.0 + jnp.tanh(c * (hid + 0.044715 * (hid * hid * hid))))
    o_ref[0, 0] = _dot(act.astype(BF16), w2_ref[...]).astype(o_ref.dtype)


def _compress(x, pe2, w1, b1, w2):
    bsz, g, nh, half = x.shape
    hid = w1.shape[1]
    hd = w2.shape[1]
    return pl.pallas_call(
        functools.partial(_compress_kernel, nh=nh, half=half),
        grid=(bsz, g),
        in_specs=[
            pl.BlockSpec((1, 1, nh, half), lambda b, gg: (b, gg, 0, 0)),
            pl.BlockSpec((2, half), lambda b, gg: (0, 0)),
            pl.BlockSpec((2 * half, hid), lambda b, gg: (0, 0)),
            pl.BlockSpec((1, hid), lambda b, gg: (0, 0)),
            pl.BlockSpec((hid, hd), lambda b, gg: (0, 0)),
        ],
        out_specs=pl.BlockSpec((1, 1, nh, hd), lambda b, gg: (b, gg, 0, 0)),
        out_shape=jax.ShapeDtypeStruct((bsz, g, nh, hd), BF16),
        compiler_params=_params(("parallel", "parallel")),
        name="compress",
    )(x, pe2, w1, b1, w2)


def _stack_heads(qt_ref):
    return jnp.concatenate([qt_ref[0, r] for r in range(NSA_REP)], axis=1)


def _per_head(x):
    return jnp.tile(x, (1, NSA_REP))


def _cmp_topk_kernel(qt_ref, kc_ref, vct_ref, ovt_ref, oct_ref, selt_ref, *, tq, nh, nb, n_sel, nq):
    i = pl.program_id(2)
    for part in range(CAUSAL_PARTS):
        lo, hi = part * nq // CAUSAL_PARTS, (part + 1) * nq // CAUSAL_PARTS

        @pl.when(jnp.logical_and(i >= lo, i < hi))
        def _(part=part):
            _cmp_topk_part(qt_ref, kc_ref, vct_ref, ovt_ref, oct_ref, selt_ref, i * tq, tq=tq,
                           nh=(part + 1) * nh // CAUSAL_PARTS, nb=(part + 1) * nb // CAUSAL_PARTS,
                           nb_all=nb, n_sel=n_sel)


def _cmp_topk_part(qt_ref, kc_ref, vct_ref, ovt_ref, oct_ref, selt_ref, t0, *, tq, nh, nb, nb_all, n_sel):
    cend = lax.broadcasted_iota(jnp.int32, (nh, tq), 0) * CMP_STRIDE + (CMP_BLOCK - 1)
    tpos = t0 + lax.broadcasted_iota(jnp.int32, (nh, tq), 1)
    bias = jnp.where(cend <= tpos, 0.0, NEG)
    seen = jnp.where(t0 + lax.broadcasted_iota(jnp.int32, (1, tq), 1) >= CMP_BLOCK - 1, 1.0, 0.0)
    qt = _stack_heads(qt_ref)
    s = _dot(kc_ref[0, 0, 0:nh, :], qt) + _per_head(bias)
    e = jnp.exp2(s - jnp.max(s, axis=0, keepdims=True))
    ea = _dot(vct_ref[0, 0, :, 0:nh], e.astype(BF16))
    inv = _per_head(seen) / jnp.maximum(ea[HEAD_DIM:HEAD_DIM + 1], TINY)
    oct_ref[0, 0] = ea[0:HEAD_DIM] * inv
    p = e * inv
    psum = p[:, 0:tq]
    for r in range(1, NSA_REP):
        psum = psum + p[:, r * tq:(r + 1) * tq]
    imp = _dot(ovt_ref[0:nb, 0:nh], psum.astype(BF16))
    blk = lax.broadcasted_iota(jnp.int32, (nb, tq), 0).astype(F32)
    cur = ((t0 + lax.broadcasted_iota(jnp.int32, (nb, tq), 1)) // SLC_BLOCK).astype(F32)
    forced = (blk == 0.0) | (blk == cur) | (blk == cur - 1.0)
    x = jnp.where(forced, -jnp.inf, jnp.where(blk <= cur, imp, NEG))
    for _ in range(n_sel - N_FORCED):
        mx = jnp.max(x, axis=0, keepdims=True)
        idx = jnp.min(jnp.where(x == mx, blk, float(nb)), axis=0, keepdims=True)
        x = jnp.where(blk == idx, -jnp.inf, x)
    selb = jnp.where(x == -jnp.inf, 0.0, NEG)
    selt_ref[0, 0, 0:nb, :] = selb.astype(selt_ref.dtype)
    if nb < nb_all:
        selt_ref[0, 0, nb:nb_all, :] = jnp.full((nb_all - nb, tq), NEG, selt_ref.dtype)


def _cmp_topk(qt, kc, vct, ovt):
    bsz, _, hd, s = qt.shape
    g, nh = kc.shape[1], kc.shape[2]
    nb = ovt.shape[0]
    assert nb >= SLC_TOPN and nh % CAUSAL_PARTS == 0 and nb % CAUSAL_PARTS == 0
    tq = min(Q_TILE, s)
    cols = NSA_REP * tq
    return pl.pallas_call(
        functools.partial(_cmp_topk_kernel, tq=tq, nh=nh, nb=nb, n_sel=SLC_TOPN, nq=s // tq),
        grid=(bsz, g, s // tq),
        in_specs=[
            pl.BlockSpec((1, NSA_REP, hd, tq), lambda b, gg, i: (b, gg, 0, i)),
            pl.BlockSpec((1, 1, nh, hd), lambda b, gg, i: (b, gg, 0, 0)),
            pl.BlockSpec((1, 1, hd + ONES_ROWS, nh), lambda b, gg, i: (b, gg, 0, 0)),
            pl.BlockSpec((nb, nh), lambda b, gg, i: (0, 0)),
        ],
        out_specs=[
            pl.BlockSpec((1, 1, hd, cols), lambda b, gg, i: (b, gg, 0, i)),
            pl.BlockSpec((1, 1, nb, tq), lambda b, gg, i: (b, gg, 0, i)),
        ],
        out_shape=[
            jax.ShapeDtypeStruct((bsz, g, hd, NSA_REP * s), F32),
            jax.ShapeDtypeStruct((bsz, g, nb, s), BF16),
        ],
        compiler_params=_params(("parallel", "parallel", "arbitrary")),
        name="cmp_topk",
    )(qt, kc, vct, ovt)


def _sel_win_kernel(qt_ref, ket_ref, vst_ref, kw_ref, vwt_ref, selt_ref, oct_ref, gatet_ref,
                    o_ref, rhs_sc, m_sc, acc_sc, sa_sc, sb_sc, ma_sc, mb_sc, sw_sc, mw_sc, ow_sc,
                    *, tq, tk, chunk, nvar, nchunk, wlen):
    t0 = pl.program_id(2) * tq
    qt = _stack_heads(qt_ref)
    for c in range(nchunk):
        selt = selt_ref[0, 0, c * chunk:(c + 1) * chunk, :]
        rhs_sc[c] = jnp.concatenate([_per_head(selt), qt], axis=0)

    m_sc[...] = jnp.full_like(m_sc, NEG)
    acc_sc[...] = jnp.zeros_like(acc_sc)

    cols = NSA_REP * tq
    part_cols = [slice(c * cols // COL_PARTS, (c + 1) * cols // COL_PARTS) for c in range(COL_PARTS)]

    def put_scores(slot, kt, causal, cs):
        s_ref, mx_ref = slot
        k0 = pl.multiple_of(kt * tk, tk)
        s = _dot(ket_ref[0, 0, pl.ds(k0, tk), :], rhs_sc[kt // nvar, :, cs])
        if causal:
            kpos = k0 + lax.broadcasted_iota(jnp.int32, (tk, tq), 0)
            tcol = t0 + lax.broadcasted_iota(jnp.int32, (tk, tq), 1)
            s = s + jnp.tile(jnp.where(kpos <= tcol, 0.0, NEG), (1, NSA_REP // COL_PARTS))
        s_ref[:, cs] = s
        mx_ref[:, cs] = jnp.max(s, axis=0, keepdims=True)

    def accumulate(slot, kt, cs):
        s_ref, mx_ref = slot
        k0 = pl.multiple_of(kt * tk, tk)
        m_old = m_sc[:, cs]
        m_new = jnp.maximum(m_old, mx_ref[:, cs])
        alpha = jnp.exp2(m_old - m_new)
        p = jnp.exp2(s_ref[:, cs] - m_new)
        acc_sc[:, cs] = alpha * acc_sc[:, cs] + _dot(vst_ref[0, 0, :, pl.ds(k0, tk)], p.astype(BF16))
        m_sc[:, cs] = m_new

    def put_all(slot, kt, causal):
        for cs in part_cols:
            put_scores(slot, kt, causal, cs)

    def accumulate_all(slot, kt):
        for cs in part_cols:
            accumulate(slot, kt, cs)

    def stage(dst, kt_new, causal, src, kt_old):
        for cs in part_cols:
            put_scores(dst, kt_new, causal, cs)
            accumulate(src, kt_old, cs)

    slot_a, slot_b = (sa_sc, ma_sc), (sb_sc, mb_sc)
    nfull = (t0 + tq - 1) // tk

    wk = wlen + tq
    w0 = pl.multiple_of(jnp.maximum(t0 - wlen, 0), tq)
    diff = (t0 + lax.broadcasted_iota(jnp.int32, (wk, tq), 1)) - (
        w0 + lax.broadcasted_iota(jnp.int32, (wk, tq), 0))
    wbias = jnp.where(jnp.logical_and(diff >= 0, diff < wlen), 0.0, NEG)
    sw = _dot(kw_ref[0, 0, pl.ds(w0, wk), :], qt) + _per_head(wbias)
    sw_sc[...] = sw
    mw_sc[...] = jnp.max(sw, axis=0, keepdims=True)

    def window_out():
        ew = jnp.exp2(sw_sc[...] - mw_sc[...])
        ow_sc[...] = _dot(vwt_ref[0, 0, :, pl.ds(w0, wk)], ew.astype(BF16))

    @pl.when(nfull == 0)
    def _():
        put_all(slot_a, 0, True)
        window_out()
        accumulate_all(slot_a, 0)

    @pl.when(nfull > 0)
    def _():
        put_all(slot_a, 0, False)
        window_out()

        def body(j, carry):
            stage(slot_b, 2 * j + 1, False, slot_a, 2 * j)
            stage(slot_a, 2 * j + 2, False, slot_b, 2 * j + 1)
            return carry

        lax.fori_loop(0, (nfull - 1) // 2, body, 0)

        @pl.when(nfull % 2 == 1)
        def _():
            stage(slot_b, nfull, True, slot_a, nfull - 1)
            accumulate_all(slot_b, nfull)

        @pl.when(nfull % 2 == 0)
        def _():
            stage(slot_b, nfull - 1, False, slot_a, nfull - 2)
            stage(slot_a, nfull, True, slot_b, nfull - 1)
            accumulate_all(slot_a, nfull)

    o_s = acc_sc[0:HEAD_DIM] * (1.0 / jnp.maximum(acc_sc[HEAD_DIM:HEAD_DIM + 1], TINY))

    o_w = ow_sc[0:HEAD_DIM] * (1.0 / jnp.maximum(ow_sc[HEAD_DIM:HEAD_DIM + 1], TINY))

    gate = gatet_ref[0, 0]
    o_c = oct_ref[0, 0]
    outs = []
    for r in range(NSA_REP):
        cs = slice(r * tq, (r + 1) * tq)
        outs.append(gate[3 * r:3 * r + 1] * o_c[:, cs] + gate[3 * r + 1:3 * r + 2] * o_s[:, cs]
                    + gate[3 * r + 2:3 * r + 3] * o_w[:, cs])
    pairs = [jnp.concatenate(outs[r:r + 2], axis=0).T for r in range(0, NSA_REP, 2)]
    o_ref[0] = jnp.concatenate(pairs, axis=1).astype(o_ref.dtype)


def _sel_win(qt, ket, vst, kw, vwt, selt, oct, gatet, tk):
    bsz, _, hd, s = qt.shape
    g = ket.shape[1]
    nb = selt.shape[2]
    chunk = ket.shape[3] - hd
    nchunk = nb // chunk
    nvar = chunk * SLC_BLOCK // tk
    tq = min(Q_TILE, s)
    cols = NSA_REP * tq
    resident = lambda shape: pl.BlockSpec(shape, lambda b, gg, i: (b, gg, 0, 0),
                                          pipeline_mode=pl.Buffered(1))
    return pl.pallas_call(
        functools.partial(_sel_win_kernel, tq=tq, tk=tk, chunk=chunk, nvar=nvar, nchunk=nchunk,
                          wlen=WINDOW),
        grid=(bsz, g, s // tq),
        in_specs=[
            pl.BlockSpec((1, NSA_REP, hd, tq), lambda b, gg, i: (b, gg, 0, i)),
            resident((1, 1, s, chunk + hd)),
            resident((1, 1, hd + ONES_ROWS, s)),
            resident((1, 1, s, hd)),
            resident((1, 1, hd + ONES_ROWS, s)),
            pl.BlockSpec((1, 1, nb, tq), lambda b, gg, i: (b, gg, 0, i)),
            pl.BlockSpec((1, 1, hd, cols), lambda b, gg, i: (b, gg, 0, i)),
            pl.BlockSpec((1, 1, GATE_ROWS, tq), lambda b, gg, i: (b, gg, 0, i)),
        ],
        out_specs=pl.BlockSpec((1, tq, NSA_REP * hd), lambda b, gg, i: (b, i, gg)),
        out_shape=jax.ShapeDtypeStruct((bsz, s, g * NSA_REP * hd), BF16),
        scratch_shapes=[
            pltpu.VMEM((nchunk, chunk + hd, cols), BF16),
            pltpu.VMEM((1, cols), F32),
            pltpu.VMEM((hd + ONES_ROWS, cols), F32),
            pltpu.VMEM((tk, cols), F32),
            pltpu.VMEM((tk, cols), F32),
            pltpu.VMEM((1, cols), F32),
            pltpu.VMEM((1, cols), F32),
            pltpu.VMEM((WINDOW + tq, cols), F32),
            pltpu.VMEM((1, cols), F32),
            pltpu.VMEM((hd + ONES_ROWS, cols), F32),
        ],
        compiler_params=_params(("parallel", "parallel", "arbitrary")),
        name="sel_win",
    )(qt, ket, vst, kw, vwt, selt, oct, gatet)


def _retention_kernel(q_ref, k_ref, v_ref, gb_ref, cos_ref, sin_ref, dmat_ref, qdec_ref, kdec_ref,
                      cdec_ref, gng_ref, gnb_ref, o_ref, state_sc, *, nchunk, c):
    @pl.when(pl.program_id(2) == 0)
    def _():
        state_sc[...] = jnp.zeros_like(state_sc)

    dmat = dmat_ref[0]
    qdec = qdec_ref[0]
    kdec = kdec_ref[0]
    cdec = cdec_ref[0]
    kscale = RET_DIM ** -0.5
    half = RET_DIM // 2
    for n in range(nchunk):
        rows = pl.ds(n * c, c)
        cos = cos_ref[rows, :]
        sin = sin_ref[rows, :]
        qf = q_ref[0, rows, :].astype(F32)
        kf = k_ref[0, rows, :].astype(F32)
        qr = qf * cos + pltpu.roll(qf, half, 1) * sin
        kr = (kf * cos + pltpu.roll(kf, half, 1) * sin) * kscale
        v = v_ref[0, rows, :]
        inner = _dot_nt(qr.astype(BF16), kr.astype(BF16)) * dmat
        state = state_sc[...]
        y = _dot(inner.astype(BF16), v) + _dot((qr * qdec).astype(BF16), state.astype(BF16))
        state_sc[...] = state * cdec + _dot_tn((kr * kdec).astype(BF16), v)
        mu = jnp.mean(y, axis=-1, keepdims=True)
        d = y - mu
        var = jnp.mean(d * d, axis=-1, keepdims=True)
        yn = d * lax.rsqrt(var + LN_EPS) * gng_ref[...] + gnb_ref[...]
        gb = gb_ref[0, rows, :].astype(F32)
        o_ref[0, rows, :] = ((gb * _sigmoid(gb)) * yn).astype(o_ref.dtype)


def _retention(ret, cos2, sin2, dmat, qdec, kdec, cdec, gng, gnb):
    bsz, s, _ = ret.shape
    h = RET_HEADS
    c = RET_CHUNK
    tc = min(RET_TILE, s)
    spec = lambda off: pl.BlockSpec((1, tc, RET_DIM), lambda b, hh, j, off=off: (b, j, off + hh))
    hspec = lambda shp: pl.BlockSpec((1,) + shp, lambda b, hh, j: (hh, 0, 0))
    return pl.pallas_call(
        functools.partial(_retention_kernel, nchunk=tc // c, c=c),
        grid=(bsz, h, s // tc),
        in_specs=[
            spec(0), spec(h), spec(2 * h), spec(3 * h),
            pl.BlockSpec((tc, RET_DIM), lambda b, hh, j: (j, 0)),
            pl.BlockSpec((tc, RET_DIM), lambda b, hh, j: (j, 0)),
            hspec((c, c)), hspec((c, RET_DIM)), hspec((c, RET_DIM)), hspec((1, RET_DIM)),
            pl.BlockSpec((1, RET_DIM), lambda b, hh, j: (0, hh)),
            pl.BlockSpec((1, RET_DIM), lambda b, hh, j: (0, hh)),
        ],
        out_specs=pl.BlockSpec((1, tc, RET_DIM), lambda b, hh, j: (b, j, hh)),
        out_shape=jax.ShapeDtypeStruct((bsz, s, h * RET_DIM), BF16),
        scratch_shapes=[pltpu.VMEM((RET_DIM, RET_DIM), F32)],
        compiler_params=_params(("parallel", "parallel", "arbitrary")),
        name="retention",
    )(ret, ret, ret, ret, cos2, sin2, dmat, qdec, kdec, cdec, gng, gnb)


def _merge_kernel(h_ref, oa_ref, ob_ref, wga_ref, wgb_ref, wpa_ref, wpb_ref, o_ref):
    hb = h_ref[...]
    ga = _sigmoid(_dot(hb, wga_ref[...]))
    gb = _sigmoid(_dot(hb, wgb_ref[...]))
    merged = ga * _dot(oa_ref[...], wpa_ref[...]) + gb * _dot(ob_ref[...], wpb_ref[...])
    o_ref[...] = merged.astype(o_ref.dtype)


def _merge(hb, oa, ob, wga, wgb, wpa, wpb):
    n, d = hb.shape
    bm = min(PROJ_ROW_TILE, n)
    bn = min(COL_TILE, d)
    ka, kb = oa.shape[1], ob.shape[1]
    return pl.pallas_call(
        _merge_kernel,
        grid=(n // bm, d // bn),
        in_specs=[
            pl.BlockSpec((bm, d), lambda i, j: (i, 0)),
            pl.BlockSpec((bm, ka), lambda i, j: (i, 0)),
            pl.BlockSpec((bm, kb), lambda i, j: (i, 0)),
            pl.BlockSpec((d, bn), lambda i, j: (0, j)),
            pl.BlockSpec((d, bn), lambda i, j: (0, j)),
            pl.BlockSpec((ka, bn), lambda i, j: (0, j)),
            pl.BlockSpec((kb, bn), lambda i, j: (0, j)),
        ],
        out_specs=pl.BlockSpec((bm, bn), lambda i, j: (i, j)),
        out_shape=jax.ShapeDtypeStruct((n, d), BF16),
        compiler_params=_params(("parallel", "arbitrary")),
        name="merge",
    )(hb, oa, ob, wga, wgb, wpa, wpb)


def _proj_ln_kernel(x_ref, m_ref, w_ref, g_ref, b_ref, o_ref, *, alpha):
    y = alpha * x_ref[...] + _dot(m_ref[...], w_ref[...])
    o_ref[...] = _layer_norm_rows(y, g_ref[...], b_ref[...])


def _proj_ln(x, m, w, g, b, alpha):
    n, d = x.shape
    bm = min(ROW_TILE, n)
    return pl.pallas_call(
        functools.partial(_proj_ln_kernel, alpha=alpha),
        grid=(n // bm,),
        in_specs=[
            pl.BlockSpec((bm, d), lambda i: (i, 0)),
            pl.BlockSpec((bm, d), lambda i: (i, 0)),
            pl.BlockSpec((d, d), lambda i: (0, 0)),
            pl.BlockSpec((1, d), lambda i: (0, 0)),
            pl.BlockSpec((1, d), lambda i: (0, 0)),
        ],
        out_specs=pl.BlockSpec((bm, d), lambda i: (i, 0)),
        out_shape=jax.ShapeDtypeStruct((n, d), F32),
        compiler_params=_params(("parallel",)),
        name="proj_ln",
    )(x, m, w, g, b)


def _overlap_matrix(nh, nb):
    c0 = np.arange(nh) * CMP_STRIDE
    c1 = c0 + CMP_BLOCK
    s0 = np.arange(nb) * SLC_BLOCK
    s1 = s0 + SLC_BLOCK
    ov = (c0[:, None] < s1[None, :]) & (c1[:, None] > s0[None, :])
    ov[nh - 1, :] = False
    return jnp.asarray(ov.T, BF16)


def _expand_matrix(nb, s):
    chunk = min(SEL_CHUNK, nb)
    blk_in_chunk = (np.arange(s) // SLC_BLOCK) % chunk
    return jnp.asarray(blk_in_chunk[:, None] == np.arange(chunk)[None, :], BF16)


def _retention_tables(s):
    h, c, d = RET_HEADS, RET_CHUNK, RET_DIM
    inv = ROPE_BASE ** (-jnp.arange(0, d, 2, dtype=F32) / d)
    ang = jnp.arange(s)[:, None].astype(F32) * inv[None, :]
    cos, sin = jnp.cos(ang), jnp.sin(ang)
    cos2 = jnp.concatenate([cos, cos], -1)
    sin2 = jnp.concatenate([-sin, sin], -1)
    log_g = jnp.log1p(-jnp.exp2(-5.0 - jnp.arange(h, dtype=F32)))
    i = jnp.arange(c, dtype=F32)
    diff = i[:, None] - i[None, :]
    dmat = jnp.where(diff >= 0, jnp.exp(jnp.maximum(diff, 0.0)[None] * log_g[:, None, None]), 0.0)
    kdec = jnp.exp((c - 1 - i)[None, :] * log_g[:, None])
    qdec = jnp.exp((i + 1)[None, :] * log_g[:, None])
    cdec = jnp.exp(c * log_g)
    bc = lambda t: jnp.broadcast_to(t[:, :, None], (h, c, d))
    return cos2, sin2, dmat, bc(qdec), bc(kdec), jnp.broadcast_to(cdec[:, None, None], (h, 1, d))


def _pad_cols(w, mult):
    pad = (-w.shape[1]) % mult
    return jnp.pad(w, ((0, 0), (0, pad))) if pad else w


def _mixer(hf, hb, bsz, s, w_in, cmp_k, cmp_v, ret_gn_g, ret_gn_b, w_merge_gate, w_proj_a, w_proj_b,
           w_o, ln_g, ln_b, alpha, tables):
    n, d = hf.shape
    g, hd = NSA_GROUPS, HEAD_DIM
    ovt, expand, ret_tabs = tables
    o_nsa = NSA_Q + 6 * NSA_KV
    w_q = w_in[:, :NSA_Q].astype(BF16)
    w_kv = w_in[:, NSA_Q:o_nsa].astype(BF16)
    w_gate = w_in[:, o_nsa:o_nsa + NSA_GATE].reshape(d, g, 3 * NSA_REP)
    w_gate = jnp.pad(w_gate, ((0, 0), (0, 0), (0, GATE_ROWS - 3 * NSA_REP))).reshape(d, g * GATE_ROWS)
    w_gate = _pad_cols(w_gate, LANE).astype(BF16)
    w_ret = w_in[:, o_nsa + NSA_GATE:].astype(BF16)

    qt, kc_in, vc_in, ket, vst, kw, vwt, gatet = _nsa_proj(hb, w_q, w_kv, w_gate, expand, bsz, s)
    ret = _matmul(hb, w_ret, BF16, "proj_ret")

    nh = s // CMP_STRIDE
    half = CMP_STRIDE * hd

    def compress(t, prm):
        pe, w1, b1, w2 = prm
        return _compress(t.reshape(bsz, g, nh, half), pe.reshape(2, half), w1.astype(BF16),
                         b1.reshape(1, -1), w2.astype(BF16))

    kc = compress(kc_in, cmp_k)
    ones_rows = jnp.zeros((bsz, g, ONES_ROWS, nh), BF16).at[:, :, 0].set(1.0)
    vct = jnp.concatenate([jnp.transpose(compress(vc_in, cmp_v), (0, 1, 3, 2)), ones_rows], axis=2)
    oct, selt = _cmp_topk(qt, kc, vct, ovt)

    o_a = _sel_win(qt, ket, vst, kw, vwt, selt, oct, gatet, min(K_TILE, s))

    o_b = _retention(ret.reshape(bsz, s, 4 * RET_W), *ret_tabs,
                     ret_gn_g.reshape(1, -1), ret_gn_b.reshape(1, -1))

    merged = _merge(hb, o_a.reshape(n, NSA_Q), o_b.reshape(n, RET_W),
                    w_merge_gate[:, :d].astype(BF16), w_merge_gate[:, d:].astype(BF16),
                    w_proj_a.astype(BF16), w_proj_b.astype(BF16))
    return _proj_ln(hf, merged, w_o.astype(BF16), ln_g.reshape(1, -1), ln_b.reshape(1, -1), alpha)


def kernel(x, ffn1_w_gate, ffn1_w_up, ffn1_w_down, ln1_g, ln1_b, w_in, cmp_k_pe, cmp_k_w1, cmp_k_b1,
           cmp_k_w2, cmp_v_pe, cmp_v_w1, cmp_v_b1, cmp_v_w2, ret_gn_g, ret_gn_b, w_merge_gate, w_proj_a,
           w_proj_b, w_o, ln2_g, ln2_b, ffn2_w_gate, ffn2_w_up, ffn2_w_down, ln3_g, ln3_b):
    bsz, s, d = x.shape
    depth = ffn1_w_gate.shape[0]
    f = ffn1_w_gate.shape[2]
    alpha = (2 * depth) ** 0.25
    fpad = (-f) % min(FF_TILE, f)
    nb = s // SLC_BLOCK
    tables = (_overlap_matrix(s // CMP_STRIDE, nb), _expand_matrix(nb, s), _retention_tables(s))

    def ffn_weights(wg, wu, wd):
        bf = min(FF_TILE, f + fpad)
        blocks = lambda w: jnp.pad(w.astype(BF16), ((0, 0), (0, fpad))).reshape(d, -1, bf)
        wgu = jnp.stack([blocks(wg), blocks(wu)], axis=2).reshape(d, 2 * (f + fpad))
        return wgu, jnp.pad(wd.astype(BF16), ((0, fpad), (0, 0)))

    row = lambda t: t.reshape(1, -1)
    xf = x.reshape(bsz * s, d)
    for l in range(depth):
        hf, hb = _ffn_ln(xf, *ffn_weights(ffn1_w_gate[l], ffn1_w_up[l], ffn1_w_down[l]),
                         row(ln1_g[l]), row(ln1_b[l]), alpha, True)
        xf = _mixer(hf, hb, bsz, s, w_in[l],
                    (cmp_k_pe[l], cmp_k_w1[l], cmp_k_b1[l], cmp_k_w2[l]),
                    (cmp_v_pe[l], cmp_v_w1[l], cmp_v_b1[l], cmp_v_w2[l]),
                    ret_gn_g[l], ret_gn_b[l], w_merge_gate[l], w_proj_a[l], w_proj_b[l], w_o[l],
                    ln2_g[l], ln2_b[l], alpha, tables)
        xf, _ = _ffn_ln(xf, *ffn_weights(ffn2_w_gate[l], ffn2_w_up[l], ffn2_w_down[l]),
                        row(ln3_g[l]), row(ln3_b[l]), alpha, False)
    return xf.reshape(bsz, s, d)
```

```python
import functools
import math

import jax
import jax.numpy as jnp
import numpy as np
from jax import lax
from jax.experimental import pallas as pl
from jax.experimental.pallas import tpu as pltpu

NSA_HEADS = 16
NSA_GROUPS = 4
NSA_REP = NSA_HEADS // NSA_GROUPS
HEAD_DIM = 64
CMP_BLOCK = 32
CMP_STRIDE = 16
SLC_BLOCK = 64
SLC_TOPN = 16
N_FORCED = 3
CAUSAL_PARTS = 8
COL_PARTS = 2
SEL_CHUNK = 64
LOOP_STAGES = 4
WINDOW = 512
RET_HEADS = 8
RET_DIM = 128
RET_CHUNK = 128
ROPE_BASE = 10000.0
LN_EPS = 1e-5
NEG = -1e30
TINY = 1e-30
NSA_Q = NSA_HEADS * HEAD_DIM
NSA_KV = NSA_GROUPS * HEAD_DIM
NSA_GATE = 3 * NSA_HEADS
GATE_ROWS = 16
ONES_ROWS = 16
QK_SCALE = HEAD_DIM ** -0.5 * math.log2(math.e)
RET_W = RET_HEADS * RET_DIM

LANE = 128
VMEM_LIMIT = 52 * 1024 * 1024
ROW_TILE = 512
FF_TILE = 512
COL_TILE = 512
PROJ_ROW_TILE = 1024
PROJ_COL_TILE = 1024
Q_TILE = 256
K_TILE = 512
RET_TILE = 1024

BF16 = jnp.bfloat16
F32 = jnp.float32


def _dot(a, b):
    return jnp.dot(a, b, preferred_element_type=F32)


def _dot_nt(a, b):
    return lax.dot_general(a, b, (((1,), (1,)), ((), ())), preferred_element_type=F32)


def _dot_tn(a, b):
    return lax.dot_general(a, b, (((0,), (0,)), ((), ())), preferred_element_type=F32)


def _sigmoid(x):
    return 1.0 / (1.0 + jnp.exp(-x))


def _layer_norm_rows(y, g, b):
    mu = jnp.mean(y, axis=-1, keepdims=True)
    d = y - mu
    var = jnp.mean(d * d, axis=-1, keepdims=True)
    return d * lax.rsqrt(var + LN_EPS) * g + b


def _params(sem):
    return pltpu.CompilerParams(dimension_semantics=sem, vmem_limit_bytes=VMEM_LIMIT)


def _ffn_ln_kernel(x_ref, wg_ref, wu_ref, wd_ref, g_ref, b_ref, *rest, alpha, nf, with_bf16):
    if with_bf16:
        o_ref, ob_ref, xb_sc, acc_sc = rest
    else:
        o_ref, xb_sc, acc_sc = rest
        ob_ref = None
    f = pl.program_id(1)

    @pl.when(f == 0)
    def _():
        xb_sc[...] = x_ref[...].astype(BF16)
        acc_sc[...] = jnp.zeros_like(acc_sc)

    xb = xb_sc[...]
    a = _dot(xb, wg_ref[...])
    u = _dot(xb, wu_ref[...])
    h = (a * _sigmoid(a)) * u
    acc_sc[...] += _dot(h.astype(BF16), wd_ref[...])

    @pl.when(f == nf - 1)
    def _():
        y = alpha * x_ref[...] + 0.5 * acc_sc[...]
        out = _layer_norm_rows(y, g_ref[...], b_ref[...])
        o_ref[...] = out
        if with_bf16:
            ob_ref[...] = out.astype(BF16)


def _ffn_ln(x, wg, wu, wd, g, b, alpha, with_bf16):
    n, d = x.shape
    fp = wg.shape[1]
    bm = min(ROW_TILE, n)
    bf = min(FF_TILE, fp)
    nf = fp // bf
    out_shape = [jax.ShapeDtypeStruct((n, d), F32)]
    out_specs = [pl.BlockSpec((bm, d), lambda i, f: (i, 0))]
    if with_bf16:
        out_shape.append(jax.ShapeDtypeStruct((n, d), BF16))
        out_specs.append(pl.BlockSpec((bm, d), lambda i, f: (i, 0)))
    res = pl.pallas_call(
        functools.partial(_ffn_ln_kernel, alpha=alpha, nf=nf, with_bf16=with_bf16),
        grid=(n // bm, nf),
        in_specs=[
            pl.BlockSpec((bm, d), lambda i, f: (i, 0)),
            pl.BlockSpec((d, bf), lambda i, f: (0, f)),
            pl.BlockSpec((d, bf), lambda i, f: (0, f)),
            pl.BlockSpec((bf, d), lambda i, f: (f, 0)),
            pl.BlockSpec((1, d), lambda i, f: (0, 0)),
            pl.BlockSpec((1, d), lambda i, f: (0, 0)),
        ],
        out_specs=out_specs,
        out_shape=out_shape,
        scratch_shapes=[pltpu.VMEM((bm, d), BF16), pltpu.VMEM((bm, d), F32)],
        compiler_params=_params(("parallel", "arbitrary")),
        name="ffn_ln",
    )(x, wg, wu, wd, g, b)
    return res if with_bf16 else (res[0], None)


def _mm_kernel(x_ref, w_ref, o_ref):
    o_ref[...] = _dot(x_ref[...], w_ref[...]).astype(o_ref.dtype)


def _matmul(x, w, out_dtype, name):
    n, k = x.shape
    nout = w.shape[1]
    bm = min(PROJ_ROW_TILE, n)
    bn = next(c for c in (PROJ_COL_TILE, COL_TILE, 2 * LANE, LANE, nout) if nout % c == 0)
    assert n % bm == 0
    return pl.pallas_call(
        _mm_kernel,
        grid=(n // bm, nout // bn),
        in_specs=[
            pl.BlockSpec((bm, k), lambda i, j: (i, 0)),
            pl.BlockSpec((k, bn), lambda i, j: (0, j)),
        ],
        out_specs=pl.BlockSpec((bm, bn), lambda i, j: (i, j)),
        out_shape=jax.ShapeDtypeStruct((n, nout), out_dtype),
        compiler_params=_params(("parallel", "arbitrary")),
        name=name,
    )(x, w)


def _nsa_proj_kernel(x_ref, wq_ref, wkv_ref, wg_ref, exp_ref, qt_ref, kc_ref, vc_ref, ket_ref, vst_ref,
                     kw_ref, vwt_ref, gt_ref, *, chunk):
    x = x_ref[...]
    hd, g = HEAD_DIM, NSA_GROUPS
    pair = 2 * hd
    tg = _sigmoid(_dot(x, wg_ref[...])).T
    for gg in range(g):
        gt_ref[0, gg] = tg[gg * GATE_ROWS:(gg + 1) * GATE_ROWS]
    yq = _dot(x, wq_ref[...]) * QK_SCALE
    for p in range(NSA_HEADS // 2):
        t = yq[:, p * pair:(p + 1) * pair].T.astype(qt_ref.dtype)
        qt_ref[0, 2 * p] = t[0:hd]
        qt_ref[0, 2 * p + 1] = t[hd:pair]
    ykv = _dot(x, wkv_ref[...])
    col = lambda kind, gg: (kind * g + gg) * hd
    ones = jnp.where(lax.broadcasted_iota(jnp.int32, (ONES_ROWS, x.shape[0]), 0) == 0, 1.0, 0.0)
    for gg in range(g):
        rows = lambda kind: ykv[:, col(kind, gg):col(kind, gg) + hd]
        kc_ref[0, gg] = rows(0).astype(kc_ref.dtype)
        vc_ref[0, gg] = rows(1).astype(vc_ref.dtype)
        ket_ref[0, gg, :, 0:chunk] = exp_ref[...]
        ket_ref[0, gg, :, chunk:chunk + hd] = rows(2).astype(ket_ref.dtype)
        kw_ref[0, gg] = rows(4).astype(kw_ref.dtype)
    for kind, out_ref in ((3, vst_ref), (5, vwt_ref)):
        for gg in range(0, g, 2):
            t = ykv[:, col(kind, gg):col(kind, gg) + pair].T.astype(out_ref.dtype)
            for k in range(2):
                out_ref[0, gg + k, 0:hd] = t[k * hd:(k + 1) * hd]
                out_ref[0, gg + k, hd:hd + ONES_ROWS] = ones.astype(out_ref.dtype)


def _nsa_proj(hb, w_q, w_kv, w_g, expand, bsz, s):
    n, d = hb.shape
    g, hd = NSA_GROUPS, HEAD_DIM
    chunk = expand.shape[1]
    bm = min(ROW_TILE, s)
    nj = s // bm
    rows_spec = lambda w: pl.BlockSpec((1, g, bm, w), lambda b, j: (b, 0, j, 0))
    cols_spec = lambda h, r: pl.BlockSpec((1, h, r, bm), lambda b, j: (b, 0, 0, j))
    full = lambda shape: pl.BlockSpec(shape, lambda b, j: (0, 0), pipeline_mode=pl.Buffered(1))
    rows_shape = lambda w: jax.ShapeDtypeStruct((bsz, g, s, w), BF16)
    cols_shape = lambda h, r: jax.ShapeDtypeStruct((bsz, h, r, s), BF16)
    return pl.pallas_call(
        functools.partial(_nsa_proj_kernel, chunk=chunk),
        grid=(bsz, nj),
        in_specs=[
            pl.BlockSpec((bm, d), lambda b, j: (b * nj + j, 0)),
            full(w_q.shape), full(w_kv.shape), full(w_g.shape),
            pl.BlockSpec((bm, chunk), lambda b, j: (j, 0)),
        ],
        out_specs=[cols_spec(NSA_HEADS, hd), rows_spec(hd), rows_spec(hd), rows_spec(chunk + hd),
                   cols_spec(g, hd + ONES_ROWS), rows_spec(hd), cols_spec(g, hd + ONES_ROWS),
                   cols_spec(g, GATE_ROWS)],
        out_shape=[cols_shape(NSA_HEADS, hd), rows_shape(hd), rows_shape(hd), rows_shape(chunk + hd),
                   cols_shape(g, hd + ONES_ROWS), rows_shape(hd), cols_shape(g, hd + ONES_ROWS),
                   jax.ShapeDtypeStruct((bsz, g, GATE_ROWS, s), F32)],
        compiler_params=_params(("parallel", "arbitrary")),
        name="nsa_proj",
    )(hb, w_q, w_kv, w_g, expand)


def _compress_kernel(x_ref, pe_ref, w1_ref, b1_ref, w2_ref, o_ref, *, nh, half):
    x = x_ref[0, 0].astype(F32)
    xa = (x + pe_ref[0:1, :]).astype(BF16)
    xb = (x + pe_ref[1:2, :]).astype(BF16)
    ha = _dot(xa, w1_ref[0:half, :])
    hb = _dot(xb, w1_ref[half:2 * half, :])
    hid = ha + pltpu.roll(hb, nh - 1, 0) + b1_ref[...]
    c = math.sqrt(2.0 / math.pi)
    act = 0.5 * hid * (1.0 + jnp.tanh(c * (hid + 0.044715 * (hid * hid * hid))))
    o_ref[0, 0] = _dot(act.astype(BF16), w2_ref[...]).astype(o_ref.dtype)


def _compress(x, pe2, w1, b1, w2):
    bsz, g, nh, half = x.shape
    hid = w1.shape[1]
    hd = w2.shape[1]
    return pl.pallas_call(
        functools.partial(_compress_kernel, nh=nh, half=half),
        grid=(bsz, g),
        in_specs=[
            pl.BlockSpec((1, 1, nh, half), lambda b, gg: (b, gg, 0, 0)),
            pl.BlockSpec((2, half), lambda b, gg: (0, 0)),
            pl.BlockSpec((2 * half, hid), lambda b, gg: (0, 0)),
            pl.BlockSpec((1, hid), lambda b, gg: (0, 0)),
            pl.BlockSpec((hid, hd), lambda b, gg: (0, 0)),
        ],
        out_specs=pl.BlockSpec((1, 1, nh, hd), lambda b, gg: (b, gg, 0, 0)),
        out_shape=jax.ShapeDtypeStruct((bsz, g, nh, hd), BF16),
        compiler_params=_params(("parallel", "parallel")),
        name="compress",
    )(x, pe2, w1, b1, w2)


def _stack_heads(qt_ref):
    return jnp.concatenate([qt_ref[0, r] for r in range(NSA_REP)], axis=1)


def _per_head(x):
    return jnp.tile(x, (1, NSA_REP))


def _cmp_topk_kernel(qt_ref, kc_ref, vct_ref, ovt_ref, oct_ref, selt_ref, *, tq, nh, nb, n_sel, nq):
    i = pl.program_id(2)
    for part in range(CAUSAL_PARTS):
        lo, hi = part * nq // CAUSAL_PARTS, (part + 1) * nq // CAUSAL_PARTS

        @pl.when(jnp.logical_and(i >= lo, i < hi))
        def _(part=part):
            _cmp_topk_part(qt_ref, kc_ref, vct_ref, ovt_ref, oct_ref, selt_ref, i * tq, tq=tq,
                           nh=(part + 1) * nh // CAUSAL_PARTS, nb=(part + 1) * nb // CAUSAL_PARTS,
                           nb_all=nb, n_sel=n_sel)


def _cmp_topk_part(qt_ref, kc_ref, vct_ref, ovt_ref, oct_ref, selt_ref, t0, *, tq, nh, nb, nb_all, n_sel):
    cend = lax.broadcasted_iota(jnp.int32, (nh, tq), 0) * CMP_STRIDE + (CMP_BLOCK - 1)
    tpos = t0 + lax.broadcasted_iota(jnp.int32, (nh, tq), 1)
    bias = jnp.where(cend <= tpos, 0.0, NEG)
    seen = jnp.where(t0 + lax.broadcasted_iota(jnp.int32, (1, tq), 1) >= CMP_BLOCK - 1, 1.0, 0.0)
    qt = _stack_heads(qt_ref)
    s = _dot(kc_ref[0, 0, 0:nh, :], qt) + _per_head(bias)
    e = jnp.exp2(s - jnp.max(s, axis=0, keepdims=True))
    ea = _dot(vct_ref[0, 0, :, 0:nh], e.astype(BF16))
    inv = _per_head(seen) / jnp.maximum(ea[HEAD_DIM:HEAD_DIM + 1], TINY)
    oct_ref[0, 0] = ea[0:HEAD_DIM] * inv
    p = e * inv
    psum = p[:, 0:tq]
    for r in range(1, NSA_REP):
        psum = psum + p[:, r * tq:(r + 1) * tq]
    imp = _dot(ovt_ref[0:nb, 0:nh], psum.astype(BF16))
    blk = lax.broadcasted_iota(jnp.int32, (nb, tq), 0).astype(F32)
    cur = ((t0 + lax.broadcasted_iota(jnp.int32, (nb, tq), 1)) // SLC_BLOCK).astype(F32)
    forced = (blk == 0.0) | (blk == cur) | (blk == cur - 1.0)
    x = jnp.where(forced, -jnp.inf, jnp.where(blk <= cur, imp, NEG))
    for _ in range(n_sel - N_FORCED):
        mx = jnp.max(x, axis=0, keepdims=True)
        idx = jnp.min(jnp.where(x == mx, blk, float(nb)), axis=0, keepdims=True)
        x = jnp.where(blk == idx, -jnp.inf, x)
    selb = jnp.where(x == -jnp.inf, 0.0, NEG)
    selt_ref[0, 0, 0:nb, :] = selb.astype(selt_ref.dtype)
    if nb < nb_all:
        selt_ref[0, 0, nb:nb_all, :] = jnp.full((nb_all - nb, tq), NEG, selt_ref.dtype)


def _cmp_topk(qt, kc, vct, ovt):
    bsz, _, hd, s = qt.shape
    g, nh = kc.shape[1], kc.shape[2]
    nb = ovt.shape[0]
    assert nb >= SLC_TOPN and nh % CAUSAL_PARTS == 0 and nb % CAUSAL_PARTS == 0
    tq = min(Q_TILE, s)
    cols = NSA_REP * tq
    return pl.pallas_call(
        functools.partial(_cmp_topk_kernel, tq=tq, nh=nh, nb=nb, n_sel=SLC_TOPN, nq=s // tq),
        grid=(bsz, g, s // tq),
        in_specs=[
            pl.BlockSpec((1, NSA_REP, hd, tq), lambda b, gg, i: (b, gg, 0, i)),
            pl.BlockSpec((1, 1, nh, hd), lambda b, gg, i: (b, gg, 0, 0)),
            pl.BlockSpec((1, 1, hd + ONES_ROWS, nh), lambda b, gg, i: (b, gg, 0, 0)),
            pl.BlockSpec((nb, nh), lambda b, gg, i: (0, 0)),
        ],
        out_specs=[
            pl.BlockSpec((1, 1, hd, cols), lambda b, gg, i: (b, gg, 0, i)),
            pl.BlockSpec((1, 1, nb, tq), lambda b, gg, i: (b, gg, 0, i)),
        ],
        out_shape=[
            jax.ShapeDtypeStruct((bsz, g, hd, NSA_REP * s), F32),
            jax.ShapeDtypeStruct((bsz, g, nb, s), BF16),
        ],
        compiler_params=_params(("parallel", "parallel", "arbitrary")),
        name="cmp_topk",
    )(qt, kc, vct, ovt)


def _sel_win_kernel(qt_ref, ket_ref, vst_ref, kw_ref, vwt_ref, selt_ref, oct_ref, gatet_ref,
                    o_ref, rhs_sc, m_sc, acc_sc, sa_sc, sb_sc, ma_sc, mb_sc, sw_sc, mw_sc, ow_sc,
                    *, tq, tk, chunk, nvar, nchunk, wlen):
    t0 = pl.program_id(2) * tq
    qt = _stack_heads(qt_ref)
    for c in range(nchunk):
        selt = selt_ref[0, 0, c * chunk:(c + 1) * chunk, :]
        rhs_sc[c] = jnp.concatenate([_per_head(selt), qt], axis=0)

    m_sc[...] = jnp.full_like(m_sc, NEG)
    acc_sc[...] = jnp.zeros_like(acc_sc)

    cols = NSA_REP * tq
    part_cols = [slice(c * cols // COL_PARTS, (c + 1) * cols // COL_PARTS) for c in range(COL_PARTS)]

    def put_scores(slot, kt, causal, cs):
        s_ref, mx_ref = slot
        k0 = pl.multiple_of(kt * tk, tk)
        s = _dot(ket_ref[0, 0, pl.ds(k0, tk), :], rhs_sc[kt // nvar, :, cs])
        if causal:
            kpos = k0 + lax.broadcasted_iota(jnp.int32, (tk, tq), 0)
            tcol = t0 + lax.broadcasted_iota(jnp.int32, (tk, tq), 1)
            s = s + jnp.tile(jnp.where(kpos <= tcol, 0.0, NEG), (1, NSA_REP // COL_PARTS))
        s_ref[:, cs] = s
        mx_ref[:, cs] = jnp.max(s, axis=0, keepdims=True)

    def accumulate(slot, kt, cs):
        s_ref, mx_ref = slot
        k0 = pl.multiple_of(kt * tk, tk)
        m_old = m_sc[:, cs]
        m_new = jnp.maximum(m_old, mx_ref[:, cs])
        alpha = jnp.exp2(m_old - m_new)
        p = jnp.exp2(s_ref[:, cs] - m_new)
        acc_sc[:, cs] = alpha * acc_sc[:, cs] + _dot(vst_ref[0, 0, :, pl.ds(k0, tk)], p.astype(BF16))
        m_sc[:, cs] = m_new

    def put_all(slot, kt, causal):
        for cs in part_cols:
            put_scores(slot, kt, causal, cs)

    def accumulate_all(slot, kt):
        for cs in part_cols:
            accumulate(slot, kt, cs)

    def stage(dst, kt_new, causal, src, kt_old):
        for cs in part_cols:
            put_scores(dst, kt_new, causal, cs)
            accumulate(src, kt_old, cs)

    slot_a, slot_b = (sa_sc, ma_sc), (sb_sc, mb_sc)
    nfull = (t0 + tq - 1) // tk

    wk = wlen + tq
    w0 = pl.multiple_of(jnp.maximum(t0 - wlen, 0), tq)
    diff = (t0 + lax.broadcasted_iota(jnp.int32, (wk, tq), 1)) - (
        w0 + lax.broadcasted_iota(jnp.int32, (wk, tq), 0))
    wbias = jnp.where(jnp.logical_and(diff >= 0, diff < wlen), 0.0, NEG)
    sw = _dot(kw_ref[0, 0, pl.ds(w0, wk), :], qt) + _per_head(wbias)
    sw_sc[...] = sw
    mw_sc[...] = jnp.max(sw, axis=0, keepdims=True)

    def window_out():
        ew = jnp.exp2(sw_sc[...] - mw_sc[...])
        ow_sc[...] = _dot(vwt_ref[0, 0, :, pl.ds(w0, wk)], ew.astype(BF16))

    @pl.when(nfull == 0)
    def _():
        put_all(slot_a, 0, True)
        window_out()
        accumulate_all(slot_a, 0)

    @pl.when(nfull > 0)
    def _():
        put_all(slot_a, 0, False)
        window_out()

        slots = (slot_a, slot_b)

        def stages(first, count, last_causal):
            for u in range(count):
                stage(slots[(1 + u) % 2], first + u, last_causal and u == count - 1,
                      slots[u % 2], first + u - 1)

        def body(j, carry):
            stages(LOOP_STAGES * j + 1, LOOP_STAGES, False)
            return carry

        nloop = (nfull - 1) // LOOP_STAGES
        lax.fori_loop(0, nloop, body, 0)
        done = LOOP_STAGES * nloop
        for rest in range(LOOP_STAGES):

            @pl.when(nfull - 1 - done == rest)
            def _(rest=rest):
                stages(done + 1, rest + 1, True)
                accumulate_all(slots[(1 + rest) % 2], nfull)

    o_s = acc_sc[0:HEAD_DIM] * (1.0 / jnp.maximum(acc_sc[HEAD_DIM:HEAD_DIM + 1], TINY))

    o_w = ow_sc[0:HEAD_DIM] * (1.0 / jnp.maximum(ow_sc[HEAD_DIM:HEAD_DIM + 1], TINY))

    gate = gatet_ref[0, 0]
    o_c = oct_ref[0, 0]
    outs = []
    for r in range(NSA_REP):
        cs = slice(r * tq, (r + 1) * tq)
        outs.append(gate[3 * r:3 * r + 1] * o_c[:, cs] + gate[3 * r + 1:3 * r + 2] * o_s[:, cs]
                    + gate[3 * r + 2:3 * r + 3] * o_w[:, cs])
    pairs = [jnp.concatenate(outs[r:r + 2], axis=0).T for r in range(0, NSA_REP, 2)]
    o_ref[0] = jnp.concatenate(pairs, axis=1).astype(o_ref.dtype)


def _sel_win(qt, ket, vst, kw, vwt, selt, oct, gatet, tk):
    bsz, _, hd, s = qt.shape
    g = ket.shape[1]
    nb = selt.shape[2]
    chunk = ket.shape[3] - hd
    nchunk = nb // chunk
    nvar = chunk * SLC_BLOCK // tk
    tq = min(Q_TILE, s)
    cols = NSA_REP * tq
    resident = lambda shape: pl.BlockSpec(shape, lambda b, gg, i: (b, gg, 0, 0),
                                          pipeline_mode=pl.Buffered(1))
    return pl.pallas_call(
        functools.partial(_sel_win_kernel, tq=tq, tk=tk, chunk=chunk, nvar=nvar, nchunk=nchunk,
                          wlen=WINDOW),
        grid=(bsz, g, s // tq),
        in_specs=[
            pl.BlockSpec((1, NSA_REP, hd, tq), lambda b, gg, i: (b, gg, 0, i)),
            resident((1, 1, s, chunk + hd)),
            resident((1, 1, hd + ONES_ROWS, s)),
            resident((1, 1, s, hd)),
            resident((1, 1, hd + ONES_ROWS, s)),
            pl.BlockSpec((1, 1, nb, tq), lambda b, gg, i: (b, gg, 0, i)),
            pl.BlockSpec((1, 1, hd, cols), lambda b, gg, i: (b, gg, 0, i)),
            pl.BlockSpec((1, 1, GATE_ROWS, tq), lambda b, gg, i: (b, gg, 0, i)),
        ],
        out_specs=pl.BlockSpec((1, tq, NSA_REP * hd), lambda b, gg, i: (b, i, gg)),
        out_shape=jax.ShapeDtypeStruct((bsz, s, g * NSA_REP * hd), BF16),
        scratch_shapes=[
            pltpu.VMEM((nchunk, chunk + hd, cols), BF16),
            pltpu.VMEM((1, cols), F32),
            pltpu.VMEM((hd + ONES_ROWS, cols), F32),
            pltpu.VMEM((tk, cols), F32),
            pltpu.VMEM((tk, cols), F32),
            pltpu.VMEM((1, cols), F32),
            pltpu.VMEM((1, cols), F32),
            pltpu.VMEM((WINDOW + tq, cols), F32),
            pltpu.VMEM((1, cols), F32),
            pltpu.VMEM((hd + ONES_ROWS, cols), F32),
        ],
        compiler_params=_params(("parallel", "parallel", "arbitrary")),
        name="sel_win",
    )(qt, ket, vst, kw, vwt, selt, oct, gatet)


def _retention_kernel(q_ref, k_ref, v_ref, gb_ref, cos_ref, sin_ref, dmat_ref, qdec_ref, kdec_ref,
                      cdec_ref, gng_ref, gnb_ref, o_ref, state_sc, *, nchunk, c):
    @pl.when(pl.program_id(2) == 0)
    def _():
        state_sc[...] = jnp.zeros_like(state_sc)

    dmat = dmat_ref[0]
    qdec = qdec_ref[0]
    kdec = kdec_ref[0]
    cdec = cdec_ref[0]
    kscale = RET_DIM ** -0.5
    half = RET_DIM // 2
    for n in range(nchunk):
        rows = pl.ds(n * c, c)
        cos = cos_ref[rows, :]
        sin = sin_ref[rows, :]
        qf = q_ref[0, rows, :].astype(F32)
        kf = k_ref[0, rows, :].astype(F32)
        qr = qf * cos + pltpu.roll(qf, half, 1) * sin
        kr = (kf * cos + pltpu.roll(kf, half, 1) * sin) * kscale
        v = v_ref[0, rows, :]
        inner = _dot_nt(qr.astype(BF16), kr.astype(BF16)) * dmat
        state = state_sc[...]
        y = _dot(inner.astype(BF16), v) + _dot((qr * qdec).astype(BF16), state.astype(BF16))
        state_sc[...] = state * cdec + _dot_tn((kr * kdec).astype(BF16), v)
        mu = jnp.mean(y, axis=-1, keepdims=True)
        d = y - mu
        var = jnp.mean(d * d, axis=-1, keepdims=True)
        yn = d * lax.rsqrt(var + LN_EPS) * gng_ref[...] + gnb_ref[...]
        gb = gb_ref[0, rows, :].astype(F32)
        o_ref[0, rows, :] = ((gb * _sigmoid(gb)) * yn).astype(o_ref.dtype)


def _retention(ret, cos2, sin2, dmat, qdec, kdec, cdec, gng, gnb):
    bsz, s, _ = ret.shape
    h = RET_HEADS
    c = RET_CHUNK
    tc = min(RET_TILE, s)
    spec = lambda off: pl.BlockSpec((1, tc, RET_DIM), lambda b, hh, j, off=off: (b, j, off + hh))
    hspec = lambda shp: pl.BlockSpec((1,) + shp, lambda b, hh, j: (hh, 0, 0))
    return pl.pallas_call(
        functools.partial(_retention_kernel, nchunk=tc // c, c=c),
        grid=(bsz, h, s // tc),
        in_specs=[
            spec(0), spec(h), spec(2 * h), spec(3 * h),
            pl.BlockSpec((tc, RET_DIM), lambda b, hh, j: (j, 0)),
            pl.BlockSpec((tc, RET_DIM), lambda b, hh, j: (j, 0)),
            hspec((c, c)), hspec((c, RET_DIM)), hspec((c, RET_DIM)), hspec((1, RET_DIM)),
            pl.BlockSpec((1, RET_DIM), lambda b, hh, j: (0, hh)),
            pl.BlockSpec((1, RET_DIM), lambda b, hh, j: (0, hh)),
        ],
        out_specs=pl.BlockSpec((1, tc, RET_DIM), lambda b, hh, j: (b, j, hh)),
        out_shape=jax.ShapeDtypeStruct((bsz, s, h * RET_DIM), BF16),
        scratch_shapes=[pltpu.VMEM((RET_DIM, RET_DIM), F32)],
        compiler_params=_params(("parallel", "parallel", "arbitrary")),
        name="retention",
    )(ret, ret, ret, ret, cos2, sin2, dmat, qdec, kdec, cdec, gng, gnb)


def _merge_kernel(h_ref, oa_ref, ob_ref, wga_ref, wgb_ref, wpa_ref, wpb_ref, o_ref):
    hb = h_ref[...]
    ga = _sigmoid(_dot(hb, wga_ref[...]))
    gb = _sigmoid(_dot(hb, wgb_ref[...]))
    merged = ga * _dot(oa_ref[...], wpa_ref[...]) + gb * _dot(ob_ref[...], wpb_ref[...])
    o_ref[...] = merged.astype(o_ref.dtype)


def _merge(hb, oa, ob, wga, wgb, wpa, wpb):
    n, d = hb.shape
    bm = min(PROJ_ROW_TILE, n)
    bn = min(COL_TILE, d)
    ka, kb = oa.shape[1], ob.shape[1]
    return pl.pallas_call(
        _merge_kernel,
        grid=(n // bm, d // bn),
        in_specs=[
            pl.BlockSpec((bm, d), lambda i, j: (i, 0)),
            pl.BlockSpec((bm, ka), lambda i, j: (i, 0)),
            pl.BlockSpec((bm, kb), lambda i, j: (i, 0)),
            pl.BlockSpec((d, bn), lambda i, j: (0, j)),
            pl.BlockSpec((d, bn), lambda i, j: (0, j)),
            pl.BlockSpec((ka, bn), lambda i, j: (0, j)),
            pl.BlockSpec((kb, bn), lambda i, j: (0, j)),
        ],
        out_specs=pl.BlockSpec((bm, bn), lambda i, j: (i, j)),
        out_shape=jax.ShapeDtypeStruct((n, d), BF16),
        compiler_params=_params(("parallel", "arbitrary")),
        name="merge",
    )(hb, oa, ob, wga, wgb, wpa, wpb)


def _proj_ln_kernel(x_ref, m_ref, w_ref, g_ref, b_ref, o_ref, *, alpha):
    y = alpha * x_ref[...] + _dot(m_ref[...], w_ref[...])
    o_ref[...] = _layer_norm_rows(y, g_ref[...], b_ref[...])


def _proj_ln(x, m, w, g, b, alpha):
    n, d = x.shape
    bm = min(ROW_TILE, n)
    return pl.pallas_call(
        functools.partial(_proj_ln_kernel, alpha=alpha),
        grid=(n // bm,),
        in_specs=[
            pl.BlockSpec((bm, d), lambda i: (i, 0)),
            pl.BlockSpec((bm, d), lambda i: (i, 0)),
            pl.BlockSpec((d, d), lambda i: (0, 0)),
            pl.BlockSpec((1, d), lambda i: (0, 0)),
            pl.BlockSpec((1, d), lambda i: (0, 0)),
        ],
        out_specs=pl.BlockSpec((bm, d), lambda i: (i, 0)),
        out_shape=jax.ShapeDtypeStruct((n, d), F32),
        compiler_params=_params(("parallel",)),
        name="proj_ln",
    )(x, m, w, g, b)


def _overlap_matrix(nh, nb):
    c0 = np.arange(nh) * CMP_STRIDE
    c1 = c0 + CMP_BLOCK
    s0 = np.arange(nb) * SLC_BLOCK
    s1 = s0 + SLC_BLOCK
    ov = (c0[:, None] < s1[None, :]) & (c1[:, None] > s0[None, :])
    ov[nh - 1, :] = False
    return jnp.asarray(ov.T, BF16)


def _expand_matrix(nb, s):
    chunk = min(SEL_CHUNK, nb)
    blk_in_chunk = (np.arange(s) // SLC_BLOCK) % chunk
    return jnp.asarray(blk_in_chunk[:, None] == np.arange(chunk)[None, :], BF16)


def _retention_tables(s):
    h, c, d = RET_HEADS, RET_CHUNK, RET_DIM
    inv = ROPE_BASE ** (-jnp.arange(0, d, 2, dtype=F32) / d)
    ang = jnp.arange(s)[:, None].astype(F32) * inv[None, :]
    cos, sin = jnp.cos(ang), jnp.sin(ang)
    cos2 = jnp.concatenate([cos, cos], -1)
    sin2 = jnp.concatenate([-sin, sin], -1)
    log_g = jnp.log1p(-jnp.exp2(-5.0 - jnp.arange(h, dtype=F32)))
    i = jnp.arange(c, dtype=F32)
    diff = i[:, None] - i[None, :]
    dmat = jnp.where(diff >= 0, jnp.exp(jnp.maximum(diff, 0.0)[None] * log_g[:, None, None]), 0.0)
    kdec = jnp.exp((c - 1 - i)[None, :] * log_g[:, None])
    qdec = jnp.exp((i + 1)[None, :] * log_g[:, None])
    cdec = jnp.exp(c * log_g)
    bc = lambda t: jnp.broadcast_to(t[:, :, None], (h, c, d))
    return cos2, sin2, dmat, bc(qdec), bc(kdec), jnp.broadcast_to(cdec[:, None, None], (h, 1, d))


def _pad_cols(w, mult):
    pad = (-w.shape[1]) % mult
    return jnp.pad(w, ((0, 0), (0, pad))) if pad else w


def _mixer(hf, hb, bsz, s, w_in, cmp_k, cmp_v, ret_gn_g, ret_gn_b, w_merge_gate, w_proj_a, w_proj_b,
           w_o, ln_g, ln_b, alpha, tables):
    n, d = hf.shape
    g, hd = NSA_GROUPS, HEAD_DIM
    ovt, expand, ret_tabs = tables
    o_nsa = NSA_Q + 6 * NSA_KV
    w_q = w_in[:, :NSA_Q].astype(BF16)
    w_kv = w_in[:, NSA_Q:o_nsa].astype(BF16)
    w_gate = w_in[:, o_nsa:o_nsa + NSA_GATE].reshape(d, g, 3 * NSA_REP)
    w_gate = jnp.pad(w_gate, ((0, 0), (0, 0), (0, GATE_ROWS - 3 * NSA_REP))).reshape(d, g * GATE_ROWS)
    w_gate = _pad_cols(w_gate, LANE).astype(BF16)
    w_ret = w_in[:, o_nsa + NSA_GATE:].astype(BF16)

    qt, kc_in, vc_in, ket, vst, kw, vwt, gatet = _nsa_proj(hb, w_q, w_kv, w_gate, expand, bsz, s)
    ret = _matmul(hb, w_ret, BF16, "proj_ret")

    nh = s // CMP_STRIDE
    half = CMP_STRIDE * hd

    def compress(t, prm):
        pe, w1, b1, w2 = prm
        return _compress(t.reshape(bsz, g, nh, half), pe.reshape(2, half), w1.astype(BF16),
                         b1.reshape(1, -1), w2.astype(BF16))

    kc = compress(kc_in, cmp_k)
    ones_rows = jnp.zeros((bsz, g, ONES_ROWS, nh), BF16).at[:, :, 0].set(1.0)
    vct = jnp.concatenate([jnp.transpose(compress(vc_in, cmp_v), (0, 1, 3, 2)), ones_rows], axis=2)
    oct, selt = _cmp_topk(qt, kc, vct, ovt)

    o_a = _sel_win(qt, ket, vst, kw, vwt, selt, oct, gatet, min(K_TILE, s))

    o_b = _retention(ret.reshape(bsz, s, 4 * RET_W), *ret_tabs,
                     ret_gn_g.reshape(1, -1), ret_gn_b.reshape(1, -1))

    merged = _merge(hb, o_a.reshape(n, NSA_Q), o_b.reshape(n, RET_W),
                    w_merge_gate[:, :d].astype(BF16), w_merge_gate[:, d:].astype(BF16),
                    w_proj_a.astype(BF16), w_proj_b.astype(BF16))
    return _proj_ln(hf, merged, w_o.astype(BF16), ln_g.reshape(1, -1), ln_b.reshape(1, -1), alpha)


def kernel(x, ffn1_w_gate, ffn1_w_up, ffn1_w_down, ln1_g, ln1_b, w_in, cmp_k_pe, cmp_k_w1, cmp_k_b1,
           cmp_k_w2, cmp_v_pe, cmp_v_w1, cmp_v_b1, cmp_v_w2, ret_gn_g, ret_gn_b, w_merge_gate, w_proj_a,
           w_proj_b, w_o, ln2_g, ln2_b, ffn2_w_gate, ffn2_w_up, ffn2_w_down, ln3_g, ln3_b):
    bsz, s, d = x.shape
    depth = ffn1_w_gate.shape[0]
    f = ffn1_w_gate.shape[2]
    alpha = (2 * depth) ** 0.25
    fpad = (-f) % min(FF_TILE, f)
    nb = s // SLC_BLOCK
    tables = (_overlap_matrix(s // CMP_STRIDE, nb), _expand_matrix(nb, s), _retention_tables(s))

    def ffn_weights(wg, wu, wd):
        return (jnp.pad(wg.astype(BF16), ((0, 0), (0, fpad))), jnp.pad(wu.astype(BF16), ((0, 0), (0, fpad))),
                jnp.pad(wd.astype(BF16), ((0, fpad), (0, 0))))

    row = lambda t: t.reshape(1, -1)
    xf = x.reshape(bsz * s, d)
    for l in range(depth):
        hf, hb = _ffn_ln(xf, *ffn_weights(ffn1_w_gate[l], ffn1_w_up[l], ffn1_w_down[l]),
                         row(ln1_g[l]), row(ln1_b[l]), alpha, True)
        xf = _mixer(hf, hb, bsz, s, w_in[l],
                    (cmp_k_pe[l], cmp_k_w1[l], cmp_k_b1[l], cmp_k_w2[l]),
                    (cmp_v_pe[l], cmp_v_w1[l], cmp_v_b1[l], cmp_v_w2[l]),
                    ret_gn_g[l], ret_gn_b[l], w_merge_gate[l], w_proj_a[l], w_proj_b[l], w_o[l],
                    ln2_g[l], ln2_b[l], alpha, tables)
        xf, _ = _ffn_ln(xf, *ffn_weights(ffn2_w_gate[l], ffn2_w_up[l], ffn2_w_down[l]),
                        row(ln3_g[l]), row(ln3_b[l]), alpha, False)
    return xf.reshape(bsz, s, d)
```

```python
import functools
import math

import jax
import jax.numpy as jnp
import numpy as np
from jax import lax
from jax.experimental import pallas as pl
from jax.experimental.pallas import tpu as pltpu

NSA_HEADS = 16
NSA_GROUPS = 4
NSA_REP = NSA_HEADS // NSA_GROUPS
HEAD_DIM = 64
CMP_BLOCK = 32
CMP_STRIDE = 16
SLC_BLOCK = 64
SLC_TOPN = 16
N_FORCED = 3
CAUSAL_PARTS = 8
COL_PARTS = 2
SEL_CHUNK = 64
LOOP_STAGES = 4
WINDOW = 512
RET_HEADS = 8
RET_DIM = 128
RET_CHUNK = 128
ROPE_BASE = 10000.0
LN_EPS = 1e-5
NEG = -1e30
TINY = 1e-30
NSA_Q = NSA_HEADS * HEAD_DIM
NSA_KV = NSA_GROUPS * HEAD_DIM
NSA_GATE = 3 * NSA_HEADS
GATE_ROWS = 16
ONES_ROWS = 16
QK_SCALE = HEAD_DIM ** -0.5 * math.log2(math.e)
RET_W = RET_HEADS * RET_DIM

LANE = 128
VMEM_LIMIT = 52 * 1024 * 1024
ROW_TILE = 512
FF_TILE = 512
COL_TILE = 512
PROJ_ROW_TILE = 1024
PROJ_COL_TILE = 1024
Q_TILE = 256
K_TILE = 512
RET_TILE = 1024

BF16 = jnp.bfloat16
F32 = jnp.float32


def _dot(a, b):
    return jnp.dot(a, b, preferred_element_type=F32)


def _dot_nt(a, b):
    return lax.dot_general(a, b, (((1,), (1,)), ((), ())), preferred_element_type=F32)


def _dot_tn(a, b):
    return lax.dot_general(a, b, (((0,), (0,)), ((), ())), preferred_element_type=F32)


def _sigmoid(x):
    return 1.0 / (1.0 + jnp.exp(-x))


def _layer_norm_rows(y, g, b):
    mu = jnp.mean(y, axis=-1, keepdims=True)
    d = y - mu
    var = jnp.mean(d * d, axis=-1, keepdims=True)
    return d * lax.rsqrt(var + LN_EPS) * g + b


def _params(sem):
    return pltpu.CompilerParams(dimension_semantics=sem, vmem_limit_bytes=VMEM_LIMIT)


def _ffn_ln_kernel(x_ref, wg_ref, wu_ref, wd_ref, g_ref, b_ref, *rest, alpha, nf, bf, last, with_bf16):
    if with_bf16:
        o_ref, ob_ref, xb_sc, acc_sc = rest
    else:
        o_ref, xb_sc, acc_sc = rest
        ob_ref = None
    f = pl.program_id(1)

    @pl.when(f == 0)
    def _():
        xb_sc[...] = x_ref[...].astype(BF16)
        acc_sc[...] = jnp.zeros_like(acc_sc)

    def step(cols):
        xb = xb_sc[...]
        a = _dot(xb, wg_ref[:, 0:cols])
        u = _dot(xb, wu_ref[:, 0:cols])
        h = (a * _sigmoid(a)) * u
        acc_sc[...] += _dot(h.astype(BF16), wd_ref[0:cols, :])

    if nf > 1:
        @pl.when(f < nf - 1)
        def _():
            step(bf)

    @pl.when(f == nf - 1)
    def _():
        step(last)
        y = alpha * x_ref[...] + 0.5 * acc_sc[...]
        out = _layer_norm_rows(y, g_ref[...], b_ref[...])
        o_ref[...] = out
        if with_bf16:
            ob_ref[...] = out.astype(BF16)


def _ffn_ln(x, wg, wu, wd, g, b, alpha, with_bf16):
    n, d = x.shape
    ff = wg.shape[1]
    bm = min(ROW_TILE, n)
    bf = min(FF_TILE, ff)
    nf = pl.cdiv(ff, bf)
    last = ff - (nf - 1) * bf
    out_shape = [jax.ShapeDtypeStruct((n, d), F32)]
    out_specs = [pl.BlockSpec((bm, d), lambda i, f: (i, 0))]
    if with_bf16:
        out_shape.append(jax.ShapeDtypeStruct((n, d), BF16))
        out_specs.append(pl.BlockSpec((bm, d), lambda i, f: (i, 0)))
    res = pl.pallas_call(
        functools.partial(_ffn_ln_kernel, alpha=alpha, nf=nf, bf=bf, last=last, with_bf16=with_bf16),
        grid=(n // bm, nf),
        in_specs=[
            pl.BlockSpec((bm, d), lambda i, f: (i, 0)),
            pl.BlockSpec((d, bf), lambda i, f: (0, f)),
            pl.BlockSpec((d, bf), lambda i, f: (0, f)),
            pl.BlockSpec((bf, d), lambda i, f: (f, 0)),
            pl.BlockSpec((1, d), lambda i, f: (0, 0)),
            pl.BlockSpec((1, d), lambda i, f: (0, 0)),
        ],
        out_specs=out_specs,
        out_shape=out_shape,
        scratch_shapes=[pltpu.VMEM((bm, d), BF16), pltpu.VMEM((bm, d), F32)],
        compiler_params=_params(("parallel", "arbitrary")),
        name="ffn_ln",
    )(x, wg, wu, wd, g, b)
    return res if with_bf16 else (res[0], None)


def _mm_kernel(x_ref, w_ref, o_ref):
    o_ref[...] = _dot(x_ref[...], w_ref[...]).astype(o_ref.dtype)


def _matmul(x, w, out_dtype, name):
    n, k = x.shape
    nout = w.shape[1]
    bm = min(PROJ_ROW_TILE, n)
    bn = next(c for c in (PROJ_COL_TILE, COL_TILE, 2 * LANE, LANE, nout) if nout % c == 0)
    assert n % bm == 0
    return pl.pallas_call(
        _mm_kernel,
        grid=(n // bm, nout // bn),
        in_specs=[
            pl.BlockSpec((bm, k), lambda i, j: (i, 0)),
            pl.BlockSpec((k, bn), lambda i, j: (0, j)),
        ],
        out_specs=pl.BlockSpec((bm, bn), lambda i, j: (i, j)),
        out_shape=jax.ShapeDtypeStruct((n, nout), out_dtype),
        compiler_params=_params(("parallel", "arbitrary")),
        name=name,
    )(x, w)


def _nsa_proj_kernel(x_ref, wq_ref, wkv_ref, wg_ref, exp_ref, qt_ref, kc_ref, vc_ref, ket_ref, vst_ref,
                     kw_ref, vwt_ref, gt_ref, *, chunk):
    x = x_ref[...]
    hd, g = HEAD_DIM, NSA_GROUPS
    pair = 2 * hd
    tg = _sigmoid(_dot(x, wg_ref[...])).T
    for gg in range(g):
        gt_ref[0, gg] = tg[gg * GATE_ROWS:(gg + 1) * GATE_ROWS]
    yq = _dot(x, wq_ref[...]) * QK_SCALE
    for p in range(NSA_HEADS // 2):
        t = yq[:, p * pair:(p + 1) * pair].T.astype(qt_ref.dtype)
        qt_ref[0, 2 * p] = t[0:hd]
        qt_ref[0, 2 * p + 1] = t[hd:pair]
    ykv = _dot(x, wkv_ref[...])
    col = lambda kind, gg: (kind * g + gg) * hd
    ones = jnp.where(lax.broadcasted_iota(jnp.int32, (ONES_ROWS, x.shape[0]), 0) == 0, 1.0, 0.0)
    for gg in range(g):
        rows = lambda kind: ykv[:, col(kind, gg):col(kind, gg) + hd]
        kc_ref[0, gg] = rows(0).astype(kc_ref.dtype)
        vc_ref[0, gg] = rows(1).astype(vc_ref.dtype)
        ket_ref[0, gg, :, 0:chunk] = exp_ref[...]
        ket_ref[0, gg, :, chunk:chunk + hd] = rows(2).astype(ket_ref.dtype)
        kw_ref[0, gg] = rows(4).astype(kw_ref.dtype)
    for kind, out_ref in ((3, vst_ref), (5, vwt_ref)):
        for gg in range(0, g, 2):
            t = ykv[:, col(kind, gg):col(kind, gg) + pair].T.astype(out_ref.dtype)
            for k in range(2):
                out_ref[0, gg + k, 0:hd] = t[k * hd:(k + 1) * hd]
                out_ref[0, gg + k, hd:hd + ONES_ROWS] = ones.astype(out_ref.dtype)


def _nsa_proj(hb, w_q, w_kv, w_g, expand, bsz, s):
    n, d = hb.shape
    g, hd = NSA_GROUPS, HEAD_DIM
    chunk = expand.shape[1]
    bm = min(ROW_TILE, s)
    nj = s // bm
    rows_spec = lambda w: pl.BlockSpec((1, g, bm, w), lambda b, j: (b, 0, j, 0))
    cols_spec = lambda h, r: pl.BlockSpec((1, h, r, bm), lambda b, j: (b, 0, 0, j))
    full = lambda shape: pl.BlockSpec(shape, lambda b, j: (0, 0), pipeline_mode=pl.Buffered(1))
    rows_shape = lambda w: jax.ShapeDtypeStruct((bsz, g, s, w), BF16)
    cols_shape = lambda h, r: jax.ShapeDtypeStruct((bsz, h, r, s), BF16)
    return pl.pallas_call(
        functools.partial(_nsa_proj_kernel, chunk=chunk),
        grid=(bsz, nj),
        in_specs=[
            pl.BlockSpec((bm, d), lambda b, j: (b * nj + j, 0)),
            full(w_q.shape), full(w_kv.shape), full(w_g.shape),
            pl.BlockSpec((bm, chunk), lambda b, j: (j, 0)),
        ],
        out_specs=[cols_spec(NSA_HEADS, hd), rows_spec(hd), rows_spec(hd), rows_spec(chunk + hd),
                   cols_spec(g, hd + ONES_ROWS), rows_spec(hd), cols_spec(g, hd + ONES_ROWS),
                   cols_spec(g, GATE_ROWS)],
        out_shape=[cols_shape(NSA_HEADS, hd), rows_shape(hd), rows_shape(hd), rows_shape(chunk + hd),
                   cols_shape(g, hd + ONES_ROWS), rows_shape(hd), cols_shape(g, hd + ONES_ROWS),
                   jax.ShapeDtypeStruct((bsz, g, GATE_ROWS, s), F32)],
        compiler_params=_params(("parallel", "arbitrary")),
        name="nsa_proj",
    )(hb, w_q, w_kv, w_g, expand)


def _compress_kernel(x_ref, pe_ref, w1_ref, b1_ref, w2_ref, o_ref, *, nh, half):
    x = x_ref[0, 0].astype(F32)
    xa = (x + pe_ref[0:1, :]).astype(BF16)
    xb = (x + pe_ref[1:2, :]).astype(BF16)
    ha = _dot(xa, w1_ref[0:half, :])
    hb = _dot(xb, w1_ref[half:2 * half, :])
    hid = ha + pltpu.roll(hb, nh - 1, 0) + b1_ref[...]
    c = math.sqrt(2.0 / math.pi)
    act = 0.5 * hid * (1.0 + jnp.tanh(c * (hid + 0.044715 * (hid * hid * hid))))
    o_ref[0, 0] = _dot(act.astype(BF16), w2_ref[...]).astype(o_ref.dtype)


def _compress(x, pe2, w1, b1, w2):
    bsz, g, nh, half = x.shape
    hid = w1.shape[1]
    hd = w2.shape[1]
    return pl.pallas_call(
        functools.partial(_compress_kernel, nh=nh, half=half),
        grid=(bsz, g),
        in_specs=[
            pl.BlockSpec((1, 1, nh, half), lambda b, gg: (b, gg, 0, 0)),
            pl.BlockSpec((2, half), lambda b, gg: (0, 0)),
            pl.BlockSpec((2 * half, hid), lambda b, gg: (0, 0)),
            pl.BlockSpec((1, hid), lambda b, gg: (0, 0)),
            pl.BlockSpec((hid, hd), lambda b, gg: (0, 0)),
        ],
        out_specs=pl.BlockSpec((1, 1, nh, hd), lambda b, gg: (b, gg, 0, 0)),
        out_shape=jax.ShapeDtypeStruct((bsz, g, nh, hd), BF16),
        compiler_params=_params(("parallel", "parallel")),
        name="compress",
    )(x, pe2, w1, b1, w2)


def _stack_heads(qt_ref):
    return jnp.concatenate([qt_ref[0, r] for r in range(NSA_REP)], axis=1)


def _per_head(x):
    return jnp.tile(x, (1, NSA_REP))


def _cmp_topk_kernel(qt_ref, kc_ref, vct_ref, ovt_ref, oct_ref, selt_ref, *, tq, nh, nb, n_sel, nq):
    i = pl.program_id(2)
    for part in range(CAUSAL_PARTS):
        lo, hi = part * nq // CAUSAL_PARTS, (part + 1) * nq // CAUSAL_PARTS

        @pl.when(jnp.logical_and(i >= lo, i < hi))
        def _(part=part):
            _cmp_topk_part(qt_ref, kc_ref, vct_ref, ovt_ref, oct_ref, selt_ref, i * tq, tq=tq,
                           nh=(part + 1) * nh // CAUSAL_PARTS, nb=(part + 1) * nb // CAUSAL_PARTS,
                           nb_all=nb, n_sel=n_sel)


def _cmp_topk_part(qt_ref, kc_ref, vct_ref, ovt_ref, oct_ref, selt_ref, t0, *, tq, nh, nb, nb_all, n_sel):
    cend = lax.broadcasted_iota(jnp.int32, (nh, tq), 0) * CMP_STRIDE + (CMP_BLOCK - 1)
    tpos = t0 + lax.broadcasted_iota(jnp.int32, (nh, tq), 1)
    bias = jnp.where(cend <= tpos, 0.0, NEG)
    seen = jnp.where(t0 + lax.broadcasted_iota(jnp.int32, (1, tq), 1) >= CMP_BLOCK - 1, 1.0, 0.0)
    qt = _stack_heads(qt_ref)
    s = _dot(kc_ref[0, 0, 0:nh, :], qt) + _per_head(bias)
    e = jnp.exp2(s - jnp.max(s, axis=0, keepdims=True))
    ea = _dot(vct_ref[0, 0, :, 0:nh], e.astype(BF16))
    inv = _per_head(seen) / jnp.maximum(ea[HEAD_DIM:HEAD_DIM + 1], TINY)
    oct_ref[0, 0] = ea[0:HEAD_DIM] * inv
    p = e * inv
    psum = p[:, 0:tq]
    for r in range(1, NSA_REP):
        psum = psum + p[:, r * tq:(r + 1) * tq]
    imp = _dot(ovt_ref[0:nb, 0:nh], psum.astype(BF16))
    blk = lax.broadcasted_iota(jnp.int32, (nb, tq), 0).astype(F32)
    cur = ((t0 + lax.broadcasted_iota(jnp.int32, (nb, tq), 1)) // SLC_BLOCK).astype(F32)
    forced = (blk == 0.0) | (blk == cur) | (blk == cur - 1.0)
    x = jnp.where(forced, -jnp.inf, jnp.where(blk <= cur, imp, NEG))
    for _ in range(n_sel - N_FORCED):
        mx = jnp.max(x, axis=0, keepdims=True)
        idx = jnp.min(jnp.where(x == mx, blk, float(nb)), axis=0, keepdims=True)
        x = jnp.where(blk == idx, -jnp.inf, x)
    selb = jnp.where(x == -jnp.inf, 0.0, NEG)
    selt_ref[0, 0, 0:nb, :] = selb.astype(selt_ref.dtype)
    if nb < nb_all:
        selt_ref[0, 0, nb:nb_all, :] = jnp.full((nb_all - nb, tq), NEG, selt_ref.dtype)


def _cmp_topk(qt, kc, vct, ovt):
    bsz, _, hd, s = qt.shape
    g, nh = kc.shape[1], kc.shape[2]
    nb = ovt.shape[0]
    assert nb >= SLC_TOPN and nh % CAUSAL_PARTS == 0 and nb % CAUSAL_PARTS == 0
    tq = min(Q_TILE, s)
    cols = NSA_REP * tq
    return pl.pallas_call(
        functools.partial(_cmp_topk_kernel, tq=tq, nh=nh, nb=nb, n_sel=SLC_TOPN, nq=s // tq),
        grid=(bsz, g, s // tq),
        in_specs=[
            pl.BlockSpec((1, NSA_REP, hd, tq), lambda b, gg, i: (b, gg, 0, i)),
            pl.BlockSpec((1, 1, nh, hd), lambda b, gg, i: (b, gg, 0, 0)),
            pl.BlockSpec((1, 1, hd + ONES_ROWS, nh), lambda b, gg, i: (b, gg, 0, 0)),
            pl.BlockSpec((nb, nh), lambda b, gg, i: (0, 0)),
        ],
        out_specs=[
            pl.BlockSpec((1, 1, hd, cols), lambda b, gg, i: (b, gg, 0, i)),
            pl.BlockSpec((1, 1, nb, tq), lambda b, gg, i: (b, gg, 0, i)),
        ],
        out_shape=[
            jax.ShapeDtypeStruct((bsz, g, hd, NSA_REP * s), F32),
            jax.ShapeDtypeStruct((bsz, g, nb, s), BF16),
        ],
        compiler_params=_params(("parallel", "parallel", "arbitrary")),
        name="cmp_topk",
    )(qt, kc, vct, ovt)


def _sel_win_kernel(qt_ref, ket_ref, vst_ref, kw_ref, vwt_ref, selt_ref, oct_ref, gatet_ref,
                    o_ref, rhs_sc, m_sc, acc_sc, sa_sc, sb_sc, ma_sc, mb_sc, sw_sc, mw_sc, ow_sc,
                    *, tq, tk, chunk, nvar, nchunk, wlen):
    t0 = pl.program_id(2) * tq
    qt = _stack_heads(qt_ref)
    for c in range(nchunk):
        selt = selt_ref[0, 0, c * chunk:(c + 1) * chunk, :]
        rhs_sc[c] = jnp.concatenate([_per_head(selt), qt], axis=0)

    m_sc[...] = jnp.full_like(m_sc, NEG)
    acc_sc[...] = jnp.zeros_like(acc_sc)

    cols = NSA_REP * tq
    part_cols = [slice(c * cols // COL_PARTS, (c + 1) * cols // COL_PARTS) for c in range(COL_PARTS)]

    def put_scores(slot, kt, causal, cs):
        s_ref, mx_ref = slot
        k0 = pl.multiple_of(kt * tk, tk)
        s = _dot(ket_ref[0, 0, pl.ds(k0, tk), :], rhs_sc[kt // nvar, :, cs])
        if causal:
            kpos = k0 + lax.broadcasted_iota(jnp.int32, (tk, tq), 0)
            tcol = t0 + lax.broadcasted_iota(jnp.int32, (tk, tq), 1)
            s = s + jnp.tile(jnp.where(kpos <= tcol, 0.0, NEG), (1, NSA_REP // COL_PARTS))
        s_ref[:, cs] = s
        mx_ref[:, cs] = jnp.max(s, axis=0, keepdims=True)

    def accumulate(slot, kt, cs):
        s_ref, mx_ref = slot
        k0 = pl.multiple_of(kt * tk, tk)
        m_old = m_sc[:, cs]
        m_new = jnp.maximum(m_old, mx_ref[:, cs])
        alpha = jnp.exp2(m_old - m_new)
        p = jnp.exp2(s_ref[:, cs] - m_new)
        acc_sc[:, cs] = alpha * acc_sc[:, cs] + _dot(vst_ref[0, 0, :, pl.ds(k0, tk)], p.astype(BF16))
        m_sc[:, cs] = m_new

    def put_all(slot, kt, causal):
        for cs in part_cols:
            put_scores(slot, kt, causal, cs)

    def accumulate_all(slot, kt):
        for cs in part_cols:
            accumulate(slot, kt, cs)

    def stage(dst, kt_new, causal, src, kt_old):
        for cs in part_cols:
            put_scores(dst, kt_new, causal, cs)
            accumulate(src, kt_old, cs)

    slot_a, slot_b = (sa_sc, ma_sc), (sb_sc, mb_sc)
    nfull = (t0 + tq - 1) // tk

    wk = wlen + tq
    w0 = pl.multiple_of(jnp.maximum(t0 - wlen, 0), tq)
    diff = (t0 + lax.broadcasted_iota(jnp.int32, (wk, tq), 1)) - (
        w0 + lax.broadcasted_iota(jnp.int32, (wk, tq), 0))
    wbias = jnp.where(jnp.logical_and(diff >= 0, diff < wlen), 0.0, NEG)
    sw = _dot(kw_ref[0, 0, pl.ds(w0, wk), :], qt) + _per_head(wbias)
    sw_sc[...] = sw
    mw_sc[...] = jnp.max(sw, axis=0, keepdims=True)

    def window_out():
        ew = jnp.exp2(sw_sc[...] - mw_sc[...])
        ow_sc[...] = _dot(vwt_ref[0, 0, :, pl.ds(w0, wk)], ew.astype(BF16))

    @pl.when(nfull == 0)
    def _():
        put_all(slot_a, 0, True)
        window_out()
        accumulate_all(slot_a, 0)

    @pl.when(nfull > 0)
    def _():
        put_all(slot_a, 0, False)
        window_out()

        slots = (slot_a, slot_b)

        def stages(first, count, last_causal):
            for u in range(count):
                stage(slots[(1 + u) % 2], first + u, last_causal and u == count - 1,
                      slots[u % 2], first + u - 1)

        def body(j, carry):
            stages(LOOP_STAGES * j + 1, LOOP_STAGES, False)
            return carry

        nloop = (nfull - 1) // LOOP_STAGES
        lax.fori_loop(0, nloop, body, 0)
        done = LOOP_STAGES * nloop
        for rest in range(LOOP_STAGES):

            @pl.when(nfull - 1 - done == rest)
            def _(rest=rest):
                stages(done + 1, rest + 1, True)
                accumulate_all(slots[(1 + rest) % 2], nfull)

    o_s = acc_sc[0:HEAD_DIM] * (1.0 / jnp.maximum(acc_sc[HEAD_DIM:HEAD_DIM + 1], TINY))

    o_w = ow_sc[0:HEAD_DIM] * (1.0 / jnp.maximum(ow_sc[HEAD_DIM:HEAD_DIM + 1], TINY))

    gate = gatet_ref[0, 0]
    o_c = oct_ref[0, 0]
    outs = []
    for r in range(NSA_REP):
        cs = slice(r * tq, (r + 1) * tq)
        outs.append(gate[3 * r:3 * r + 1] * o_c[:, cs] + gate[3 * r + 1:3 * r + 2] * o_s[:, cs]
                    + gate[3 * r + 2:3 * r + 3] * o_w[:, cs])
    pairs = [jnp.concatenate(outs[r:r + 2], axis=0).T for r in range(0, NSA_REP, 2)]
    o_ref[0] = jnp.concatenate(pairs, axis=1).astype(o_ref.dtype)


def _sel_win(qt, ket, vst, kw, vwt, selt, oct, gatet, tk):
    bsz, _, hd, s = qt.shape
    g = ket.shape[1]
    nb = selt.shape[2]
    chunk = ket.shape[3] - hd
    nchunk = nb // chunk
    nvar = chunk * SLC_BLOCK // tk
    tq = min(Q_TILE, s)
    cols = NSA_REP * tq
    resident = lambda shape: pl.BlockSpec(shape, lambda b, gg, i: (b, gg, 0, 0),
                                          pipeline_mode=pl.Buffered(1))
    return pl.pallas_call(
        functools.partial(_sel_win_kernel, tq=tq, tk=tk, chunk=chunk, nvar=nvar, nchunk=nchunk,
                          wlen=WINDOW),
        grid=(bsz, g, s // tq),
        in_specs=[
            pl.BlockSpec((1, NSA_REP, hd, tq), lambda b, gg, i: (b, gg, 0, i)),
            resident((1, 1, s, chunk + hd)),
            resident((1, 1, hd + ONES_ROWS, s)),
            resident((1, 1, s, hd)),
            resident((1, 1, hd + ONES_ROWS, s)),
            pl.BlockSpec((1, 1, nb, tq), lambda b, gg, i: (b, gg, 0, i)),
            pl.BlockSpec((1, 1, hd, cols), lambda b, gg, i: (b, gg, 0, i)),
            pl.BlockSpec((1, 1, GATE_ROWS, tq), lambda b, gg, i: (b, gg, 0, i)),
        ],
        out_specs=pl.BlockSpec((1, tq, NSA_REP * hd), lambda b, gg, i: (b, i, gg)),
        out_shape=jax.ShapeDtypeStruct((bsz, s, g * NSA_REP * hd), BF16),
        scratch_shapes=[
            pltpu.VMEM((nchunk, chunk + hd, cols), BF16),
            pltpu.VMEM((1, cols), F32),
            pltpu.VMEM((hd + ONES_ROWS, cols), F32),
            pltpu.VMEM((tk, cols), F32),
            pltpu.VMEM((tk, cols), F32),
            pltpu.VMEM((1, cols), F32),
            pltpu.VMEM((1, cols), F32),
            pltpu.VMEM((WINDOW + tq, cols), F32),
            pltpu.VMEM((1, cols), F32),
            pltpu.VMEM((hd + ONES_ROWS, cols), F32),
        ],
        compiler_params=_params(("parallel", "parallel", "arbitrary")),
        name="sel_win",
    )(qt, ket, vst, kw, vwt, selt, oct, gatet)


def _retention_kernel(q_ref, k_ref, v_ref, gb_ref, cos_ref, sin_ref, dmat_ref, qdec_ref, kdec_ref,
                      cdec_ref, gng_ref, gnb_ref, o_ref, state_sc, *, nchunk, c):
    @pl.when(pl.program_id(2) == 0)
    def _():
        state_sc[...] = jnp.zeros_like(state_sc)

    dmat = dmat_ref[0]
    qdec = qdec_ref[0]
    kdec = kdec_ref[0]
    cdec = cdec_ref[0]
    kscale = RET_DIM ** -0.5
    half = RET_DIM // 2
    for n in range(nchunk):
        rows = pl.ds(n * c, c)
        cos = cos_ref[rows, :]
        sin = sin_ref[rows, :]
        qf = q_ref[0, rows, :].astype(F32)
        kf = k_ref[0, rows, :].astype(F32)
        qr = qf * cos + pltpu.roll(qf, half, 1) * sin
        kr = (kf * cos + pltpu.roll(kf, half, 1) * sin) * kscale
        v = v_ref[0, rows, :]
        inner = _dot_nt(qr.astype(BF16), kr.astype(BF16)) * dmat
        state = state_sc[...]
        y = _dot(inner.astype(BF16), v) + _dot((qr * qdec).astype(BF16), state.astype(BF16))
        state_sc[...] = state * cdec + _dot_tn((kr * kdec).astype(BF16), v)
        mu = jnp.mean(y, axis=-1, keepdims=True)
        d = y - mu
        var = jnp.mean(d * d, axis=-1, keepdims=True)
        yn = d * lax.rsqrt(var + LN_EPS) * gng_ref[...] + gnb_ref[...]
        gb = gb_ref[0, rows, :].astype(F32)
        o_ref[0, rows, :] = ((gb * _sigmoid(gb)) * yn).astype(o_ref.dtype)


def _retention(ret, cos2, sin2, dmat, qdec, kdec, cdec, gng, gnb):
    bsz, s, _ = ret.shape
    h = RET_HEADS
    c = RET_CHUNK
    tc = min(RET_TILE, s)
    spec = lambda off: pl.BlockSpec((1, tc, RET_DIM), lambda b, hh, j, off=off: (b, j, off + hh))
    hspec = lambda shp: pl.BlockSpec((1,) + shp, lambda b, hh, j: (hh, 0, 0))
    return pl.pallas_call(
        functools.partial(_retention_kernel, nchunk=tc // c, c=c),
        grid=(bsz, h, s // tc),
        in_specs=[
            spec(0), spec(h), spec(2 * h), spec(3 * h),
            pl.BlockSpec((tc, RET_DIM), lambda b, hh, j: (j, 0)),
            pl.BlockSpec((tc, RET_DIM), lambda b, hh, j: (j, 0)),
            hspec((c, c)), hspec((c, RET_DIM)), hspec((c, RET_DIM)), hspec((1, RET_DIM)),
            pl.BlockSpec((1, RET_DIM), lambda b, hh, j: (0, hh)),
            pl.BlockSpec((1, RET_DIM), lambda b, hh, j: (0, hh)),
        ],
        out_specs=pl.BlockSpec((1, tc, RET_DIM), lambda b, hh, j: (b, j, hh)),
        out_shape=jax.ShapeDtypeStruct((bsz, s, h * RET_DIM), BF16),
        scratch_shapes=[pltpu.VMEM((RET_DIM, RET_DIM), F32)],
        compiler_params=_params(("parallel", "parallel", "arbitrary")),
        name="retention",
    )(ret, ret, ret, ret, cos2, sin2, dmat, qdec, kdec, cdec, gng, gnb)


def _merge_kernel(h_ref, oa_ref, ob_ref, wga_ref, wgb_ref, wpa_ref, wpb_ref, o_ref):
    hb = h_ref[...]
    ga = _sigmoid(_dot(hb, wga_ref[...]))
    gb = _sigmoid(_dot(hb, wgb_ref[...]))
    merged = ga * _dot(oa_ref[...], wpa_ref[...]) + gb * _dot(ob_ref[...], wpb_ref[...])
    o_ref[...] = merged.astype(o_ref.dtype)


def _merge(hb, oa, ob, wga, wgb, wpa, wpb):
    n, d = hb.shape
    bm = min(PROJ_ROW_TILE, n)
    bn = min(COL_TILE, d)
    ka, kb = oa.shape[1], ob.shape[1]
    return pl.pallas_call(
        _merge_kernel,
        grid=(n // bm, d // bn),
        in_specs=[
            pl.BlockSpec((bm, d), lambda i, j: (i, 0)),
            pl.BlockSpec((bm, ka), lambda i, j: (i, 0)),
            pl.BlockSpec((bm, kb), lambda i, j: (i, 0)),
            pl.BlockSpec((d, bn), lambda i, j: (0, j)),
            pl.BlockSpec((d, bn), lambda i, j: (0, j)),
            pl.BlockSpec((ka, bn), lambda i, j: (0, j)),
            pl.BlockSpec((kb, bn), lambda i, j: (0, j)),
        ],
        out_specs=pl.BlockSpec((bm, bn), lambda i, j: (i, j)),
        out_shape=jax.ShapeDtypeStruct((n, d), BF16),
        compiler_params=_params(("parallel", "arbitrary")),
        name="merge",
    )(hb, oa, ob, wga, wgb, wpa, wpb)


def _proj_ln_kernel(x_ref, m_ref, w_ref, g_ref, b_ref, o_ref, *, alpha):
    y = alpha * x_ref[...] + _dot(m_ref[...], w_ref[...])
    o_ref[...] = _layer_norm_rows(y, g_ref[...], b_ref[...])


def _proj_ln(x, m, w, g, b, alpha):
    n, d = x.shape
    bm = min(ROW_TILE, n)
    return pl.pallas_call(
        functools.partial(_proj_ln_kernel, alpha=alpha),
        grid=(n // bm,),
        in_specs=[
            pl.BlockSpec((bm, d), lambda i: (i, 0)),
            pl.BlockSpec((bm, d), lambda i: (i, 0)),
            pl.BlockSpec((d, d), lambda i: (0, 0)),
            pl.BlockSpec((1, d), lambda i: (0, 0)),
            pl.BlockSpec((1, d), lambda i: (0, 0)),
        ],
        out_specs=pl.BlockSpec((bm, d), lambda i: (i, 0)),
        out_shape=jax.ShapeDtypeStruct((n, d), F32),
        compiler_params=_params(("parallel",)),
        name="proj_ln",
    )(x, m, w, g, b)


def _overlap_matrix(nh, nb):
    c0 = np.arange(nh) * CMP_STRIDE
    c1 = c0 + CMP_BLOCK
    s0 = np.arange(nb) * SLC_BLOCK
    s1 = s0 + SLC_BLOCK
    ov = (c0[:, None] < s1[None, :]) & (c1[:, None] > s0[None, :])
    ov[nh - 1, :] = False
    return jnp.asarray(ov.T, BF16)


def _expand_matrix(nb, s):
    chunk = min(SEL_CHUNK, nb)
    blk_in_chunk = (np.arange(s) // SLC_BLOCK) % chunk
    return jnp.asarray(blk_in_chunk[:, None] == np.arange(chunk)[None, :], BF16)


def _retention_tables(s):
    h, c, d = RET_HEADS, RET_CHUNK, RET_DIM
    inv = ROPE_BASE ** (-jnp.arange(0, d, 2, dtype=F32) / d)
    ang = jnp.arange(s)[:, None].astype(F32) * inv[None, :]
    cos, sin = jnp.cos(ang), jnp.sin(ang)
    cos2 = jnp.concatenate([cos, cos], -1)
    sin2 = jnp.concatenate([-sin, sin], -1)
    log_g = jnp.log1p(-jnp.exp2(-5.0 - jnp.arange(h, dtype=F32)))
    i = jnp.arange(c, dtype=F32)
    diff = i[:, None] - i[None, :]
    dmat = jnp.where(diff >= 0, jnp.exp(jnp.maximum(diff, 0.0)[None] * log_g[:, None, None]), 0.0)
    kdec = jnp.exp((c - 1 - i)[None, :] * log_g[:, None])
    qdec = jnp.exp((i + 1)[None, :] * log_g[:, None])
    cdec = jnp.exp(c * log_g)
    bc = lambda t: jnp.broadcast_to(t[:, :, None], (h, c, d))
    return cos2, sin2, dmat, bc(qdec), bc(kdec), jnp.broadcast_to(cdec[:, None, None], (h, 1, d))


def _pad_cols(w, mult):
    pad = (-w.shape[1]) % mult
    return jnp.pad(w, ((0, 0), (0, pad))) if pad else w


def _mixer(hf, hb, bsz, s, w_in, cmp_k, cmp_v, ret_gn_g, ret_gn_b, w_merge_gate, w_proj_a, w_proj_b,
           w_o, ln_g, ln_b, alpha, tables):
    n, d = hf.shape
    g, hd = NSA_GROUPS, HEAD_DIM
    ovt, expand, ret_tabs = tables
    o_nsa = NSA_Q + 6 * NSA_KV
    w_q = w_in[:, :NSA_Q].astype(BF16)
    w_kv = w_in[:, NSA_Q:o_nsa].astype(BF16)
    w_gate = w_in[:, o_nsa:o_nsa + NSA_GATE].reshape(d, g, 3 * NSA_REP)
    w_gate = jnp.pad(w_gate, ((0, 0), (0, 0), (0, GATE_ROWS - 3 * NSA_REP))).reshape(d, g * GATE_ROWS)
    w_gate = _pad_cols(w_gate, LANE).astype(BF16)
    w_ret = w_in[:, o_nsa + NSA_GATE:].astype(BF16)

    qt, kc_in, vc_in, ket, vst, kw, vwt, gatet = _nsa_proj(hb, w_q, w_kv, w_gate, expand, bsz, s)
    ret = _matmul(hb, w_ret, BF16, "proj_ret")

    nh = s // CMP_STRIDE
    half = CMP_STRIDE * hd

    def compress(t, prm):
        pe, w1, b1, w2 = prm
        return _compress(t.reshape(bsz, g, nh, half), pe.reshape(2, half), w1.astype(BF16),
                         b1.reshape(1, -1), w2.astype(BF16))

    kc = compress(kc_in, cmp_k)
    ones_rows = jnp.zeros((bsz, g, ONES_ROWS, nh), BF16).at[:, :, 0].set(1.0)
    vct = jnp.concatenate([jnp.transpose(compress(vc_in, cmp_v), (0, 1, 3, 2)), ones_rows], axis=2)
    oct, selt = _cmp_topk(qt, kc, vct, ovt)

    o_a = _sel_win(qt, ket, vst, kw, vwt, selt, oct, gatet, min(K_TILE, s))

    o_b = _retention(ret.reshape(bsz, s, 4 * RET_W), *ret_tabs,
                     ret_gn_g.reshape(1, -1), ret_gn_b.reshape(1, -1))

    merged = _merge(hb, o_a.reshape(n, NSA_Q), o_b.reshape(n, RET_W),
                    w_merge_gate[:, :d].astype(BF16), w_merge_gate[:, d:].astype(BF16),
                    w_proj_a.astype(BF16), w_proj_b.astype(BF16))
    return _proj_ln(hf, merged, w_o.astype(BF16), ln_g.reshape(1, -1), ln_b.reshape(1, -1), alpha)


def kernel(x, ffn1_w_gate, ffn1_w_up, ffn1_w_down, ln1_g, ln1_b, w_in, cmp_k_pe, cmp_k_w1, cmp_k_b1,
           cmp_k_w2, cmp_v_pe, cmp_v_w1, cmp_v_b1, cmp_v_w2, ret_gn_g, ret_gn_b, w_merge_gate, w_proj_a,
           w_proj_b, w_o, ln2_g, ln2_b, ffn2_w_gate, ffn2_w_up, ffn2_w_down, ln3_g, ln3_b):
    bsz, s, d = x.shape
    depth = ffn1_w_gate.shape[0]
    alpha = (2 * depth) ** 0.25
    nb = s // SLC_BLOCK
    tables = (_overlap_matrix(s // CMP_STRIDE, nb), _expand_matrix(nb, s), _retention_tables(s))
    ffn_weights = lambda wg, wu, wd: (wg.astype(BF16), wu.astype(BF16), wd.astype(BF16))

    row = lambda t: t.reshape(1, -1)
    xf = x.reshape(bsz * s, d)
    for l in range(depth):
        hf, hb = _ffn_ln(xf, *ffn_weights(ffn1_w_gate[l], ffn1_w_up[l], ffn1_w_down[l]),
                         row(ln1_g[l]), row(ln1_b[l]), alpha, True)
        xf = _mixer(hf, hb, bsz, s, w_in[l],
                    (cmp_k_pe[l], cmp_k_w1[l], cmp_k_b1[l], cmp_k_w2[l]),
                    (cmp_v_pe[l], cmp_v_w1[l], cmp_v_b1[l], cmp_v_w2[l]),
                    ret_gn_g[l], ret_gn_b[l], w_merge_gate[l], w_proj_a[l], w_proj_b[l], w_o[l],
                    ln2_g[l], ln2_b[l], alpha, tables)
        xf, _ = _ffn_ln(xf, *ffn_weights(ffn2_w_gate[l], ffn2_w_up[l], ffn2_w_down[l]),
                        row(ln3_g[l]), row(ln3_b[l]), alpha, False)
    return xf.reshape(bsz, s, d)
```

```python
import functools
import math

import jax
import jax.numpy as jnp
import numpy as np
from jax import lax
from jax.experimental import pallas as pl
from jax.experimental.pallas import tpu as pltpu

NSA_HEADS = 16
NSA_GROUPS = 4
NSA_REP = NSA_HEADS // NSA_GROUPS
HEAD_DIM = 64
CMP_BLOCK = 32
CMP_STRIDE = 16
SLC_BLOCK = 64
SLC_TOPN = 16
N_FORCED = 3
CAUSAL_PARTS = 8
COL_PARTS = 2
SEL_CHUNK = 64
LOOP_STAGES = 4
WINDOW = 512
RET_HEADS = 8
RET_DIM = 128
RET_CHUNK = 128
ROPE_BASE = 10000.0
LN_EPS = 1e-5
NEG = -1e30
TINY = 1e-30
NSA_Q = NSA_HEADS * HEAD_DIM
NSA_KV = NSA_GROUPS * HEAD_DIM
NSA_GATE = 3 * NSA_HEADS
GATE_ROWS = 16
ONES_ROWS = 16
QK_SCALE = HEAD_DIM ** -0.5 * math.log2(math.e)
RET_W = RET_HEADS * RET_DIM

LANE = 128
VMEM_LIMIT = 52 * 1024 * 1024
ROW_TILE = 512
FF_TILE = 512
COL_TILE = 512
PROJ_ROW_TILE = 1024
PROJ_COL_TILE = 1024
Q_TILE = 256
K_TILE = 512
RET_TILE = 1024
RET_HEADS_PER_STEP = 2

BF16 = jnp.bfloat16
F32 = jnp.float32


def _dot(a, b):
    return jnp.dot(a, b, preferred_element_type=F32)


def _dot_nt(a, b):
    return lax.dot_general(a, b, (((1,), (1,)), ((), ())), preferred_element_type=F32)


def _dot_tn(a, b):
    return lax.dot_general(a, b, (((0,), (0,)), ((), ())), preferred_element_type=F32)


def _sigmoid(x):
    return 1.0 / (1.0 + jnp.exp(-x))


def _layer_norm_rows(y, g, b):
    mu = jnp.mean(y, axis=-1, keepdims=True)
    d = y - mu
    var = jnp.mean(d * d, axis=-1, keepdims=True)
    return d * lax.rsqrt(var + LN_EPS) * g + b


def _params(sem):
    return pltpu.CompilerParams(dimension_semantics=sem, vmem_limit_bytes=VMEM_LIMIT)


def _ffn_ln_kernel(x_ref, wg_ref, wu_ref, wd_ref, g_ref, b_ref, *rest, alpha, nf, bf, last, with_bf16):
    if with_bf16:
        o_ref, ob_ref, xb_sc, acc_sc = rest
    else:
        o_ref, xb_sc, acc_sc = rest
        ob_ref = None
    f = pl.program_id(1)

    @pl.when(f == 0)
    def _():
        xb_sc[...] = x_ref[...].astype(BF16)
        acc_sc[...] = jnp.zeros_like(acc_sc)

    def step(cols):
        xb = xb_sc[...]
        a = _dot(xb, wg_ref[:, 0:cols])
        u = _dot(xb, wu_ref[:, 0:cols])
        h = (a * _sigmoid(a)) * u
        acc_sc[...] += _dot(h.astype(BF16), wd_ref[0:cols, :])

    if nf > 1:
        @pl.when(f < nf - 1)
        def _():
            step(bf)

    @pl.when(f == nf - 1)
    def _():
        step(last)
        y = alpha * x_ref[...] + 0.5 * acc_sc[...]
        out = _layer_norm_rows(y, g_ref[...], b_ref[...])
        o_ref[...] = out
        if with_bf16:
            ob_ref[...] = out.astype(BF16)


def _ffn_ln(x, wg, wu, wd, g, b, alpha, with_bf16):
    n, d = x.shape
    ff = wg.shape[1]
    bm = min(ROW_TILE, n)
    bf = min(FF_TILE, ff)
    nf = pl.cdiv(ff, bf)
    last = ff - (nf - 1) * bf
    out_shape = [jax.ShapeDtypeStruct((n, d), F32)]
    out_specs = [pl.BlockSpec((bm, d), lambda i, f: (i, 0))]
    if with_bf16:
        out_shape.append(jax.ShapeDtypeStruct((n, d), BF16))
        out_specs.append(pl.BlockSpec((bm, d), lambda i, f: (i, 0)))
    res = pl.pallas_call(
        functools.partial(_ffn_ln_kernel, alpha=alpha, nf=nf, bf=bf, last=last, with_bf16=with_bf16),
        grid=(n // bm, nf),
        in_specs=[
            pl.BlockSpec((bm, d), lambda i, f: (i, 0)),
            pl.BlockSpec((d, bf), lambda i, f: (0, f)),
            pl.BlockSpec((d, bf), lambda i, f: (0, f)),
            pl.BlockSpec((bf, d), lambda i, f: (f, 0)),
            pl.BlockSpec((1, d), lambda i, f: (0, 0)),
            pl.BlockSpec((1, d), lambda i, f: (0, 0)),
        ],
        out_specs=out_specs,
        out_shape=out_shape,
        scratch_shapes=[pltpu.VMEM((bm, d), BF16), pltpu.VMEM((bm, d), F32)],
        compiler_params=_params(("parallel", "arbitrary")),
        name="ffn_ln",
    )(x, wg, wu, wd, g, b)
    return res if with_bf16 else (res[0], None)


def _mm_kernel(x_ref, w_ref, o_ref):
    o_ref[...] = _dot(x_ref[...], w_ref[...]).astype(o_ref.dtype)


def _matmul(x, w, out_dtype, name):
    n, k = x.shape
    nout = w.shape[1]
    bm = min(PROJ_ROW_TILE, n)
    bn = next(c for c in (PROJ_COL_TILE, COL_TILE, 2 * LANE, LANE, nout) if nout % c == 0)
    assert n % bm == 0
    return pl.pallas_call(
        _mm_kernel,
        grid=(n // bm, nout // bn),
        in_specs=[
            pl.BlockSpec((bm, k), lambda i, j: (i, 0)),
            pl.BlockSpec((k, bn), lambda i, j: (0, j)),
        ],
        out_specs=pl.BlockSpec((bm, bn), lambda i, j: (i, j)),
        out_shape=jax.ShapeDtypeStruct((n, nout), out_dtype),
        compiler_params=_params(("parallel", "arbitrary")),
        name=name,
    )(x, w)


def _nsa_proj_kernel(x_ref, wq_ref, wkv_ref, wg_ref, exp_ref, qt_ref, kc_ref, vc_ref, ket_ref, vst_ref,
                     kw_ref, vwt_ref, gt_ref, *, chunk):
    x = x_ref[...]
    hd, g = HEAD_DIM, NSA_GROUPS
    pair = 2 * hd
    tg = _sigmoid(_dot(x, wg_ref[...])).T
    for gg in range(g):
        gt_ref[0, gg] = tg[gg * GATE_ROWS:(gg + 1) * GATE_ROWS]
    yq = _dot(x, wq_ref[...]) * QK_SCALE
    for p in range(NSA_HEADS // 2):
        t = yq[:, p * pair:(p + 1) * pair].T.astype(qt_ref.dtype)
        qt_ref[0, 2 * p] = t[0:hd]
        qt_ref[0, 2 * p + 1] = t[hd:pair]
    ykv = _dot(x, wkv_ref[...])
    col = lambda kind, gg: (kind * g + gg) * hd
    ones = jnp.where(lax.broadcasted_iota(jnp.int32, (ONES_ROWS, x.shape[0]), 0) == 0, 1.0, 0.0)
    for gg in range(g):
        rows = lambda kind: ykv[:, col(kind, gg):col(kind, gg) + hd]
        kc_ref[0, gg] = rows(0).astype(kc_ref.dtype)
        vc_ref[0, gg] = rows(1).astype(vc_ref.dtype)
        ket_ref[0, gg, :, 0:chunk] = exp_ref[...]
        ket_ref[0, gg, :, chunk:chunk + hd] = rows(2).astype(ket_ref.dtype)
        kw_ref[0, gg] = rows(4).astype(kw_ref.dtype)
    for kind, out_ref in ((3, vst_ref), (5, vwt_ref)):
        for gg in range(0, g, 2):
            t = ykv[:, col(kind, gg):col(kind, gg) + pair].T.astype(out_ref.dtype)
            for k in range(2):
                out_ref[0, gg + k, 0:hd] = t[k * hd:(k + 1) * hd]
                out_ref[0, gg + k, hd:hd + ONES_ROWS] = ones.astype(out_ref.dtype)


def _nsa_proj(hb, w_q, w_kv, w_g, expand, bsz, s):
    n, d = hb.shape
    g, hd = NSA_GROUPS, HEAD_DIM
    chunk = expand.shape[1]
    bm = min(ROW_TILE, s)
    nj = s // bm
    rows_spec = lambda w: pl.BlockSpec((1, g, bm, w), lambda b, j: (b, 0, j, 0))
    cols_spec = lambda h, r: pl.BlockSpec((1, h, r, bm), lambda b, j: (b, 0, 0, j))
    full = lambda shape: pl.BlockSpec(shape, lambda b, j: (0, 0), pipeline_mode=pl.Buffered(1))
    rows_shape = lambda w: jax.ShapeDtypeStruct((bsz, g, s, w), BF16)
    cols_shape = lambda h, r: jax.ShapeDtypeStruct((bsz, h, r, s), BF16)
    return pl.pallas_call(
        functools.partial(_nsa_proj_kernel, chunk=chunk),
        grid=(bsz, nj),
        in_specs=[
            pl.BlockSpec((bm, d), lambda b, j: (b * nj + j, 0)),
            full(w_q.shape), full(w_kv.shape), full(w_g.shape),
            pl.BlockSpec((bm, chunk), lambda b, j: (j, 0)),
        ],
        out_specs=[cols_spec(NSA_HEADS, hd), rows_spec(hd), rows_spec(hd), rows_spec(chunk + hd),
                   cols_spec(g, hd + ONES_ROWS), rows_spec(hd), cols_spec(g, hd + ONES_ROWS),
                   cols_spec(g, GATE_ROWS)],
        out_shape=[cols_shape(NSA_HEADS, hd), rows_shape(hd), rows_shape(hd), rows_shape(chunk + hd),
                   cols_shape(g, hd + ONES_ROWS), rows_shape(hd), cols_shape(g, hd + ONES_ROWS),
                   jax.ShapeDtypeStruct((bsz, g, GATE_ROWS, s), F32)],
        compiler_params=_params(("parallel", "arbitrary")),
        name="nsa_proj",
    )(hb, w_q, w_kv, w_g, expand)


def _compress_kernel(x_ref, pe_ref, w1_ref, b1_ref, w2_ref, o_ref, *, nh, half):
    x = x_ref[0, 0].astype(F32)
    xa = (x + pe_ref[0:1, :]).astype(BF16)
    xb = (x + pe_ref[1:2, :]).astype(BF16)
    ha = _dot(xa, w1_ref[0:half, :])
    hb = _dot(xb, w1_ref[half:2 * half, :])
    hid = ha + pltpu.roll(hb, nh - 1, 0) + b1_ref[...]
    c = math.sqrt(2.0 / math.pi)
    act = 0.5 * hid * (1.0 + jnp.tanh(c * (hid + 0.044715 * (hid * hid * hid))))
    o_ref[0, 0] = _dot(act.astype(BF16), w2_ref[...]).astype(o_ref.dtype)


def _compress(x, pe2, w1, b1, w2):
    bsz, g, nh, half = x.shape
    hid = w1.shape[1]
    hd = w2.shape[1]
    return pl.pallas_call(
        functools.partial(_compress_kernel, nh=nh, half=half),
        grid=(bsz, g),
        in_specs=[
            pl.BlockSpec((1, 1, nh, half), lambda b, gg: (b, gg, 0, 0)),
            pl.BlockSpec((2, half), lambda b, gg: (0, 0)),
            pl.BlockSpec((2 * half, hid), lambda b, gg: (0, 0)),
            pl.BlockSpec((1, hid), lambda b, gg: (0, 0)),
            pl.BlockSpec((hid, hd), lambda b, gg: (0, 0)),
        ],
        out_specs=pl.BlockSpec((1, 1, nh, hd), lambda b, gg: (b, gg, 0, 0)),
        out_shape=jax.ShapeDtypeStruct((bsz, g, nh, hd), BF16),
        compiler_params=_params(("parallel", "parallel")),
        name="compress",
    )(x, pe2, w1, b1, w2)


def _stack_heads(qt_ref):
    return jnp.concatenate([qt_ref[0, r] for r in range(NSA_REP)], axis=1)


def _per_head(x):
    return jnp.tile(x, (1, NSA_REP))


def _cmp_topk_kernel(qt_ref, kc_ref, vct_ref, ovt_ref, oct_ref, selt_ref, *, tq, nh, nb, n_sel, nq):
    i = pl.program_id(2)
    for part in range(CAUSAL_PARTS):
        lo, hi = part * nq // CAUSAL_PARTS, (part + 1) * nq // CAUSAL_PARTS

        @pl.when(jnp.logical_and(i >= lo, i < hi))
        def _(part=part):
            _cmp_topk_part(qt_ref, kc_ref, vct_ref, ovt_ref, oct_ref, selt_ref, i * tq, tq=tq,
                           nh=(part + 1) * nh // CAUSAL_PARTS, nb=(part + 1) * nb // CAUSAL_PARTS,
                           nb_all=nb, n_sel=n_sel)


def _cmp_topk_part(qt_ref, kc_ref, vct_ref, ovt_ref, oct_ref, selt_ref, t0, *, tq, nh, nb, nb_all, n_sel):
    cend = lax.broadcasted_iota(jnp.int32, (nh, tq), 0) * CMP_STRIDE + (CMP_BLOCK - 1)
    tpos = t0 + lax.broadcasted_iota(jnp.int32, (nh, tq), 1)
    bias = jnp.where(cend <= tpos, 0.0, NEG)
    seen = jnp.where(t0 + lax.broadcasted_iota(jnp.int32, (1, tq), 1) >= CMP_BLOCK - 1, 1.0, 0.0)
    qt = _stack_heads(qt_ref)
    s = _dot(kc_ref[0, 0, 0:nh, :], qt) + _per_head(bias)
    e = jnp.exp2(s - jnp.max(s, axis=0, keepdims=True))
    ea = _dot(vct_ref[0, 0, :, 0:nh], e.astype(BF16))
    inv = _per_head(seen) / jnp.maximum(ea[HEAD_DIM:HEAD_DIM + 1], TINY)
    oct_ref[0, 0] = ea[0:HEAD_DIM] * inv
    p = e * inv
    psum = p[:, 0:tq]
    for r in range(1, NSA_REP):
        psum = psum + p[:, r * tq:(r + 1) * tq]
    imp = _dot(ovt_ref[0:nb, 0:nh], psum.astype(BF16))
    blk = lax.broadcasted_iota(jnp.int32, (nb, tq), 0).astype(F32)
    cur = ((t0 + lax.broadcasted_iota(jnp.int32, (nb, tq), 1)) // SLC_BLOCK).astype(F32)
    forced = (blk == 0.0) | (blk == cur) | (blk == cur - 1.0)
    x = jnp.where(forced, -jnp.inf, jnp.where(blk <= cur, imp, NEG))
    for _ in range(n_sel - N_FORCED):
        mx = jnp.max(x, axis=0, keepdims=True)
        idx = jnp.min(jnp.where(x == mx, blk, float(nb)), axis=0, keepdims=True)
        x = jnp.where(blk == idx, -jnp.inf, x)
    selb = jnp.where(x == -jnp.inf, 0.0, NEG)
    selt_ref[0, 0, 0:nb, :] = selb.astype(selt_ref.dtype)
    if nb < nb_all:
        selt_ref[0, 0, nb:nb_all, :] = jnp.full((nb_all - nb, tq), NEG, selt_ref.dtype)


def _cmp_topk(qt, kc, vct, ovt):
    bsz, _, hd, s = qt.shape
    g, nh = kc.shape[1], kc.shape[2]
    nb = ovt.shape[0]
    assert nb >= SLC_TOPN and nh % CAUSAL_PARTS == 0 and nb % CAUSAL_PARTS == 0
    tq = min(Q_TILE, s)
    cols = NSA_REP * tq
    return pl.pallas_call(
        functools.partial(_cmp_topk_kernel, tq=tq, nh=nh, nb=nb, n_sel=SLC_TOPN, nq=s // tq),
        grid=(bsz, g, s // tq),
        in_specs=[
            pl.BlockSpec((1, NSA_REP, hd, tq), lambda b, gg, i: (b, gg, 0, i)),
            pl.BlockSpec((1, 1, nh, hd), lambda b, gg, i: (b, gg, 0, 0)),
            pl.BlockSpec((1, 1, hd + ONES_ROWS, nh), lambda b, gg, i: (b, gg, 0, 0)),
            pl.BlockSpec((nb, nh), lambda b, gg, i: (0, 0)),
        ],
        out_specs=[
            pl.BlockSpec((1, 1, hd, cols), lambda b, gg, i: (b, gg, 0, i)),
            pl.BlockSpec((1, 1, nb, tq), lambda b, gg, i: (b, gg, 0, i)),
        ],
        out_shape=[
            jax.ShapeDtypeStruct((bsz, g, hd, NSA_REP * s), F32),
            jax.ShapeDtypeStruct((bsz, g, nb, s), BF16),
        ],
        compiler_params=_params(("parallel", "parallel", "arbitrary")),
        name="cmp_topk",
    )(qt, kc, vct, ovt)


def _sel_win_kernel(qt_ref, ket_ref, vst_ref, kw_ref, vwt_ref, selt_ref, oct_ref, gatet_ref,
                    o_ref, rhs_sc, m_sc, acc_sc, sa_sc, sb_sc, ma_sc, mb_sc, sw_sc, mw_sc, ow_sc,
                    *, tq, tk, chunk, nvar, nchunk, wlen):
    t0 = pl.program_id(2) * tq
    qt = _stack_heads(qt_ref)
    for c in range(nchunk):
        selt = selt_ref[0, 0, c * chunk:(c + 1) * chunk, :]
        rhs_sc[c] = jnp.concatenate([_per_head(selt), qt], axis=0)

    m_sc[...] = jnp.full_like(m_sc, NEG)
    acc_sc[...] = jnp.zeros_like(acc_sc)

    cols = NSA_REP * tq
    part_cols = [slice(c * cols // COL_PARTS, (c + 1) * cols // COL_PARTS) for c in range(COL_PARTS)]

    def put_scores(slot, kt, causal, cs):
        s_ref, mx_ref = slot
        k0 = pl.multiple_of(kt * tk, tk)
        s = _dot(ket_ref[0, 0, pl.ds(k0, tk), :], rhs_sc[kt // nvar, :, cs])
        if causal:
            kpos = k0 + lax.broadcasted_iota(jnp.int32, (tk, tq), 0)
            tcol = t0 + lax.broadcasted_iota(jnp.int32, (tk, tq), 1)
            s = s + jnp.tile(jnp.where(kpos <= tcol, 0.0, NEG), (1, NSA_REP // COL_PARTS))
        s_ref[:, cs] = s
        mx_ref[:, cs] = jnp.max(s, axis=0, keepdims=True)

    def accumulate(slot, kt, cs):
        s_ref, mx_ref = slot
        k0 = pl.multiple_of(kt * tk, tk)
        m_old = m_sc[:, cs]
        m_new = jnp.maximum(m_old, mx_ref[:, cs])
        alpha = jnp.exp2(m_old - m_new)
        p = jnp.exp2(s_ref[:, cs] - m_new)
        acc_sc[:, cs] = alpha * acc_sc[:, cs] + _dot(vst_ref[0, 0, :, pl.ds(k0, tk)], p.astype(BF16))
        m_sc[:, cs] = m_new

    def put_all(slot, kt, causal):
        for cs in part_cols:
            put_scores(slot, kt, causal, cs)

    def accumulate_all(slot, kt):
        for cs in part_cols:
            accumulate(slot, kt, cs)

    def stage(dst, kt_new, causal, src, kt_old):
        for cs in part_cols:
            put_scores(dst, kt_new, causal, cs)
            accumulate(src, kt_old, cs)

    slot_a, slot_b = (sa_sc, ma_sc), (sb_sc, mb_sc)
    nfull = (t0 + tq - 1) // tk

    wk = wlen + tq
    w0 = pl.multiple_of(jnp.maximum(t0 - wlen, 0), tq)
    diff = (t0 + lax.broadcasted_iota(jnp.int32, (wk, tq), 1)) - (
        w0 + lax.broadcasted_iota(jnp.int32, (wk, tq), 0))
    wbias = jnp.where(jnp.logical_and(diff >= 0, diff < wlen), 0.0, NEG)
    sw = _dot(kw_ref[0, 0, pl.ds(w0, wk), :], qt) + _per_head(wbias)
    sw_sc[...] = sw
    mw_sc[...] = jnp.max(sw, axis=0, keepdims=True)

    def window_out():
        ew = jnp.exp2(sw_sc[...] - mw_sc[...])
        ow_sc[...] = _dot(vwt_ref[0, 0, :, pl.ds(w0, wk)], ew.astype(BF16))

    @pl.when(nfull == 0)
    def _():
        put_all(slot_a, 0, True)
        window_out()
        accumulate_all(slot_a, 0)

    @pl.when(nfull > 0)
    def _():
        put_all(slot_a, 0, False)
        window_out()

        slots = (slot_a, slot_b)

        def stages(first, count, last_causal):
            for u in range(count):
                stage(slots[(1 + u) % 2], first + u, last_causal and u == count - 1,
                      slots[u % 2], first + u - 1)

        def body(j, carry):
            stages(LOOP_STAGES * j + 1, LOOP_STAGES, False)
            return carry

        nloop = (nfull - 1) // LOOP_STAGES
        lax.fori_loop(0, nloop, body, 0)
        done = LOOP_STAGES * nloop
        for rest in range(LOOP_STAGES):

            @pl.when(nfull - 1 - done == rest)
            def _(rest=rest):
                stages(done + 1, rest + 1, True)
                accumulate_all(slots[(1 + rest) % 2], nfull)

    o_s = acc_sc[0:HEAD_DIM] * (1.0 / jnp.maximum(acc_sc[HEAD_DIM:HEAD_DIM + 1], TINY))

    o_w = ow_sc[0:HEAD_DIM] * (1.0 / jnp.maximum(ow_sc[HEAD_DIM:HEAD_DIM + 1], TINY))

    gate = gatet_ref[0, 0]
    o_c = oct_ref[0, 0]
    outs = []
    for r in range(NSA_REP):
        cs = slice(r * tq, (r + 1) * tq)
        outs.append(gate[3 * r:3 * r + 1] * o_c[:, cs] + gate[3 * r + 1:3 * r + 2] * o_s[:, cs]
                    + gate[3 * r + 2:3 * r + 3] * o_w[:, cs])
    pairs = [jnp.concatenate(outs[r:r + 2], axis=0).T for r in range(0, NSA_REP, 2)]
    o_ref[0] = jnp.concatenate(pairs, axis=1).astype(o_ref.dtype)


def _sel_win(qt, ket, vst, kw, vwt, selt, oct, gatet, tk):
    bsz, _, hd, s = qt.shape
    g = ket.shape[1]
    nb = selt.shape[2]
    chunk = ket.shape[3] - hd
    nchunk = nb // chunk
    nvar = chunk * SLC_BLOCK // tk
    tq = min(Q_TILE, s)
    cols = NSA_REP * tq
    resident = lambda shape: pl.BlockSpec(shape, lambda b, gg, i: (b, gg, 0, 0),
                                          pipeline_mode=pl.Buffered(1))
    return pl.pallas_call(
        functools.partial(_sel_win_kernel, tq=tq, tk=tk, chunk=chunk, nvar=nvar, nchunk=nchunk,
                          wlen=WINDOW),
        grid=(bsz, g, s // tq),
        in_specs=[
            pl.BlockSpec((1, NSA_REP, hd, tq), lambda b, gg, i: (b, gg, 0, i)),
            resident((1, 1, s, chunk + hd)),
            resident((1, 1, hd + ONES_ROWS, s)),
            resident((1, 1, s, hd)),
            resident((1, 1, hd + ONES_ROWS, s)),
            pl.BlockSpec((1, 1, nb, tq), lambda b, gg, i: (b, gg, 0, i)),
            pl.BlockSpec((1, 1, hd, cols), lambda b, gg, i: (b, gg, 0, i)),
            pl.BlockSpec((1, 1, GATE_ROWS, tq), lambda b, gg, i: (b, gg, 0, i)),
        ],
        out_specs=pl.BlockSpec((1, tq, NSA_REP * hd), lambda b, gg, i: (b, i, gg)),
        out_shape=jax.ShapeDtypeStruct((bsz, s, g * NSA_REP * hd), BF16),
        scratch_shapes=[
            pltpu.VMEM((nchunk, chunk + hd, cols), BF16),
            pltpu.VMEM((1, cols), F32),
            pltpu.VMEM((hd + ONES_ROWS, cols), F32),
            pltpu.VMEM((tk, cols), F32),
            pltpu.VMEM((tk, cols), F32),
            pltpu.VMEM((1, cols), F32),
            pltpu.VMEM((1, cols), F32),
            pltpu.VMEM((WINDOW + tq, cols), F32),
            pltpu.VMEM((1, cols), F32),
            pltpu.VMEM((hd + ONES_ROWS, cols), F32),
        ],
        compiler_params=_params(("parallel", "parallel", "arbitrary")),
        name="sel_win",
    )(qt, ket, vst, kw, vwt, selt, oct, gatet)


def _retention_kernel(q_ref, k_ref, v_ref, gb_ref, cos_ref, sin_ref, dmat_ref, qdec_ref, kdec_ref,
                      cdec_ref, gng_ref, gnb_ref, o_ref, state_sc, *, nchunk, c):
    @pl.when(pl.program_id(2) == 0)
    def _():
        state_sc[...] = jnp.zeros_like(state_sc)

    kscale = RET_DIM ** -0.5
    half = RET_DIM // 2
    for n in range(nchunk):
        rows = pl.ds(n * c, c)
        cos = cos_ref[rows, :]
        sin = sin_ref[rows, :]
        for hh in range(RET_HEADS_PER_STEP):
            cs = slice(hh * RET_DIM, (hh + 1) * RET_DIM)
            qf = q_ref[0, rows, cs].astype(F32)
            kf = k_ref[0, rows, cs].astype(F32)
            qr = qf * cos + pltpu.roll(qf, half, 1) * sin
            kr = (kf * cos + pltpu.roll(kf, half, 1) * sin) * kscale
            v = v_ref[0, rows, cs]
            inner = _dot_nt(qr.astype(BF16), kr.astype(BF16)) * dmat_ref[hh]
            state = state_sc[hh]
            y = _dot(inner.astype(BF16), v) + _dot((qr * qdec_ref[hh]).astype(BF16), state.astype(BF16))
            state_sc[hh] = state * cdec_ref[hh] + _dot_tn((kr * kdec_ref[hh]).astype(BF16), v)
            mu = jnp.mean(y, axis=-1, keepdims=True)
            d = y - mu
            var = jnp.mean(d * d, axis=-1, keepdims=True)
            yn = d * lax.rsqrt(var + LN_EPS) * gng_ref[:, cs] + gnb_ref[:, cs]
            gb = gb_ref[0, rows, cs].astype(F32)
            o_ref[0, rows, cs] = ((gb * _sigmoid(gb)) * yn).astype(o_ref.dtype)


def _retention(ret, cos2, sin2, dmat, qdec, kdec, cdec, gng, gnb):
    bsz, s, _ = ret.shape
    h = RET_HEADS
    c = RET_CHUNK
    tc = min(RET_TILE, s)
    hs = RET_HEADS_PER_STEP
    ng = h // hs
    w = hs * RET_DIM
    spec = lambda off: pl.BlockSpec((1, tc, w), lambda b, hh, j, off=off: (b, j, off + hh))
    hspec = lambda shp: pl.BlockSpec((hs,) + shp, lambda b, hh, j: (hh, 0, 0))
    return pl.pallas_call(
        functools.partial(_retention_kernel, nchunk=tc // c, c=c),
        grid=(bsz, ng, s // tc),
        in_specs=[
            spec(0), spec(ng), spec(2 * ng), spec(3 * ng),
            pl.BlockSpec((tc, RET_DIM), lambda b, hh, j: (j, 0)),
            pl.BlockSpec((tc, RET_DIM), lambda b, hh, j: (j, 0)),
            hspec((c, c)), hspec((c, RET_DIM)), hspec((c, RET_DIM)), hspec((1, RET_DIM)),
            pl.BlockSpec((1, w), lambda b, hh, j: (0, hh)),
            pl.BlockSpec((1, w), lambda b, hh, j: (0, hh)),
        ],
        out_specs=pl.BlockSpec((1, tc, w), lambda b, hh, j: (b, j, hh)),
        out_shape=jax.ShapeDtypeStruct((bsz, s, h * RET_DIM), BF16),
        scratch_shapes=[pltpu.VMEM((hs, RET_DIM, RET_DIM), F32)],
        compiler_params=_params(("parallel", "parallel", "arbitrary")),
        name="retention",
    )(ret, ret, ret, ret, cos2, sin2, dmat, qdec, kdec, cdec, gng, gnb)


def _merge_kernel(h_ref, oa_ref, ob_ref, wga_ref, wgb_ref, wpa_ref, wpb_ref, o_ref):
    hb = h_ref[...]
    ga = _sigmoid(_dot(hb, wga_ref[...]))
    gb = _sigmoid(_dot(hb, wgb_ref[...]))
    merged = ga * _dot(oa_ref[...], wpa_ref[...]) + gb * _dot(ob_ref[...], wpb_ref[...])
    o_ref[...] = merged.astype(o_ref.dtype)


def _merge(hb, oa, ob, wga, wgb, wpa, wpb):
    n, d = hb.shape
    bm = min(PROJ_ROW_TILE, n)
    bn = min(COL_TILE, d)
    ka, kb = oa.shape[1], ob.shape[1]
    return pl.pallas_call(
        _merge_kernel,
        grid=(n // bm, d // bn),
        in_specs=[
            pl.BlockSpec((bm, d), lambda i, j: (i, 0)),
            pl.BlockSpec((bm, ka), lambda i, j: (i, 0)),
            pl.BlockSpec((bm, kb), lambda i, j: (i, 0)),
            pl.BlockSpec((d, bn), lambda i, j: (0, j)),
            pl.BlockSpec((d, bn), lambda i, j: (0, j)),
            pl.BlockSpec((ka, bn), lambda i, j: (0, j)),
            pl.BlockSpec((kb, bn), lambda i, j: (0, j)),
        ],
        out_specs=pl.BlockSpec((bm, bn), lambda i, j: (i, j)),
        out_shape=jax.ShapeDtypeStruct((n, d), BF16),
        compiler_params=_params(("parallel", "arbitrary")),
        name="merge",
    )(hb, oa, ob, wga, wgb, wpa, wpb)


def _proj_ln_kernel(x_ref, m_ref, w_ref, g_ref, b_ref, o_ref, *, alpha):
    y = alpha * x_ref[...] + _dot(m_ref[...], w_ref[...])
    o_ref[...] = _layer_norm_rows(y, g_ref[...], b_ref[...])


def _proj_ln(x, m, w, g, b, alpha):
    n, d = x.shape
    bm = min(ROW_TILE, n)
    return pl.pallas_call(
        functools.partial(_proj_ln_kernel, alpha=alpha),
        grid=(n // bm,),
        in_specs=[
            pl.BlockSpec((bm, d), lambda i: (i, 0)),
            pl.BlockSpec((bm, d), lambda i: (i, 0)),
            pl.BlockSpec((d, d), lambda i: (0, 0)),
            pl.BlockSpec((1, d), lambda i: (0, 0)),
            pl.BlockSpec((1, d), lambda i: (0, 0)),
        ],
        out_specs=pl.BlockSpec((bm, d), lambda i: (i, 0)),
        out_shape=jax.ShapeDtypeStruct((n, d), F32),
        compiler_params=_params(("parallel",)),
        name="proj_ln",
    )(x, m, w, g, b)


def _overlap_matrix(nh, nb):
    c0 = np.arange(nh) * CMP_STRIDE
    c1 = c0 + CMP_BLOCK
    s0 = np.arange(nb) * SLC_BLOCK
    s1 = s0 + SLC_BLOCK
    ov = (c0[:, None] < s1[None, :]) & (c1[:, None] > s0[None, :])
    ov[nh - 1, :] = False
    return jnp.asarray(ov.T, BF16)


def _expand_matrix(nb, s):
    chunk = min(SEL_CHUNK, nb)
    blk_in_chunk = (np.arange(s) // SLC_BLOCK) % chunk
    return jnp.asarray(blk_in_chunk[:, None] == np.arange(chunk)[None, :], BF16)


def _retention_tables(s):
    h, c, d = RET_HEADS, RET_CHUNK, RET_DIM
    inv = ROPE_BASE ** (-jnp.arange(0, d, 2, dtype=F32) / d)
    ang = jnp.arange(s)[:, None].astype(F32) * inv[None, :]
    cos, sin = jnp.cos(ang), jnp.sin(ang)
    cos2 = jnp.concatenate([cos, cos], -1)
    sin2 = jnp.concatenate([-sin, sin], -1)
    log_g = jnp.log1p(-jnp.exp2(-5.0 - jnp.arange(h, dtype=F32)))
    i = jnp.arange(c, dtype=F32)
    diff = i[:, None] - i[None, :]
    dmat = jnp.where(diff >= 0, jnp.exp(jnp.maximum(diff, 0.0)[None] * log_g[:, None, None]), 0.0)
    kdec = jnp.exp((c - 1 - i)[None, :] * log_g[:, None])
    qdec = jnp.exp((i + 1)[None, :] * log_g[:, None])
    cdec = jnp.exp(c * log_g)
    bc = lambda t: jnp.broadcast_to(t[:, :, None], (h, c, d))
    return cos2, sin2, dmat, bc(qdec), bc(kdec), jnp.broadcast_to(cdec[:, None, None], (h, 1, d))


def _pad_cols(w, mult):
    pad = (-w.shape[1]) % mult
    return jnp.pad(w, ((0, 0), (0, pad))) if pad else w


def _mixer(hf, hb, bsz, s, w_in, cmp_k, cmp_v, ret_gn_g, ret_gn_b, w_merge_gate, w_proj_a, w_proj_b,
           w_o, ln_g, ln_b, alpha, tables):
    n, d = hf.shape
    g, hd = NSA_GROUPS, HEAD_DIM
    ovt, expand, ret_tabs = tables
    o_nsa = NSA_Q + 6 * NSA_KV
    w_q = w_in[:, :NSA_Q].astype(BF16)
    w_kv = w_in[:, NSA_Q:o_nsa].astype(BF16)
    w_gate = w_in[:, o_nsa:o_nsa + NSA_GATE].reshape(d, g, 3 * NSA_REP)
    w_gate = jnp.pad(w_gate, ((0, 0), (0, 0), (0, GATE_ROWS - 3 * NSA_REP))).reshape(d, g * GATE_ROWS)
    w_gate = _pad_cols(w_gate, LANE).astype(BF16)
    w_ret = w_in[:, o_nsa + NSA_GATE:].astype(BF16)

    qt, kc_in, vc_in, ket, vst, kw, vwt, gatet = _nsa_proj(hb, w_q, w_kv, w_gate, expand, bsz, s)
    ret = _matmul(hb, w_ret, BF16, "proj_ret")

    nh = s // CMP_STRIDE
    half = CMP_STRIDE * hd

    def compress(t, prm):
        pe, w1, b1, w2 = prm
        return _compress(t.reshape(bsz, g, nh, half), pe.reshape(2, half), w1.astype(BF16),
                         b1.reshape(1, -1), w2.astype(BF16))

    kc = compress(kc_in, cmp_k)
    ones_rows = jnp.zeros((bsz, g, ONES_ROWS, nh), BF16).at[:, :, 0].set(1.0)
    vct = jnp.concatenate([jnp.transpose(compress(vc_in, cmp_v), (0, 1, 3, 2)), ones_rows], axis=2)
    oct, selt = _cmp_topk(qt, kc, vct, ovt)

    o_a = _sel_win(qt, ket, vst, kw, vwt, selt, oct, gatet, min(K_TILE, s))

    o_b = _retention(ret.reshape(bsz, s, 4 * RET_W), *ret_tabs,
                     ret_gn_g.reshape(1, -1), ret_gn_b.reshape(1, -1))

    merged = _merge(hb, o_a.reshape(n, NSA_Q), o_b.reshape(n, RET_W),
                    w_merge_gate[:, :d].astype(BF16), w_merge_gate[:, d:].astype(BF16),
                    w_proj_a.astype(BF16), w_proj_b.astype(BF16))
    return _proj_ln(hf, merged, w_o.astype(BF16), ln_g.reshape(1, -1), ln_b.reshape(1, -1), alpha)


def kernel(x, ffn1_w_gate, ffn1_w_up, ffn1_w_down, ln1_g, ln1_b, w_in, cmp_k_pe, cmp_k_w1, cmp_k_b1,
           cmp_k_w2, cmp_v_pe, cmp_v_w1, cmp_v_b1, cmp_v_w2, ret_gn_g, ret_gn_b, w_merge_gate, w_proj_a,
           w_proj_b, w_o, ln2_g, ln2_b, ffn2_w_gate, ffn2_w_up, ffn2_w_down, ln3_g, ln3_b):
    bsz, s, d = x.shape
    depth = ffn1_w_gate.shape[0]
    alpha = (2 * depth) ** 0.25
    nb = s // SLC_BLOCK
    tables = (_overlap_matrix(s // CMP_STRIDE, nb), _expand_matrix(nb, s), _retention_tables(s))
    ffn_weights = lambda wg, wu, wd: (wg.astype(BF16), wu.astype(BF16), wd.astype(BF16))

    row = lambda t: t.reshape(1, -1)
    xf = x.reshape(bsz * s, d)
    for l in range(depth):
        hf, hb = _ffn_ln(xf, *ffn_weights(ffn1_w_gate[l], ffn1_w_up[l], ffn1_w_down[l]),
                         row(ln1_g[l]), row(ln1_b[l]), alpha, True)
        xf = _mixer(hf, hb, bsz, s, w_in[l],
                    (cmp_k_pe[l], cmp_k_w1[l], cmp_k_b1[l], cmp_k_w2[l]),
                    (cmp_v_pe[l], cmp_v_w1[l], cmp_v_b1[l], cmp_v_w2[l]),
                    ret_gn_g[l], ret_gn_b[l], w_merge_gate[l], w_proj_a[l], w_proj_b[l], w_o[l],
                    ln2_g[l], ln2_b[l], alpha, tables)
        xf, _ = _ffn_ln(xf, *ffn_weights(ffn2_w_gate[l], ffn2_w_up[l], ffn2_w_down[l]),
                        row(ln3_g[l]), row(ln3_b[l]), alpha, False)
    return xf.reshape(bsz, s, d)
```
